```python
import math
import jax
import jax.numpy as jnp
from jax import lax
import numpy as np

D_MODEL = 2048
BATCH = 16
SEQ = 2048
DEPTH = 4

CTX_LEN = 256
GRID_W = 64
N_EVEN = (DEPTH + 1) // 2
N_ODD = DEPTH // 2
D_FF = 4 * D_MODEL
N_MOD = 6
EPS = 1e-6
CONV_W = 3

A_HEAD_DIM = 64
A_WIDTH = D_MODEL // 2
A_HEADS = A_WIDTH // A_HEAD_DIM
A_GROUPS = 2
A_STATE = 128
A_CONV_DIM = A_WIDTH + 2 * A_GROUPS * A_STATE
A_CHUNK = 128

B_WIDTH = D_MODEL // 2
B_HEADS = 8
B_VDIM = B_WIDTH // B_HEADS
B_KDIM = 128
B_FDIM = B_HEADS * B_KDIM
B_CHUNK = 64

C_HEADS = 4
C_V_WIDTH = D_MODEL
C_VDIM = C_V_WIDTH // C_HEADS
C_QKDIM = C_VDIM // 2
C_QK_WIDTH = C_HEADS * C_QKDIM
C_CHUNK = 128

EVEN_SIZES = (A_WIDTH, A_CONV_DIM, 2 * A_HEADS, B_FDIM, 2 * B_FDIM, B_WIDTH, B_WIDTH)
EVEN_IN = sum(EVEN_SIZES)
ODD_SIZES = (2 * C_QK_WIDTH, C_V_WIDTH, C_V_WIDTH, 2 * C_HEADS, 2 * C_HEADS)
ODD_IN = sum(ODD_SIZES)

kernel_name = "bidir_ssd_hgrn2_mlstm_prefix_dit"


def _split(u, sizes):
    return jnp.split(u, [int(s) for s in np.cumsum(sizes)[:-1]], axis=-1)


def rmsnorm(x, w):
    xf = x.astype(jnp.float32)
    y = xf * lax.rsqrt(jnp.mean(xf * xf, axis=-1, keepdims=True) + EPS)
    return (y * w.astype(jnp.float32)).astype(x.dtype)


def group_rmsnorm(y, w, groups):
    shp = y.shape
    yf = y.astype(jnp.float32).reshape(shp[:-1] + (groups, shp[-1] // groups))
    yf = yf * lax.rsqrt(jnp.mean(yf * yf, axis=-1, keepdims=True) + EPS)
    return (yf.reshape(shp) * w.astype(jnp.float32)).astype(y.dtype)


def _dwconv(u, w, b):
    r = CONV_W // 2
    l = u.shape[-2]
    up = jnp.pad(u, [(0, 0)] * (u.ndim - 2) + [(r, r), (0, 0)])
    out = up[..., 0:l, :] * w[0]
    for j in range(1, CONV_W):
        out = out + up[..., j:j + l, :] * w[j]
    return out + b


def short_conv(u_ctx, u_lat, w, b):
    bsz, s, ch = u_lat.shape
    rows = s // GRID_W
    lat = _dwconv(u_lat.reshape(bsz, rows, GRID_W, ch), w, b).reshape(bsz, s, ch)
    return _dwconv(u_ctx, w, b), lat


def _to_chunks(t, n):
    b, l = t.shape[:2]
    return jnp.moveaxis(t.reshape((b, l // n, n) + t.shape[2:]), 1, 0)


def _from_chunks(t):
    nc, b, n = t.shape[:3]
    return jnp.moveaxis(t, 0, 1).reshape((b, nc * n) + t.shape[3:])


def _segsum(a):
    t = a.shape[-1]
    cs = jnp.cumsum(a, axis=-1)
    return jnp.where(jnp.tril(jnp.ones((t, t), dtype=bool)), cs[..., :, None] - cs[..., None, :], -jnp.inf)


def ssd_scan(inputs, h0):
    xdt, a, bm, cm = (t.astype(jnp.float32) for t in inputs)
    bsz, l, h, p = xdt.shape
    g, n = bm.shape[2], bm.shape[3]
    r = h // g
    nc = l // A_CHUNK
    xr = xdt.reshape(bsz, nc, A_CHUNK, g, r, p)
    ar = a.reshape(bsz, nc, A_CHUNK, g, r).transpose(0, 3, 4, 1, 2)
    br = bm.reshape(bsz, nc, A_CHUNK, g, n)
    cr = cm.reshape(bsz, nc, A_CHUNK, g, n)
    a_cs = jnp.cumsum(ar, axis=-1)
    y_diag = jnp.einsum('bclgn,bcsgn,bgrcls,bcsgrp->bclgrp', cr, br, jnp.exp(_segsum(ar)), xr)
    states = jnp.einsum('bclgn,bgrcl,bclgrp->bcgrpn', br, jnp.exp(a_cs[..., -1:] - a_cs), xr)
    states = jnp.concatenate([h0.reshape(bsz, g, r, p, n)[:, None], states], axis=1)
    chunk_a = jnp.pad(a_cs[..., -1], ((0, 0), (0, 0), (0, 0), (1, 0)))
    states = jnp.einsum('bgrzc,bcgrpn->bzgrpn', jnp.exp(_segsum(chunk_a)), states)
    y_off = jnp.einsum('bclgn,bcgrpn,bgrcl->bclgrp', cr, states[:, :-1], jnp.exp(a_cs))
    return (y_diag + y_off).reshape(bsz, l, h, p), states[:, -1].reshape(bsz, h, p, n)


def hgrn2_scan(inputs, s0):
    q, logf, k, v = (t.astype(jnp.float32) for t in inputs)
    mask = jnp.tril(jnp.ones((B_CHUNK, B_CHUNK), dtype=bool))[None, :, :, None, None]

    def step(state, inp):
        qc, gc, kc, vc = inp
        bcum = jnp.cumsum(gc, axis=1)
        rel = jnp.where(mask, bcum[:, :, None] - bcum[:, None, :], -jnp.inf)
        att = jnp.einsum('bthk,btshk,bshk->bhts', qc, jnp.exp(rel), kc)
        o = jnp.einsum('bhts,bshv->bthv', att, vc) + jnp.einsum('bthk,bhkv->bthv', qc * jnp.exp(bcum), state)
        blast = bcum[:, -1]
        state = jnp.exp(blast)[..., None] * state + jnp.einsum('bshk,bshv->bhkv', kc * jnp.exp(blast[:, None] - bcum), vc)
        return state, o

    state, o = lax.scan(step, s0, tuple(_to_chunks(t, B_CHUNK) for t in (q, logf, k, v)))
    return _from_chunks(o), state


def mlstm_scan(inputs, state0):
    q, k, v, logi, logf = (t.astype(jnp.float32) for t in inputs)
    mask = jnp.tril(jnp.ones((C_CHUNK, C_CHUNK), dtype=bool))

    def step(carry, inp):
        cmat, nvec, m = carry
        qc, kc, vc, ic, fc = inp
        bcum = jnp.cumsum(fc, axis=1).transpose(0, 2, 1)
        ih = ic.transpose(0, 2, 1)
        logd = jnp.where(mask, bcum[..., :, None] - bcum[..., None, :] + ih[..., None, :], -jnp.inf)
        gstate = bcum + m[..., None]
        mt = jnp.maximum(jnp.max(logd, axis=-1), gstate)
        w = jnp.einsum('bthd,bshd->bhts', qc, kc) * jnp.exp(logd - mt[..., None])
        sw = jnp.exp(gstate - mt)
        num = jnp.einsum('bhts,bshv->bhtv', w, vc) + sw[..., None] * jnp.einsum('bthd,bhdv->bhtv', qc, cmat)
        den = jnp.sum(w, axis=-1) + sw * jnp.einsum('bthd,bhd->bht', qc, nvec)
        hout = num / jnp.maximum(jnp.abs(den), jnp.exp(-mt))[..., None]
        blast = bcum[..., -1]
        logw = blast[..., None] - bcum + ih
        m_new = jnp.maximum(blast + m, jnp.max(logw, axis=-1))
        ws = jnp.exp(logw - m_new[..., None])
        cs = jnp.exp(blast + m - m_new)
        cmat = cs[..., None, None] * cmat + jnp.einsum('bhs,bshd,bshv->bhdv', ws, kc, vc)
        nvec = cs[..., None] * nvec + jnp.einsum('bhs,bshd->bhd', ws, kc)
        return (cmat, nvec, m_new), hout.transpose(0, 2, 1, 3)

    state, hs = lax.scan(step, state0, tuple(_to_chunks(t, C_CHUNK) for t in (q, k, v, logi, logf)))
    return _from_chunks(hs), state


def _flip_seq(tree):
    return tuple(jnp.flip(t, axis=1) for t in tree)


def bidirectional(scan_fn, ctx_f, ctx_b, lat_f, lat_b, init):
    yc_f, s_f = scan_fn(ctx_f, init)
    yl_f, _ = scan_fn(lat_f, s_f)
    yc_b, s_b = scan_fn(_flip_seq(ctx_b), init)
    yl_b, _ = scan_fn(_flip_seq(lat_b), s_b)
    return yc_f + jnp.flip(yc_b, axis=1), yl_f + jnp.flip(yl_b, axis=1)


def even_mixer(h_c, h_l, w_in, w_out, conv_w, conv_b, a_log, dt_bias, d_skip, ssd_norm_w, lb, hgrn_norm_w, need_ctx):
    bsz = h_l.shape[0]
    z_c, xbc_c, dt_c, q_c, f_c, i_c, g_c = _split(h_c @ w_in, EVEN_SIZES)
    z_l, xbc_l, dt_l, q_l, f_l, i_l, g_l = _split(h_l @ w_in, EVEN_SIZES)

    xbc_c, xbc_l = short_conv(xbc_c, xbc_l, conv_w, conv_b)
    xbc_c, xbc_l = jax.nn.silu(xbc_c), jax.nn.silu(xbc_l)

    def ssd_streams(xbc, dt_raw):
        bs, l, _ = xbc.shape
        xs, bm, cm = _split(xbc, (A_WIDTH, A_GROUPS * A_STATE, A_GROUPS * A_STATE))
        xs = xs.reshape(bs, l, A_HEADS, A_HEAD_DIM)
        bm = bm.reshape(bs, l, A_GROUPS, A_STATE)
        cm = cm.reshape(bs, l, A_GROUPS, A_STATE)
        dirs = []
        for d in range(2):
            dt = jax.nn.softplus(dt_raw[..., d * A_HEADS:(d + 1) * A_HEADS].astype(jnp.float32) + dt_bias[d])
            dirs.append((xs * dt[..., None], -jnp.exp(a_log[d].astype(jnp.float32)) * dt, bm, cm))
        return xs, dirs[0], dirs[1]

    xs_c, sc_f, sc_b = ssd_streams(xbc_c, dt_c)
    xs_l, sl_f, sl_b = ssd_streams(xbc_l, dt_l)
    h0 = jnp.zeros((bsz, A_HEADS, A_HEAD_DIM, A_STATE), jnp.float32)
    ya_c, ya_l = bidirectional(ssd_scan, sc_f, sc_b, sl_f, sl_b, h0)

    def ssd_out(y, xs, z):
        y = y + xs * d_skip[:, None]
        bs, l = y.shape[:2]
        y = y.reshape(bs, l, A_WIDTH).astype(z.dtype)
        return group_rmsnorm(y * jax.nn.silu(z), ssd_norm_w, A_GROUPS)

    log_lb = jnp.log(lb)
    log_1mlb = jnp.log1p(-lb)

    def hgrn_streams(q_raw, f_raw, i_raw):
        bs, l, _ = q_raw.shape
        q = jax.nn.silu(q_raw).reshape(bs, l, B_HEADS, B_KDIM)
        v = i_raw.reshape(bs, l, B_HEADS, B_VDIM)
        dirs = []
        for d in range(2):
            zf = f_raw[..., d * B_FDIM:(d + 1) * B_FDIM].astype(jnp.float32)
            logf = jnp.logaddexp(log_lb, log_1mlb + jax.nn.log_sigmoid(zf))
            kin = (1.0 - lb) * jax.nn.sigmoid(-zf)
            dirs.append((q, logf.reshape(bs, l, B_HEADS, B_KDIM), kin.reshape(bs, l, B_HEADS, B_KDIM), v))
        return dirs[0], dirs[1]

    hc_f, hc_b = hgrn_streams(q_c, f_c, i_c)
    hl_f, hl_b = hgrn_streams(q_l, f_l, i_l)
    s0 = jnp.zeros((bsz, B_HEADS, B_KDIM, B_VDIM), jnp.float32)
    yb_c, yb_l = bidirectional(hgrn2_scan, hc_f, hc_b, hl_f, hl_b, s0)

    def hgrn_out(o, g):
        bs, l = o.shape[:2]
        o = o.reshape(bs, l, B_WIDTH).astype(g.dtype)
        return group_rmsnorm(o, hgrn_norm_w, B_HEADS) * jax.nn.silu(g)

    out_l = jnp.concatenate([ssd_out(ya_l, xs_l, z_l), hgrn_out(yb_l, g_l)], axis=-1) @ w_out
    out_c = None
    if need_ctx:
        out_c = jnp.concatenate([ssd_out(ya_c, xs_c, z_c), hgrn_out(yb_c, g_c)], axis=-1) @ w_out
    return out_c, out_l


def odd_mixer(h_c, h_l, w_in, w_out, conv_w, conv_b, gate_b, norm_w, need_ctx):
    bsz = h_l.shape[0]
    qk_c, v_c, o_c, i_c, f_c = _split(h_c @ w_in, ODD_SIZES)
    qk_l, v_l, o_l, i_l, f_l = _split(h_l @ w_in, ODD_SIZES)
    qk_c, qk_l = short_conv(qk_c, qk_l, conv_w, conv_b)

    def streams(qk, v, i_raw, f_raw):
        bs, l, _ = qk.shape
        q, k = _split(jax.nn.silu(qk), (C_QK_WIDTH, C_QK_WIDTH))
        q = q.reshape(bs, l, C_HEADS, C_QKDIM)
        k = k.reshape(bs, l, C_HEADS, C_QKDIM) * (C_QKDIM ** -0.5)
        v = v.reshape(bs, l, C_HEADS, C_VDIM)
        dirs = []
        for d in range(2):
            logi = i_raw[..., d * C_HEADS:(d + 1) * C_HEADS].astype(jnp.float32) + gate_b[d]
            logf = jax.nn.log_sigmoid(f_raw[..., d * C_HEADS:(d + 1) * C_HEADS].astype(jnp.float32) + gate_b[2 + d])
            dirs.append((q, k, v, logi, logf))
        return dirs[0], dirs[1]

    c_f, c_b = streams(qk_c, v_c, i_c, f_c)
    l_f, l_b = streams(qk_l, v_l, i_l, f_l)
    init = (jnp.zeros((bsz, C_HEADS, C_QKDIM, C_VDIM), jnp.float32),
            jnp.zeros((bsz, C_HEADS, C_QKDIM), jnp.float32),
            jnp.zeros((bsz, C_HEADS), jnp.float32))
    hc, hl = bidirectional(mlstm_scan, c_f, c_b, l_f, l_b, init)

    def readout(hh, o_raw):
        bs, l = hh.shape[:2]
        hh = hh.reshape(bs, l, C_V_WIDTH).astype(o_raw.dtype)
        return (group_rmsnorm(hh, norm_w, C_HEADS) * jax.nn.sigmoid(o_raw)) @ w_out

    out_l = readout(hl, o_l)
    out_c = readout(hc, o_c) if need_ctx else None
    return out_c, out_l


def squared_relu_mlp(h, w1, w2):
    return jnp.square(jax.nn.relu(h @ w1)) @ w2


def setup_inputs(seed: int = 0) -> dict:
    key = jax.random.key(seed)
    ks = iter(jax.random.split(key, 32))

    def nrm(shape, scale):
        return jax.random.normal(next(ks), shape, jnp.float32) * scale

    def unif(shape, lo, hi):
        return jax.random.uniform(next(ks), shape, jnp.float32, minval=lo, maxval=hi)

    dt0 = jnp.exp(unif((N_EVEN, 2, A_HEADS), math.log(1e-3), math.log(1e-1)))
    return {
        "x": nrm((BATCH, SEQ, D_MODEL), 1.0),
        "c": nrm((BATCH, D_MODEL), 1.0),
        "ctx": nrm((BATCH, CTX_LEN, D_MODEL), 1.0),
        "c_ctx": nrm((D_MODEL,), 1.0),
        "mod_w": nrm((DEPTH, D_MODEL, N_MOD * D_MODEL), D_MODEL ** -0.5),
        "mod_b": nrm((DEPTH, N_MOD * D_MODEL), 0.02),
        "norm_w": 1.0 + nrm((DEPTH, 2, D_MODEL), 0.1),
        "final_norm_w": 1.0 + nrm((D_MODEL,), 0.1),
        "mlp_w1": nrm((DEPTH, D_MODEL, D_FF), D_MODEL ** -0.5),
        "mlp_w2": nrm((DEPTH, D_FF, D_MODEL), D_FF ** -0.5),
        "even_w_in": nrm((N_EVEN, D_MODEL, EVEN_IN), D_MODEL ** -0.5),
        "even_w_out": nrm((N_EVEN, A_WIDTH + B_WIDTH, D_MODEL), (A_WIDTH + B_WIDTH) ** -0.5),
        "ssd_conv_w": nrm((N_EVEN, CONV_W, A_CONV_DIM), CONV_W ** -0.5),
        "ssd_conv_b": nrm((N_EVEN, A_CONV_DIM), 0.02),
        "ssd_a_log": jnp.log(unif((N_EVEN, 2, A_HEADS), 1.0, 16.0)),
        "ssd_dt_bias": dt0 + jnp.log(-jnp.expm1(-dt0)),
        "ssd_d": 1.0 + nrm((N_EVEN, A_HEADS), 0.1),
        "ssd_norm_w": 1.0 + nrm((N_EVEN, A_WIDTH), 0.1),
        "hgrn_lb": nrm((N_EVEN, B_FDIM), 0.5),
        "hgrn_norm_w": 1.0 + nrm((N_EVEN, B_WIDTH), 0.1),
        "odd_w_in": nrm((N_ODD, D_MODEL, ODD_IN), D_MODEL ** -0.5),
        "odd_w_out": nrm((N_ODD, C_V_WIDTH, D_MODEL), C_V_WIDTH ** -0.5),
        "mlstm_conv_w": nrm((N_ODD, CONV_W, 2 * C_QK_WIDTH), CONV_W ** -0.5),
        "mlstm_conv_b": nrm((N_ODD, 2 * C_QK_WIDTH), 0.02),
        "mlstm_gate_b": jnp.concatenate([nrm((N_ODD, 2, C_HEADS), 0.1), unif((N_ODD, 2, C_HEADS), 3.0, 6.0)], axis=1),
        "mlstm_norm_w": 1.0 + nrm((N_ODD, C_V_WIDTH), 0.1),
    }


def reference(x, c, ctx, c_ctx, mod_w, mod_b, norm_w, final_norm_w, mlp_w1, mlp_w2,
              even_w_in, even_w_out, ssd_conv_w, ssd_conv_b, ssd_a_log, ssd_dt_bias, ssd_d, ssd_norm_w,
              hgrn_lb, hgrn_norm_w,
              odd_w_in, odd_w_out, mlstm_conv_w, mlstm_conv_b, mlstm_gate_b, mlstm_norm_w):
    lb_all = jnp.cumsum(jax.nn.softmax(hgrn_lb.astype(jnp.float32), axis=0), axis=0)
    lb_all = lb_all - lb_all[0]
    xc = ctx
    for layer in range(DEPTH):
        need_ctx = layer < DEPTH - 1
        mod_l = (jax.nn.silu(c) @ mod_w[layer] + mod_b[layer])[:, None, :]
        mod_c = jax.nn.silu(c_ctx) @ mod_w[layer] + mod_b[layer]
        sh1, sc1, g1, sh2, sc2, g2 = jnp.split(mod_l, N_MOD, axis=-1)
        csh1, csc1, cg1, csh2, csc2, cg2 = jnp.split(mod_c, N_MOD, axis=-1)
        h_l = rmsnorm(x, norm_w[layer, 0]) * (1 + sc1) + sh1
        h_c = rmsnorm(xc, norm_w[layer, 0]) * (1 + csc1) + csh1
        if layer % 2 == 0:
            e = layer // 2
            m_c, m_l = even_mixer(h_c, h_l, even_w_in[e], even_w_out[e], ssd_conv_w[e], ssd_conv_b[e],
                                  ssd_a_log[e], ssd_dt_bias[e], ssd_d[e], ssd_norm_w[e],
                                  lb_all[e], hgrn_norm_w[e], need_ctx)
        else:
            o = layer // 2
            m_c, m_l = odd_mixer(h_c, h_l, odd_w_in[o], odd_w_out[o], mlstm_conv_w[o], mlstm_conv_b[o],
                                 mlstm_gate_b[o], mlstm_norm_w[o], need_ctx)
        x = x + g1 * m_l
        x = x + g2 * squared_relu_mlp(rmsnorm(x, norm_w[layer, 1]) * (1 + sc2) + sh2, mlp_w1[layer], mlp_w2[layer])
        if need_ctx:
            xc = xc + cg1 * m_c
            xc = xc + cg2 * squared_relu_mlp(rmsnorm(xc, norm_w[layer, 1]) * (1 + csc2) + csh2, mlp_w1[layer], mlp_w2[layer])
    return rmsnorm(x, final_norm_w)
```

```python
import functools

import numpy as np
import jax
import jax.numpy as jnp
from jax import lax
from jax.experimental import pallas as pl
from jax.experimental.pallas import tpu as pltpu

F32 = jnp.float32
BF16 = jnp.bfloat16

EPS = 1e-6
GRID_W = 64
N_MOD = 6
C_ROWS = 32

V7X_VMEM_BYTES = 64 * 1024 * 1024
VMEM_LIMIT = V7X_VMEM_BYTES - 8 * 1024 * 1024

SSD_CHUNK = 128
SSD_HEAD_DIM = 64
SSD_STATE = 128
SSD_GROUPS = 2
HGRN_CHUNK = 64
HGRN_HEADS = 8
MLSTM_CHUNK = 128
MLSTM_HEADS = 4
CONV_PIECE = 256


def _cparams(*sem):
    return pltpu.CompilerParams(dimension_semantics=sem, vmem_limit_bytes=VMEM_LIMIT)


def _silu(x):
    return x * jax.nn.sigmoid(x)


def _softplus(x):
    return jnp.maximum(x, 0.0) + jnp.log1p(jnp.exp(-jnp.abs(x)))


def _log_sigmoid(x):
    return -_softplus(-x)


def _dot(a, b):
    return jnp.dot(a, b, preferred_element_type=F32)


def _dot_nt(a, b):
    return lax.dot_general(a, b, (((1,), (1,)), ((), ())), preferred_element_type=F32)


def _dot_tn(a, b):
    return lax.dot_general(a, b, (((0,), (0,)), ((), ())), preferred_element_type=F32)


def _dot01(m01, x):
    hi = x.astype(BF16)
    r = x - hi.astype(F32)
    mid = r.astype(BF16)
    lo = (r - mid.astype(F32)).astype(BF16)
    return _dot(m01, hi) + (_dot(m01, mid) + _dot(m01, lo))


def _rms(x):
    return x * lax.rsqrt(jnp.mean(x * x, axis=-1, keepdims=True) + EPS)


def _conv3(u, w, b, period):
    rows = u.shape[0]
    t = lax.broadcasted_iota(jnp.int32, u.shape, 0) & (period - 1)
    left = jnp.where(t == 0, 0.0, pltpu.roll(u, 1, axis=0))
    right = jnp.where(t == period - 1, 0.0, pltpu.roll(u, rows - 1, axis=0))
    return left * w[0:1] + u * w[1:2] + right * w[2:3] + b


def _modulated_norm(x, nw, sh_l, sc_l, sh_c, sc_c, row0, n_lat):
    rows = row0 + lax.broadcasted_iota(jnp.int32, (x.shape[0], 1), 0)
    is_ctx = rows >= n_lat
    sc = jnp.where(is_ctx, sc_c, sc_l)
    sh = jnp.where(is_ctx, sh_c, sh_l)
    return _rms(x) * nw * (1.0 + sc) + sh


def _mod_kernel(c_ref, w_ref, b_ref, o_ref):
    a = _silu(c_ref[...]).astype(BF16)
    o_ref[...] = _dot(a, w_ref[...].astype(BF16)) + b_ref[...]


def _modulation(c_all, mod_w, mod_b):
    depth, d, n = mod_w.shape
    tn = 1024
    return pl.pallas_call(
        _mod_kernel,
        grid=(depth, n // tn),
        in_specs=[
            pl.BlockSpec((C_ROWS, d), lambda l, j: (0, 0)),
            pl.BlockSpec((None, d, tn), lambda l, j: (l, 0, j)),
            pl.BlockSpec((None, 1, tn), lambda l, j: (l, 0, j)),
        ],
        out_specs=pl.BlockSpec((None, C_ROWS, tn), lambda l, j: (l, 0, j)),
        out_shape=jax.ShapeDtypeStruct((depth, C_ROWS, n), F32),
        compiler_params=_cparams("parallel", "parallel"),
        name="modulation",
    )(c_all, mod_w, mod_b.reshape(depth, 1, n))


def _mod_specs(layer, ctx_row, ks, d, nargs):
    specs = []
    for k in ks:
        if nargs == 3:
            specs.append(pl.BlockSpec((None, None, None, 1, d), lambda b, i, j, k=k: (layer, b, k, 0, 0)))
            specs.append(pl.BlockSpec((None, None, None, 1, d), lambda b, i, j, k=k: (layer, ctx_row, k, 0, 0)))
        else:
            specs.append(pl.BlockSpec((None, None, None, 1, d), lambda b, i, k=k: (layer, b, k, 0, 0)))
            specs.append(pl.BlockSpec((None, None, None, 1, d), lambda b, i, k=k: (layer, ctx_row, k, 0, 0)))
    return specs


def _inproj_kernel(x_ref, nw_ref, shl_ref, shc_ref, scl_ref, scc_ref, w_ref, ws_ref, o_ref, os_ref, h_ref, *, n_lat):
    @pl.when(pl.program_id(2) == 0)
    def _():
        tm = x_ref.shape[0]
        h = _modulated_norm(x_ref[...], nw_ref[...], shl_ref[...], scl_ref[...], shc_ref[...], scc_ref[...],
                            pl.program_id(1) * tm, n_lat)
        hb = h.astype(BF16)
        h_ref[...] = hb
        os_ref[...] = _dot(hb, ws_ref[...])

    o_ref[...] = _dot(h_ref[...], w_ref[...])


def _inproj(x, nw, mods, layer, ctx_row, w_main, w_small, n_lat, tm, tn):
    b, lt, d = x.shape
    n = w_main.shape[1]
    ns = w_small.shape[1]
    return pl.pallas_call(
        functools.partial(_inproj_kernel, n_lat=n_lat),
        grid=(b, lt // tm, n // tn),
        in_specs=[
            pl.BlockSpec((None, tm, d), lambda b, i, j: (b, i, 0)),
            pl.BlockSpec((1, d), lambda b, i, j: (0, 0)),
            *_mod_specs(layer, ctx_row, (0, 1), d, 3),
            pl.BlockSpec((d, tn), lambda b, i, j: (0, j)),
            pl.BlockSpec((d, ns), lambda b, i, j: (0, 0)),
        ],
        out_specs=[
            pl.BlockSpec((None, tm, tn), lambda b, i, j: (b, i, j)),
            pl.BlockSpec((None, tm, ns), lambda b, i, j: (b, i, 0)),
        ],
        out_shape=[jax.ShapeDtypeStruct((b, lt, n), F32), jax.ShapeDtypeStruct((b, lt, ns), F32)],
        scratch_shapes=[pltpu.VMEM((tm, d), BF16)],
        compiler_params=_cparams("parallel", "parallel", "arbitrary"),
        name="inproj",
    )(x, nw, mods, mods, mods, mods, w_main, w_small)


def _outproj_kernel(*refs, n_y, n_lat):
    y_refs, w_refs = refs[:n_y], refs[n_y:2 * n_y]
    x_ref, gl_ref, gc_ref, o_ref = refs[2 * n_y:]
    acc = _dot(y_refs[0][...], w_refs[0][...])
    for y_ref, w_ref in zip(y_refs[1:], w_refs[1:]):
        acc = acc + _dot(y_ref[...], w_ref[...])
    tm = x_ref.shape[0]
    rows = pl.program_id(1) * tm + lax.broadcasted_iota(jnp.int32, (tm, 1), 0)
    g = jnp.where(rows >= n_lat, gc_ref[...], gl_ref[...])
    o_ref[...] = x_ref[...] + g * acc


def _outproj(ys, ws, x, mods, layer, ctx_row, n_lat, rows, tm):
    b, lt, d = x.shape
    n_y = len(ys)
    y_specs = [pl.BlockSpec((None, tm, y.shape[2]), lambda b, i: (b, i, 0)) for y in ys]
    w_specs = [pl.BlockSpec(w.shape, lambda b, i: (0, 0)) for w in ws]
    return pl.pallas_call(
        functools.partial(_outproj_kernel, n_y=n_y, n_lat=n_lat),
        grid=(b, rows // tm),
        in_specs=[*y_specs, *w_specs,
                  pl.BlockSpec((None, tm, d), lambda b, i: (b, i, 0)),
                  *_mod_specs(layer, ctx_row, (2,), d, 2)],
        out_specs=pl.BlockSpec((None, tm, d), lambda b, i: (b, i, 0)),
        out_shape=jax.ShapeDtypeStruct((b, rows, d), F32),
        compiler_params=_cparams("parallel", "parallel"),
        name="outproj",
    )(*ys, *ws, x, mods, mods)


def _mlp_kernel(x_ref, nw_ref, shl_ref, shc_ref, scl_ref, scc_ref, gl_ref, gc_ref, w1_ref, w2_ref, fw_ref,
                o_ref, h_ref, acc_ref, *, n_lat, final_norm):
    f = pl.program_id(2)
    tm = x_ref.shape[0]

    @pl.when(f == 0)
    def _():
        h = _modulated_norm(x_ref[...], nw_ref[...], shl_ref[...], scl_ref[...], shc_ref[...], scc_ref[...],
                            pl.program_id(1) * tm, n_lat)
        h_ref[...] = h.astype(BF16)
        acc_ref[...] = jnp.zeros_like(acc_ref)

    a = jnp.maximum(_dot(h_ref[...], w1_ref[...]), 0.0)
    acc_ref[...] += _dot((a * a).astype(BF16), w2_ref[...])

    @pl.when(f == pl.num_programs(2) - 1)
    def _():
        rows = pl.program_id(1) * tm + lax.broadcasted_iota(jnp.int32, (tm, 1), 0)
        g = jnp.where(rows >= n_lat, gc_ref[...], gl_ref[...])
        y = x_ref[...] + g * acc_ref[...]
        if final_norm:
            y = _rms(y) * fw_ref[...]
        o_ref[...] = y


def _mlp(x, nw, mods, layer, ctx_row, w1, w2, fw, n_lat, rows, tm, tf, final_norm):
    b, _, d = x.shape
    dff = w1.shape[1]
    return pl.pallas_call(
        functools.partial(_mlp_kernel, n_lat=n_lat, final_norm=final_norm),
        grid=(b, rows // tm, dff // tf),
        in_specs=[
            pl.BlockSpec((None, tm, d), lambda b, i, j: (b, i, 0)),
            pl.BlockSpec((1, d), lambda b, i, j: (0, 0)),
            *_mod_specs(layer, ctx_row, (3, 4, 5), d, 3),
            pl.BlockSpec((d, tf), lambda b, i, j: (0, j)),
            pl.BlockSpec((tf, d), lambda b, i, j: (j, 0)),
            pl.BlockSpec((1, d), lambda b, i, j: (0, 0)),
        ],
        out_specs=pl.BlockSpec((None, tm, d), lambda b, i, j: (b, i, 0)),
        out_shape=jax.ShapeDtypeStruct((b, rows, d), F32),
        scratch_shapes=[pltpu.VMEM((tm, d), BF16), pltpu.VMEM((tm, d), F32)],
        compiler_params=_cparams("parallel", "parallel", "arbitrary"),
        name="mlp",
    )(x, nw, mods, mods, mods, mods, mods, mods, w1, w2, fw)


def _tri_consts(c):
    lower = np.tril(np.ones((c, c), np.float32))
    return jnp.asarray(np.stack([lower, lower.T]), BF16)


def _chunk_index(j, d, n_lat_chunks, n_chunks):
    if d == 0:
        c = j + n_lat_chunks
        return jnp.where(c >= n_chunks, c - n_chunks, c)
    return n_chunks - 1 - j


def _conv_rows(n_lat, n_ctx, fn):
    def body(p, carry):
        fn(pl.multiple_of(p * CONV_PIECE, CONV_PIECE), CONV_PIECE, GRID_W)
        return carry
    lax.fori_loop(0, n_lat // CONV_PIECE, body, 0)
    fn(n_lat, n_ctx, n_ctx)


def _ssd_kernel(x_ref, b_ref, c_ref, z_ref, dt_ref, cwx_ref, cwb_ref, cwc_ref, cbx_ref, cbb_ref, cbc_ref,
                dtb_ref, alog_ref, dsk_ref, nw_ref, tri_ref, o_ref,
                xs, bs, cs, dts, acs, ys, hst, *, n_lat, n_ctx):
    ch = SSD_CHUNK
    p = SSD_HEAD_DIM
    heads = x_ref.shape[1] // p
    n_chunks = (n_lat + n_ctx) // ch
    n_lat_chunks = n_lat // ch

    def conv_piece(r0, rows, period):
        sl = pl.ds(r0, rows)
        xs[sl, :] = _silu(_conv3(x_ref[sl, :], cwx_ref[...], cbx_ref[...], period))
        bs[sl, :] = _silu(_conv3(b_ref[sl, :], cwb_ref[...], cbb_ref[...], period)).astype(BF16)
        cs[sl, :] = _silu(_conv3(c_ref[sl, :], cwc_ref[...], cbc_ref[...], period)).astype(BF16)
        dt = _softplus(dt_ref[sl, :] + dtb_ref[...])
        dts[sl, :] = dt
        acs[sl, :] = -jnp.exp(alog_ref[...]) * dt

    _conv_rows(n_lat, n_ctx, conv_piece)

    ti = lax.broadcasted_iota(jnp.int32, (ch, ch), 0)
    si = lax.broadcasted_iota(jnp.int32, (ch, ch), 1)

    for d in range(2):
        mask = (si <= ti) if d == 0 else (si >= ti)
        hst[...] = jnp.zeros_like(hst)

        def body(j, carry, d=d, mask=mask):
            r0 = pl.multiple_of(_chunk_index(j, d, n_lat_chunks, n_chunks) * ch, ch)
            sl = pl.ds(r0, ch)
            cum = _dot01(tri_ref[d], acs[sl, :])
            cum_t = cum.T
            tot = cum[ch - 1:ch, :] if d == 0 else cum[0:1, :]
            e_cum = jnp.exp(cum)
            e_end = jnp.exp(tot - cum) * dts[sl, :]
            e_tot = jnp.exp(tot)
            dt_c = dts[sl, :]
            bc = bs[sl, :]
            cc = cs[sl, :]
            cb = _dot_nt(cc, bc)
            xch = xs[sl, :]
            for h in range(heads):
                col = d * heads + h
                hs = slice(h * p, (h + 1) * p)
                diff = cum[:, col:col + 1] - cum_t[col:col + 1, :]
                decay = jnp.exp(jnp.where(mask, diff, -jnp.inf))
                xh = xch[:, hs]
                xdt = (xh * dt_c[:, col:col + 1]).astype(BF16)
                state = hst[h]
                y = _dot((cb * decay).astype(BF16), xdt)
                y = y + _dot(cc, state.astype(BF16)) * e_cum[:, col:col + 1]
                xw = (xh * e_end[:, col:col + 1]).astype(BF16)
                hst[h] = state * e_tot[:, col:col + 1] + _dot_tn(bc, xw)
                if d == 0:
                    ys[sl, hs] = y
                else:
                    ys[sl, hs] = ys[sl, hs] + y
            if d == 1:
                g = (ys[sl, :] + xch * dsk_ref[...]) * _silu(z_ref[sl, :])
                o_ref[sl, :] = (_rms(g) * nw_ref[...]).astype(BF16)
            return carry

        lax.fori_loop(0, n_chunks, body, 0)


def _ssd(u, us, conv_w, conv_b, dtb, alog, dsk, nw, n_lat, n_ctx):
    b, lt, _ = u.shape
    gw = SSD_HEAD_DIM * 8
    ns = SSD_STATE
    x_blk, b_blk, c_blk = 1024 // gw, 2048 // ns, 2304 // ns
    cx_blk, cb_blk, cc_blk = 0, 1024 // ns, 1280 // ns
    seq = lambda w, off: pl.BlockSpec((None, lt, w), lambda b, g, off=off: (b, 0, off + g))
    par = lambda r, w, off: pl.BlockSpec((r, w), lambda b, g, off=off: (0, off + g))
    grp = lambda w: pl.BlockSpec((None, 1, w), lambda b, g: (g, 0, 0))
    return pl.pallas_call(
        functools.partial(_ssd_kernel, n_lat=n_lat, n_ctx=n_ctx),
        grid=(b, SSD_GROUPS),
        in_specs=[
            seq(gw, x_blk), seq(ns, b_blk), seq(ns, c_blk), seq(gw, 0),
            pl.BlockSpec((None, lt, 128), lambda b, g: (b, 0, g)),
            par(3, gw, cx_blk), par(3, ns, cb_blk), par(3, ns, cc_blk),
            par(1, gw, cx_blk), par(1, ns, cb_blk), par(1, ns, cc_blk),
            grp(128), grp(128), grp(gw), grp(gw),
            pl.BlockSpec((2, SSD_CHUNK, SSD_CHUNK), lambda b, g: (0, 0, 0)),
        ],
        out_specs=pl.BlockSpec((None, lt, gw), lambda b, g: (b, 0, g)),
        out_shape=jax.ShapeDtypeStruct((b, lt, SSD_GROUPS * gw), BF16),
        scratch_shapes=[
            pltpu.VMEM((lt, gw), F32), pltpu.VMEM((lt, ns), BF16), pltpu.VMEM((lt, ns), BF16),
            pltpu.VMEM((lt, 128), F32), pltpu.VMEM((lt, 128), F32), pltpu.VMEM((lt, gw), F32),
            pltpu.VMEM((8, ns, SSD_HEAD_DIM), F32),
        ],
        compiler_params=_cparams("parallel", "parallel"),
        name="ssd",
    )(u, u, u, u, us, conv_w, conv_w, conv_w, conv_b, conv_b, conv_b, dtb, alog, dsk, nw, _tri_consts(SSD_CHUNK))


_HGRN_LEVELS = (32, 16, 8, 4, 2, 1)


def _hgrn_consts():
    c = HGRN_CHUNK
    sums = np.zeros((7, c, c), np.float32)
    pairs = np.zeros((7, c, c), np.float32)
    for li, m in enumerate(_HGRN_LEVELS):
        for t in range(c):
            beta = (t // (2 * m)) * 2 * m
            mid = beta + m
            if t >= mid:
                sums[li, t, mid:t + 1] = 1.0
                pairs[li, t, beta:mid] = 1.0
            else:
                sums[li, t, t + 1:mid] = 1.0
    sums[6] = np.tril(np.ones((c, c), np.float32))
    pairs[6] = np.eye(c, dtype=np.float32)
    sums = np.stack([sums, sums[:, ::-1, ::-1]])
    pairs = np.stack([pairs, pairs[:, ::-1, ::-1]])
    return jnp.asarray(sums.reshape(2, 7 * c, c), BF16), jnp.asarray(pairs, F32)


def _hgrn_kernel(q_ref, ff_ref, fb_ref, i_ref, g_ref, lb_ref, nw_ref, sums_ref, pairs_ref, o_ref, ys,
                 *, n_lat, n_ctx):
    ch = HGRN_CHUNK
    n_chunks = (n_lat + n_ctx) // ch
    n_lat_chunks = n_lat // ch
    lb = lb_ref[...]
    log_lb = jnp.log(lb)
    log_1mlb = jnp.log1p(-lb)
    one_m_lb = 1.0 - lb
    dk = q_ref.shape[1]
    dv = i_ref.shape[1]

    for d in range(2):
        f_ref = ff_ref if d == 0 else fb_ref

        def body(j, state_t, d=d, f_ref=f_ref):
            r0 = pl.multiple_of(_chunk_index(j, d, n_lat_chunks, n_chunks) * ch, ch)
            sl = pl.ds(r0, ch)
            zf = f_ref[sl, :]
            gate = log_1mlb + _log_sigmoid(zf)
            logf = jnp.maximum(log_lb, gate) + jnp.log1p(jnp.exp(-jnp.abs(log_lb - gate)))
            kin = one_m_lb * jax.nn.sigmoid(-zf)
            q = _silu(q_ref[sl, :])
            v = i_ref[sl, :].astype(BF16)
            rel = _dot01(sums_ref[d], logf)
            bcum = rel[6 * ch:7 * ch]
            tot = bcum[ch - 1:ch] if d == 0 else bcum[0:1]
            att = pairs_ref[d, 6] * _dot_nt(q.astype(BF16), kin.astype(BF16))
            for li in range(6):
                e = jnp.exp(rel[li * ch:(li + 1) * ch])
                att = att + pairs_ref[d, li] * _dot_nt((q * e).astype(BF16), (kin * e).astype(BF16))
            o = _dot(att.astype(BF16), v)
            o = o + _dot_nt((q * jnp.exp(bcum)).astype(BF16), state_t.astype(BF16))
            kd = (kin * jnp.exp(tot - bcum)).astype(BF16)
            state_t = state_t * jnp.exp(tot) + _dot_tn(v, kd)
            if d == 0:
                ys[sl, :] = o
            else:
                o = ys[sl, :] + o
                o_ref[sl, :] = (_rms(o) * nw_ref[...] * _silu(g_ref[sl, :])).astype(BF16)
            return state_t

        lax.fori_loop(0, n_chunks, body, jnp.zeros((dv, dk), F32))


def _hgrn(u, lb, nw, n_lat, n_ctx):
    b, lt, _ = u.shape
    w = 128
    sums, pairs = _hgrn_consts()
    seq = lambda off: pl.BlockSpec((None, lt, w), lambda b, h, off=off: (b, 0, off + h))
    head = pl.BlockSpec((None, 1, w), lambda b, h: (h, 0, 0))
    return pl.pallas_call(
        functools.partial(_hgrn_kernel, n_lat=n_lat, n_ctx=n_ctx),
        grid=(b, HGRN_HEADS),
        in_specs=[seq(2560 // w), seq(3584 // w), seq(4608 // w), seq(5632 // w), seq(6656 // w), head, head,
                  pl.BlockSpec(sums.shape, lambda b, h: (0, 0, 0)),
                  pl.BlockSpec(pairs.shape, lambda b, h: (0, 0, 0, 0))],
        out_specs=pl.BlockSpec((None, lt, w), lambda b, h: (b, 0, h)),
        out_shape=jax.ShapeDtypeStruct((b, lt, HGRN_HEADS * w), BF16),
        scratch_shapes=[pltpu.VMEM((lt, w), F32)],
        compiler_params=_cparams("parallel", "parallel"),
        name="hgrn2",
    )(u, u, u, u, u, lb, nw, sums, pairs)


def _mlstm_kernel(q_ref, k_ref, v_ref, og_ref, gt_ref, cwq_ref, cwk_ref, cbq_ref, cbk_ref, gb_ref, nw_ref, tri_ref,
                  o_ref, qs, ks, gs, ys, cst, *, n_lat, n_ctx):
    ch = MLSTM_CHUNK
    n_chunks = (n_lat + n_ctx) // ch
    n_lat_chunks = n_lat // ch
    dqk = q_ref.shape[1]
    dv = v_ref.shape[1]
    k_scale = dqk ** -0.5

    def conv_piece(r0, rows, period):
        sl = pl.ds(r0, rows)
        qs[sl, :] = _silu(_conv3(q_ref[sl, :], cwq_ref[...], cbq_ref[...], period)).astype(BF16)
        ks[sl, :] = (_silu(_conv3(k_ref[sl, :], cwk_ref[...], cbk_ref[...], period)) * k_scale).astype(BF16)
        raw = gt_ref[sl, :] + gb_ref[...]
        lane = lax.broadcasted_iota(jnp.int32, raw.shape, 1)
        gs[sl, :] = jnp.where(lane < 2, raw, _log_sigmoid(raw))

    _conv_rows(n_lat, n_ctx, conv_piece)

    ti = lax.broadcasted_iota(jnp.int32, (ch, ch), 0)
    si = lax.broadcasted_iota(jnp.int32, (ch, ch), 1)
    ones_blk = jnp.ones((ch, 128), BF16)

    for d in range(2):
        mask = (si <= ti) if d == 0 else (si >= ti)
        cst[...] = jnp.zeros_like(cst)

        def body(j, m_prev, d=d, mask=mask):
            r0 = pl.multiple_of(_chunk_index(j, d, n_lat_chunks, n_chunks) * ch, ch)
            sl = pl.ds(r0, ch)
            gates = gs[sl, :]
            cum = _dot01(tri_ref[d], gates)
            bcol = cum[:, 2 + d:3 + d]
            icol = gates[:, d:d + 1]
            tot = bcol[ch - 1:ch] if d == 0 else bcol[0:1]
            w_row = jnp.broadcast_to(icol - bcol, (ch, ch)).T
            logd = jnp.where(mask, bcol + w_row, -jnp.inf)
            gstate = bcol + m_prev
            mt = jnp.maximum(jnp.max(logd, axis=-1, keepdims=True), gstate)
            qc = qs[sl, :]
            kc = ks[sl, :]
            w = _dot_nt(qc, kc) * jnp.exp(logd - mt)
            sw = jnp.exp(gstate - mt)
            vc = v_ref[sl, :].astype(BF16)
            vaug = jnp.concatenate([vc, ones_blk], axis=1)
            state = cst[...]
            qstate = _dot(qc, state.astype(BF16))
            num = _dot(w.astype(BF16), vc) + sw * qstate[:, :dv]
            den = jnp.sum(w, axis=-1, keepdims=True) + sw * qstate[:, dv:dv + 1]
            hout = num / jnp.maximum(jnp.abs(den), jnp.exp(-mt))
            logw = tot - bcol + icol
            m_new = jnp.maximum(tot + m_prev, jnp.max(logw, axis=0, keepdims=True))
            ws = jnp.exp(logw - m_new)
            kw = (kc.astype(F32) * ws).astype(BF16)
            cst[...] = jnp.exp(tot + m_prev - m_new) * state + _dot_tn(kw, vaug)
            if d == 0:
                ys[sl, :] = hout
            else:
                hh = ys[sl, :] + hout
                o_ref[sl, :] = (_rms(hh) * nw_ref[...] * jax.nn.sigmoid(og_ref[sl, :])).astype(BF16)
            return m_new

        lax.fori_loop(0, n_chunks, body, jnp.zeros((1, 1), F32))


def _mlstm(u, us, conv_w, conv_b, gate_b, nw, n_lat, n_ctx):
    b, lt, _ = u.shape
    dqk, dv = 256, 512
    nh = MLSTM_HEADS
    seq = lambda w, off: pl.BlockSpec((None, lt, w), lambda b, h, off=off: (b, 0, off + h))
    par = lambda r, off: pl.BlockSpec((r, dqk), lambda b, h, off=off: (0, off + h))
    return pl.pallas_call(
        functools.partial(_mlstm_kernel, n_lat=n_lat, n_ctx=n_ctx),
        grid=(b, nh),
        in_specs=[
            seq(dqk, 0), seq(dqk, nh), seq(dv, 2048 // dv), seq(dv, 4096 // dv), seq(128, 0),
            par(3, 0), par(3, nh), par(1, 0), par(1, nh),
            pl.BlockSpec((None, 1, 128), lambda b, h: (h, 0, 0)),
            pl.BlockSpec((None, 1, dv), lambda b, h: (h, 0, 0)),
            pl.BlockSpec((2, MLSTM_CHUNK, MLSTM_CHUNK), lambda b, h: (0, 0, 0)),
        ],
        out_specs=pl.BlockSpec((None, lt, dv), lambda b, h: (b, 0, h)),
        out_shape=jax.ShapeDtypeStruct((b, lt, nh * dv), BF16),
        scratch_shapes=[
            pltpu.VMEM((lt, dqk), BF16), pltpu.VMEM((lt, dqk), BF16), pltpu.VMEM((lt, 128), F32),
            pltpu.VMEM((lt, dv), F32), pltpu.VMEM((dqk, dv + 128), F32),
        ],
        compiler_params=_cparams("parallel", "parallel"),
        name="mlstm",
    )(u, u, u, u, us, conv_w, conv_w, conv_b, conv_b, gate_b, nw, _tri_consts(MLSTM_CHUNK))


def _pad_lanes(a, width=128):
    return jnp.pad(a, [(0, 0)] * (a.ndim - 1) + [(0, width - a.shape[-1])])


def _even_params(w_in, dt_bias, a_log, d_skip):
    heads = a_log.shape[1]
    hg = heads // SSD_GROUPS
    w_main = jnp.concatenate([w_in[:, :2560], w_in[:, 2560 + 2 * heads:]], axis=1).astype(BF16)
    dt_w = w_in[:, 2560:2560 + 2 * heads]
    per_group = lambda a: [_pad_lanes(jnp.concatenate([a[..., g * hg:(g + 1) * hg], a[..., heads + g * hg:heads + (g + 1) * hg]], axis=-1))
                           for g in range(SSD_GROUPS)]
    w_small = jnp.concatenate(per_group(dt_w), axis=1).astype(BF16)
    flat = lambda a: a.reshape(1, 2 * heads)
    dtb = jnp.stack(per_group(flat(dt_bias)))
    alog = jnp.stack(per_group(flat(a_log)))
    dsk = jnp.repeat(d_skip, SSD_HEAD_DIM).reshape(SSD_GROUPS, 1, hg * SSD_HEAD_DIM)
    return w_main, w_small, dtb, alog, dsk


def _odd_params(w_in, gate_b):
    nh = MLSTM_HEADS
    w_main = w_in[:, :6144].astype(BF16)
    gw = w_in[:, 6144:]
    w_small = jnp.concatenate([_pad_lanes(gw[:, h::nh]) for h in range(nh)], axis=1).astype(BF16)
    gb = jnp.stack([_pad_lanes(gate_b[:, h].reshape(1, 4)) for h in range(nh)])
    return w_main, w_small, gb


def kernel(x, c, ctx, c_ctx, mod_w, mod_b, norm_w, final_norm_w, mlp_w1, mlp_w2, even_w_in, even_w_out, ssd_conv_w, ssd_conv_b, ssd_a_log, ssd_dt_bias, ssd_d, ssd_norm_w, hgrn_lb, hgrn_norm_w, odd_w_in, odd_w_out, mlstm_conv_w, mlstm_conv_b, mlstm_gate_b, mlstm_norm_w):
    bsz, n_lat, d = x.shape
    n_ctx = ctx.shape[1]
    depth = mod_w.shape[0]
    lt = n_lat + n_ctx
    assert bsz < C_ROWS and n_lat % CONV_PIECE == 0 and n_ctx % SSD_CHUNK == 0 and n_ctx & (n_ctx - 1) == 0
    ctx_row = bsz

    c_all = jnp.zeros((C_ROWS, d), F32).at[:bsz].set(c).at[ctx_row].set(c_ctx)
    mods = _modulation(c_all, mod_w, mod_b).reshape(depth, C_ROWS, N_MOD, 1, d)

    lb_all = jnp.cumsum(jax.nn.softmax(hgrn_lb.astype(F32), axis=0), axis=0)
    lb_all = lb_all - lb_all[0]

    tm_in = lt // 2
    tm_full = lt // 4
    tm_lat = min(512, n_lat)

    xx = jnp.concatenate([x, ctx], axis=1)
    for layer in range(depth):
        last = layer == depth - 1
        nw1 = norm_w[layer, 0].reshape(1, d)
        nw2 = norm_w[layer, 1].reshape(1, d)
        if layer % 2 == 0:
            e = layer // 2
            w_main, w_small, dtb, alog, dsk = _even_params(even_w_in[e], ssd_dt_bias[e], ssd_a_log[e], ssd_d[e])
            u, us = _inproj(xx, nw1, mods, layer, ctx_row, w_main, w_small, n_lat, tm_in, 768)
            ya = _ssd(u, us, ssd_conv_w[e], ssd_conv_b[e].reshape(1, -1), dtb, alog, dsk,
                      ssd_norm_w[e].reshape(SSD_GROUPS, 1, -1), n_lat, n_ctx)
            yb = _hgrn(u, lb_all[e].reshape(HGRN_HEADS, 1, -1), hgrn_norm_w[e].reshape(HGRN_HEADS, 1, -1), n_lat, n_ctx)
            w_out = even_w_out[e].astype(BF16)
            ys, ws = [ya, yb], [w_out[:ya.shape[2]], w_out[ya.shape[2]:]]
        else:
            o = layer // 2
            w_main, w_small, gb = _odd_params(odd_w_in[o], mlstm_gate_b[o])
            u, us = _inproj(xx, nw1, mods, layer, ctx_row, w_main, w_small, n_lat, tm_in, 768)
            yc = _mlstm(u, us, mlstm_conv_w[o], mlstm_conv_b[o].reshape(1, -1), gb,
                        mlstm_norm_w[o].reshape(MLSTM_HEADS, 1, -1), n_lat, n_ctx)
            ys, ws = [yc], [odd_w_out[o].astype(BF16)]
        rows, tm = (n_lat, tm_lat) if last else (lt, tm_full)
        xx = _outproj(ys, ws, xx, mods, layer, ctx_row, n_lat, rows, tm // 2 if not last else tm)
        xx = _mlp(xx, nw2, mods, layer, ctx_row, mlp_w1[layer].astype(BF16), mlp_w2[layer].astype(BF16),
                  final_norm_w.reshape(1, d), n_lat, rows, tm, 512, last)
    return xx
```

```python
import functools

import numpy as np
import jax
import jax.numpy as jnp
from jax import lax
from jax.experimental import pallas as pl
from jax.experimental.pallas import tpu as pltpu

F32 = jnp.float32
BF16 = jnp.bfloat16

EPS = 1e-6
GRID_W = 64
N_MOD = 6
LOG2_E = 1.4426950408889634
C_ROWS = 32

V7X_VMEM_BYTES = 64 * 1024 * 1024
VMEM_LIMIT = V7X_VMEM_BYTES - 8 * 1024 * 1024

SSD_CHUNK = 128
SSD_HEAD_DIM = 64
SSD_STATE = 128
SSD_GROUPS = 2
HGRN_CHUNK = 64
HGRN_HEADS = 8
MLSTM_CHUNK = 128
MLSTM_HEADS = 4
CONV_PIECE = 256


def _cparams(*sem):
    return pltpu.CompilerParams(dimension_semantics=sem, vmem_limit_bytes=VMEM_LIMIT)


def _silu(x):
    return x * jax.nn.sigmoid(x)


def _softplus(x):
    return jnp.maximum(x, 0.0) + jnp.log(1.0 + jnp.exp(-jnp.abs(x)))


def _log_sigmoid(x):
    return -_softplus(-x)


def _dot(a, b):
    return jnp.dot(a, b, preferred_element_type=F32)


def _dot_nt(a, b):
    return lax.dot_general(a, b, (((1,), (1,)), ((), ())), preferred_element_type=F32)


def _dot_tn(a, b):
    return lax.dot_general(a, b, (((0,), (0,)), ((), ())), preferred_element_type=F32)


def _dot01(m01, x):
    hi = x.astype(BF16)
    r = x - hi.astype(F32)
    mid = r.astype(BF16)
    lo = (r - mid.astype(F32)).astype(BF16)
    return _dot(m01, hi) + (_dot(m01, mid) + _dot(m01, lo))


def _rms(x):
    return x * lax.rsqrt(jnp.mean(x * x, axis=-1, keepdims=True) + EPS)


def _conv3(u, w, b, period):
    rows = u.shape[0]
    t = lax.broadcasted_iota(jnp.int32, u.shape, 0) & (period - 1)
    left = jnp.where(t == 0, 0.0, pltpu.roll(u, 1, axis=0))
    right = jnp.where(t == period - 1, 0.0, pltpu.roll(u, rows - 1, axis=0))
    return left * w[0:1] + u * w[1:2] + right * w[2:3] + b


def _modulated_norm(x, nw, sh_l, sc_l, sh_c, sc_c, row0, n_lat):
    rows = row0 + lax.broadcasted_iota(jnp.int32, (x.shape[0], 1), 0)
    is_ctx = rows >= n_lat
    sc = jnp.where(is_ctx, sc_c, sc_l)
    sh = jnp.where(is_ctx, sh_c, sh_l)
    return _rms(x) * nw * (1.0 + sc) + sh


def _mod_kernel(c_ref, w_ref, b_ref, o_ref):
    a = _silu(c_ref[...]).astype(BF16)
    o_ref[...] = _dot(a, w_ref[...].astype(BF16)) + b_ref[...]


def _modulation(c_all, mod_w, mod_b):
    depth, d, n = mod_w.shape
    tn = 1024
    return pl.pallas_call(
        _mod_kernel,
        grid=(depth, n // tn),
        in_specs=[
            pl.BlockSpec((C_ROWS, d), lambda l, j: (0, 0)),
            pl.BlockSpec((None, d, tn), lambda l, j: (l, 0, j)),
            pl.BlockSpec((None, 1, tn), lambda l, j: (l, 0, j)),
        ],
        out_specs=pl.BlockSpec((None, C_ROWS, tn), lambda l, j: (l, 0, j)),
        out_shape=jax.ShapeDtypeStruct((depth, C_ROWS, n), F32),
        compiler_params=_cparams("parallel", "parallel"),
        name="modulation",
    )(c_all, mod_w, mod_b.reshape(depth, 1, n))


def _mod_specs(layer, ctx_row, ks, d, nargs):
    specs = []
    for k in ks:
        if nargs == 3:
            specs.append(pl.BlockSpec((None, None, None, 1, d), lambda b, i, j, k=k: (layer, b, k, 0, 0)))
            specs.append(pl.BlockSpec((None, None, None, 1, d), lambda b, i, j, k=k: (layer, ctx_row, k, 0, 0)))
        else:
            specs.append(pl.BlockSpec((None, None, None, 1, d), lambda b, i, k=k: (layer, b, k, 0, 0)))
            specs.append(pl.BlockSpec((None, None, None, 1, d), lambda b, i, k=k: (layer, ctx_row, k, 0, 0)))
    return specs


def _inproj_kernel(x_ref, nw_ref, shl_ref, shc_ref, scl_ref, scc_ref, w_ref, ws_ref, o_ref, os_ref, h_ref, *, n_lat):
    @pl.when(pl.program_id(2) == 0)
    def _():
        tm = x_ref.shape[0]
        h = _modulated_norm(x_ref[...], nw_ref[...], shl_ref[...], scl_ref[...], shc_ref[...], scc_ref[...],
                            pl.program_id(1) * tm, n_lat)
        hb = h.astype(BF16)
        h_ref[...] = hb
        os_ref[...] = _dot(hb, ws_ref[...])

    o_ref[...] = _dot(h_ref[...], w_ref[...])


def _inproj(x, nw, mods, layer, ctx_row, w_main, w_small, n_lat, tm, tn):
    b, lt, d = x.shape
    n = w_main.shape[1]
    ns = w_small.shape[1]
    return pl.pallas_call(
        functools.partial(_inproj_kernel, n_lat=n_lat),
        grid=(b, lt // tm, n // tn),
        in_specs=[
            pl.BlockSpec((None, tm, d), lambda b, i, j: (b, i, 0)),
            pl.BlockSpec((1, d), lambda b, i, j: (0, 0)),
            *_mod_specs(layer, ctx_row, (0, 1), d, 3),
            pl.BlockSpec((d, tn), lambda b, i, j: (0, j)),
            pl.BlockSpec((d, ns), lambda b, i, j: (0, 0)),
        ],
        out_specs=[
            pl.BlockSpec((None, tm, tn), lambda b, i, j: (b, i, j)),
            pl.BlockSpec((None, tm, ns), lambda b, i, j: (b, i, 0)),
        ],
        out_shape=[jax.ShapeDtypeStruct((b, lt, n), F32), jax.ShapeDtypeStruct((b, lt, ns), F32)],
        scratch_shapes=[pltpu.VMEM((tm, d), BF16)],
        compiler_params=_cparams("parallel", "parallel", "arbitrary"),
        name="inproj",
    )(x, nw, mods, mods, mods, mods, w_main, w_small)


def _outproj_kernel(*refs, n_y, n_lat):
    y_refs, w_refs = refs[:n_y], refs[n_y:2 * n_y]
    x_ref, gl_ref, gc_ref, o_ref = refs[2 * n_y:]
    acc = _dot(y_refs[0][...], w_refs[0][...])
    for y_ref, w_ref in zip(y_refs[1:], w_refs[1:]):
        acc = acc + _dot(y_ref[...], w_ref[...])
    tm = x_ref.shape[0]
    rows = pl.program_id(1) * tm + lax.broadcasted_iota(jnp.int32, (tm, 1), 0)
    g = jnp.where(rows >= n_lat, gc_ref[...], gl_ref[...])
    o_ref[...] = x_ref[...] + g * acc


def _outproj(ys, ws, x, mods, layer, ctx_row, n_lat, rows, tm):
    b, lt, d = x.shape
    n_y = len(ys)
    y_specs = [pl.BlockSpec((None, tm, y.shape[2]), lambda b, i: (b, i, 0)) for y in ys]
    w_specs = [pl.BlockSpec(w.shape, lambda b, i: (0, 0)) for w in ws]
    return pl.pallas_call(
        functools.partial(_outproj_kernel, n_y=n_y, n_lat=n_lat),
        grid=(b, rows // tm),
        in_specs=[*y_specs, *w_specs,
                  pl.BlockSpec((None, tm, d), lambda b, i: (b, i, 0)),
                  *_mod_specs(layer, ctx_row, (2,), d, 2)],
        out_specs=pl.BlockSpec((None, tm, d), lambda b, i: (b, i, 0)),
        out_shape=jax.ShapeDtypeStruct((b, rows, d), F32),
        compiler_params=_cparams("parallel", "parallel"),
        name="outproj",
    )(*ys, *ws, x, mods, mods)


def _mlp_kernel(x_ref, nw_ref, shl_ref, shc_ref, scl_ref, scc_ref, gl_ref, gc_ref, w1_ref, w2_ref, fw_ref,
                o_ref, h_ref, acc_ref, *, n_lat, final_norm):
    f = pl.program_id(2)
    tm = x_ref.shape[0]

    @pl.when(f == 0)
    def _():
        h = _modulated_norm(x_ref[...], nw_ref[...], shl_ref[...], scl_ref[...], shc_ref[...], scc_ref[...],
                            pl.program_id(1) * tm, n_lat)
        h_ref[...] = h.astype(BF16)
        acc_ref[...] = jnp.zeros_like(acc_ref)

    a = jnp.maximum(_dot(h_ref[...], w1_ref[...]), 0.0)
    acc_ref[...] += _dot((a * a).astype(BF16), w2_ref[...])

    @pl.when(f == pl.num_programs(2) - 1)
    def _():
        rows = pl.program_id(1) * tm + lax.broadcasted_iota(jnp.int32, (tm, 1), 0)
        g = jnp.where(rows >= n_lat, gc_ref[...], gl_ref[...])
        y = x_ref[...] + g * acc_ref[...]
        if final_norm:
            y = _rms(y) * fw_ref[...]
        o_ref[...] = y


def _mlp(x, nw, mods, layer, ctx_row, w1, w2, fw, n_lat, rows, tm, tf, final_norm):
    b, _, d = x.shape
    dff = w1.shape[1]
    return pl.pallas_call(
        functools.partial(_mlp_kernel, n_lat=n_lat, final_norm=final_norm),
        grid=(b, rows // tm, dff // tf),
        in_specs=[
            pl.BlockSpec((None, tm, d), lambda b, i, j: (b, i, 0)),
            pl.BlockSpec((1, d), lambda b, i, j: (0, 0)),
            *_mod_specs(layer, ctx_row, (3, 4, 5), d, 3),
            pl.BlockSpec((d, tf), lambda b, i, j: (0, j)),
            pl.BlockSpec((tf, d), lambda b, i, j: (j, 0)),
            pl.BlockSpec((1, d), lambda b, i, j: (0, 0)),
        ],
        out_specs=pl.BlockSpec((None, tm, d), lambda b, i, j: (b, i, 0)),
        out_shape=jax.ShapeDtypeStruct((b, rows, d), F32),
        scratch_shapes=[pltpu.VMEM((tm, d), BF16), pltpu.VMEM((tm, d), F32)],
        compiler_params=_cparams("parallel", "parallel", "arbitrary"),
        name="mlp",
    )(x, nw, mods, mods, mods, mods, mods, mods, w1, w2, fw)


def _tri_consts(c):
    lower = np.tril(np.ones((c, c), np.float32))
    return jnp.asarray(np.stack([lower, lower.T]), BF16)


def _chunk_index(j, d, n_lat_chunks, n_chunks):
    if d == 0:
        c = j + n_lat_chunks
        return jnp.where(c >= n_chunks, c - n_chunks, c)
    return n_chunks - 1 - j


def _conv_rows(n_lat, n_ctx, fn):
    def body(p, carry):
        fn(pl.multiple_of(p * CONV_PIECE, CONV_PIECE), CONV_PIECE, GRID_W)
        return carry
    lax.fori_loop(0, n_lat // CONV_PIECE, body, 0)
    fn(n_lat, n_ctx, n_ctx)


def _ssd_kernel(x_ref, b_ref, c_ref, z_ref, dt_ref, cwx_ref, cwb_ref, cwc_ref, cbx_ref, cbb_ref, cbc_ref,
                dtb_ref, alog_ref, dsk_ref, nw_ref, tri_ref, o_ref,
                xs, bs, cs, dts, acs, ys, hst, *, n_lat, n_ctx):
    ch = SSD_CHUNK
    p = SSD_HEAD_DIM
    heads = x_ref.shape[1] // p
    n_chunks = (n_lat + n_ctx) // ch
    n_lat_chunks = n_lat // ch

    def conv_piece(r0, rows, period):
        sl = pl.ds(r0, rows)
        xs[sl, :] = _silu(_conv3(x_ref[sl, :], cwx_ref[...], cbx_ref[...], period))
        bs[sl, :] = _silu(_conv3(b_ref[sl, :], cwb_ref[...], cbb_ref[...], period)).astype(BF16)
        cs[sl, :] = _silu(_conv3(c_ref[sl, :], cwc_ref[...], cbc_ref[...], period)).astype(BF16)
        dt = _softplus(dt_ref[sl, :] + dtb_ref[...])
        dts[sl, :] = dt
        acs[sl, :] = -jnp.exp(alog_ref[...]) * dt

    _conv_rows(n_lat, n_ctx, conv_piece)

    ti = lax.broadcasted_iota(jnp.int32, (ch, ch), 0)
    si = lax.broadcasted_iota(jnp.int32, (ch, ch), 1)

    for d in range(2):
        mask = (si <= ti) if d == 0 else (si >= ti)
        hst[...] = jnp.zeros_like(hst)

        def body(j, carry, d=d, mask=mask):
            r0 = pl.multiple_of(_chunk_index(j, d, n_lat_chunks, n_chunks) * ch, ch)
            sl = pl.ds(r0, ch)
            cum = _dot01(tri_ref[d], acs[sl, :])
            cum_t = cum.T
            tot = cum[ch - 1:ch, :] if d == 0 else cum[0:1, :]
            e_cum = jnp.exp(cum)
            e_end = jnp.exp(tot - cum) * dts[sl, :]
            e_tot = jnp.exp(tot)
            dt_c = dts[sl, :]
            bc = bs[sl, :]
            cc = cs[sl, :]
            cb = _dot_nt(cc, bc)
            xch = xs[sl, :]
            for h in range(heads):
                col = d * heads + h
                hs = slice(h * p, (h + 1) * p)
                diff = cum[:, col:col + 1] - cum_t[col:col + 1, :]
                decay = jnp.exp(jnp.where(mask, diff, -jnp.inf))
                xh = xch[:, hs]
                xdt = (xh * dt_c[:, col:col + 1]).astype(BF16)
                state = hst[h]
                y = _dot((cb * decay).astype(BF16), xdt)
                y = y + _dot(cc, state.astype(BF16)) * e_cum[:, col:col + 1]
                xw = (xh * e_end[:, col:col + 1]).astype(BF16)
                hst[h] = state * e_tot[:, col:col + 1] + _dot_tn(bc, xw)
                if d == 0:
                    ys[sl, hs] = y
                else:
                    ys[sl, hs] = ys[sl, hs] + y
            if d == 1:
                g = (ys[sl, :] + xch * dsk_ref[...]) * _silu(z_ref[sl, :])
                o_ref[sl, :] = (_rms(g) * nw_ref[...]).astype(BF16)
            return carry

        lax.fori_loop(0, n_chunks, body, 0)


def _ssd(u, us, conv_w, conv_b, dtb, alog, dsk, nw, n_lat, n_ctx):
    b, lt, _ = u.shape
    gw = SSD_HEAD_DIM * 8
    ns = SSD_STATE
    x_blk, b_blk, c_blk = 1024 // gw, 2048 // ns, 2304 // ns
    cx_blk, cb_blk, cc_blk = 0, 1024 // ns, 1280 // ns
    seq = lambda w, off: pl.BlockSpec((None, lt, w), lambda b, g, off=off: (b, 0, off + g))
    par = lambda r, w, off: pl.BlockSpec((r, w), lambda b, g, off=off: (0, off + g))
    grp = lambda w: pl.BlockSpec((None, 1, w), lambda b, g: (g, 0, 0))
    return pl.pallas_call(
        functools.partial(_ssd_kernel, n_lat=n_lat, n_ctx=n_ctx),
        grid=(b, SSD_GROUPS),
        in_specs=[
            seq(gw, x_blk), seq(ns, b_blk), seq(ns, c_blk), seq(gw, 0),
            pl.BlockSpec((None, lt, 128), lambda b, g: (b, 0, g)),
            par(3, gw, cx_blk), par(3, ns, cb_blk), par(3, ns, cc_blk),
            par(1, gw, cx_blk), par(1, ns, cb_blk), par(1, ns, cc_blk),
            grp(128), grp(128), grp(gw), grp(gw),
            pl.BlockSpec((2, SSD_CHUNK, SSD_CHUNK), lambda b, g: (0, 0, 0)),
        ],
        out_specs=pl.BlockSpec((None, lt, gw), lambda b, g: (b, 0, g)),
        out_shape=jax.ShapeDtypeStruct((b, lt, SSD_GROUPS * gw), BF16),
        scratch_shapes=[
            pltpu.VMEM((lt, gw), F32), pltpu.VMEM((lt, ns), BF16), pltpu.VMEM((lt, ns), BF16),
            pltpu.VMEM((lt, 128), F32), pltpu.VMEM((lt, 128), F32), pltpu.VMEM((lt, gw), F32),
            pltpu.VMEM((8, ns, SSD_HEAD_DIM), F32),
        ],
        compiler_params=_cparams("parallel", "parallel"),
        name="ssd",
    )(u, u, u, u, us, conv_w, conv_w, conv_w, conv_b, conv_b, conv_b, dtb, alog, dsk, nw, _tri_consts(SSD_CHUNK))


_HGRN_LEVELS = (32, 16, 8, 4, 2, 1)


def _hgrn_consts():
    c = HGRN_CHUNK
    sums = np.zeros((7, c, c), np.float32)
    pairs = np.zeros((7, c, c), np.float32)
    for li, m in enumerate(_HGRN_LEVELS):
        for t in range(c):
            beta = (t // (2 * m)) * 2 * m
            mid = beta + m
            if t >= mid:
                sums[li, t, mid:t + 1] = 1.0
                pairs[li, t, beta:mid] = 1.0
            else:
                sums[li, t, t + 1:mid] = 1.0
    sums[6] = np.tril(np.ones((c, c), np.float32))
    pairs[6] = np.eye(c, dtype=np.float32)
    sums = np.stack([sums, sums[:, ::-1, ::-1]]).reshape(2, 7 * c, c)
    sums = np.concatenate([sums, sums, sums, np.zeros_like(sums)], axis=2)
    pairs = np.stack([pairs, pairs[:, ::-1, ::-1]])
    return jnp.asarray(sums, BF16), jnp.asarray(pairs, F32)


def _hgrn_kernel(q_ref, ff_ref, fb_ref, i_ref, g_ref, lb_ref, nw_ref, sums_ref, pairs_ref, o_ref,
                 ys, w_s, qb_s, kb_s, qd_s, kd_s, et_s, att_s, p_s, *, n_lat, n_ctx):
    ch = HGRN_CHUNK
    n_chunks = (n_lat + n_ctx) // ch
    n_lat_chunks = n_lat // ch
    n_lv = len(_HGRN_LEVELS)
    lb = lb_ref[...]
    log_lb = jnp.log(lb)
    log_1mlb = jnp.log1p(-lb)
    one_m_lb = 1.0 - lb
    dk = q_ref.shape[1]
    dv = i_ref.shape[1]
    t_idx = lax.broadcasted_iota(jnp.int32, (ch, dk), 0)

    for d in range(2):
        f_ref = ff_ref if d == 0 else fb_ref
        later = [((t_idx & m) != 0) == (d == 0) for m in _HGRN_LEVELS]

        def operands(pi, carry, d=d, f_ref=f_ref, later=later):
            r0 = pl.multiple_of(pi * 2 * ch, 2 * ch)
            sl = pl.ds(r0, 2 * ch)
            zf = f_ref[sl, :]
            ls = _log_sigmoid(zf)
            gate = log_1mlb + ls
            logf = jnp.maximum(log_lb, gate) + jnp.log(1.0 + jnp.exp(-jnp.abs(log_lb - gate)))
            kin = one_m_lb * jnp.exp(ls - zf)
            q = _silu(q_ref[sl, :])
            qb_s[sl, :] = q.astype(BF16)
            kb_s[sl, :] = kin.astype(BF16)
            logf2 = logf * LOG2_E
            lf = jnp.concatenate([logf2[:ch], logf2[ch:]], axis=1)
            hi = lf.astype(BF16)
            r1 = lf - hi.astype(F32)
            mid = r1.astype(BF16)
            lo = (r1 - mid.astype(F32)).astype(BF16)
            rel2 = _dot(sums_ref[d], jnp.concatenate([hi, mid, lo, jnp.zeros_like(hi)], axis=0))
            for half in range(2):
                rows = slice(half * ch, (half + 1) * ch)
                rel = rel2[:, half * dk:(half + 1) * dk]
                qh, kh = q[rows], kin[rows]
                c = pi * 2 + half
                for li in range(n_lv):
                    e = jnp.exp2(rel[li * ch:(li + 1) * ch])
                    w_s[c, li * ch:(li + 1) * ch, :] = (jnp.where(later[li], qh, kh) * e).astype(BF16)
                bcum = rel[n_lv * ch:(n_lv + 1) * ch]
                tot = bcum[ch - 1:ch] if d == 0 else bcum[0:1]
                hs = pl.ds(r0 + half * ch, ch)
                qd_s[hs, :] = (qh * jnp.exp2(bcum)).astype(BF16)
                kd_s[hs, :] = (kh * jnp.exp2(tot - bcum)).astype(BF16)
                et_s[c] = jnp.broadcast_to(jnp.exp2(tot), (8, dk))
            return carry

        lax.fori_loop(0, n_chunks // 2, operands, 0, unroll=2)

        def intra(c, carry, d=d):
            sl = pl.ds(pl.multiple_of(c * ch, ch), ch)
            att = pairs_ref[d, n_lv] * _dot_nt(qb_s[sl, :], kb_s[sl, :])
            for li in range(n_lv):
                w = w_s[c, li * ch:(li + 1) * ch, :]
                att = att + pairs_ref[d, li] * _dot_nt(w, w)
            att_s[sl, :] = att.astype(BF16)
            p_s[c] = _dot_tn(i_ref[sl, :].astype(BF16), kd_s[sl, :])
            return carry

        lax.fori_loop(0, n_chunks, intra, 0, unroll=6)

        def scan(j, state_t, d=d):
            c = _chunk_index(j, d, n_lat_chunks, n_chunks)
            sl = pl.ds(pl.multiple_of(c * ch, ch), ch)
            o = _dot(att_s[sl, :], i_ref[sl, :].astype(BF16))
            o = o + _dot_nt(qd_s[sl, :], state_t.astype(BF16))
            if d == 0:
                ys[sl, :] = o
            else:
                o = ys[sl, :] + o
                o_ref[sl, :] = (_rms(o) * nw_ref[...] * _silu(g_ref[sl, :])).astype(BF16)
            return state_t * et_s[c][0:1] + p_s[c]

        lax.fori_loop(0, n_chunks, scan, jnp.zeros((dv, dk), F32), unroll=6)


def _hgrn(u, lb, nw, n_lat, n_ctx):
    b, lt, _ = u.shape
    w = 128
    ch = HGRN_CHUNK
    n_chunks = lt // ch
    sums, pairs = _hgrn_consts()
    seq = lambda off: pl.BlockSpec((None, lt, w), lambda b, h, off=off: (b, 0, off + h))
    head = pl.BlockSpec((None, 1, w), lambda b, h: (h, 0, 0))
    return pl.pallas_call(
        functools.partial(_hgrn_kernel, n_lat=n_lat, n_ctx=n_ctx),
        grid=(b, HGRN_HEADS),
        in_specs=[seq(2560 // w), seq(3584 // w), seq(4608 // w), seq(5632 // w), seq(6656 // w), head, head,
                  pl.BlockSpec(sums.shape, lambda b, h: (0, 0, 0)),
                  pl.BlockSpec(pairs.shape, lambda b, h: (0, 0, 0, 0))],
        out_specs=pl.BlockSpec((None, lt, w), lambda b, h: (b, 0, h)),
        out_shape=jax.ShapeDtypeStruct((b, lt, HGRN_HEADS * w), BF16),
        scratch_shapes=[
            pltpu.VMEM((lt, w), F32),
            pltpu.VMEM((n_chunks, len(_HGRN_LEVELS) * ch, w), BF16),
            pltpu.VMEM((lt, w), BF16), pltpu.VMEM((lt, w), BF16), pltpu.VMEM((lt, w), BF16), pltpu.VMEM((lt, w), BF16),
            pltpu.VMEM((n_chunks, 8, w), F32),
            pltpu.VMEM((lt, ch), BF16),
            pltpu.VMEM((n_chunks, w, w), F32),
        ],
        compiler_params=_cparams("parallel", "parallel"),
        name="hgrn2",
    )(u, u, u, u, u, lb, nw, sums, pairs)


def _mlstm_kernel(q_ref, k_ref, v_ref, og_ref, gt_ref, cwq_ref, cwk_ref, cbq_ref, cbk_ref, gb_ref, nw_ref, tri_ref,
                  o_ref, qs, ks, gs, ys, cst, *, n_lat, n_ctx):
    ch = MLSTM_CHUNK
    n_chunks = (n_lat + n_ctx) // ch
    n_lat_chunks = n_lat // ch
    dqk = q_ref.shape[1]
    dv = v_ref.shape[1]
    k_scale = dqk ** -0.5

    def conv_piece(r0, rows, period):
        sl = pl.ds(r0, rows)
        qs[sl, :] = _silu(_conv3(q_ref[sl, :], cwq_ref[...], cbq_ref[...], period)).astype(BF16)
        ks[sl, :] = (_silu(_conv3(k_ref[sl, :], cwk_ref[...], cbk_ref[...], period)) * k_scale).astype(BF16)
        raw = gt_ref[sl, :] + gb_ref[...]
        lane = lax.broadcasted_iota(jnp.int32, raw.shape, 1)
        gs[sl, :] = jnp.where(lane < 2, raw, _log_sigmoid(raw))

    _conv_rows(n_lat, n_ctx, conv_piece)

    ti = lax.broadcasted_iota(jnp.int32, (ch, ch), 0)
    si = lax.broadcasted_iota(jnp.int32, (ch, ch), 1)
    ones_blk = jnp.ones((ch, 128), BF16)

    for d in range(2):
        mask = (si <= ti) if d == 0 else (si >= ti)
        cst[...] = jnp.zeros_like(cst)

        def body(j, m_prev, d=d, mask=mask):
            r0 = pl.multiple_of(_chunk_index(j, d, n_lat_chunks, n_chunks) * ch, ch)
            sl = pl.ds(r0, ch)
            gates = gs[sl, :]
            cum = _dot01(tri_ref[d], gates)
            bcol = cum[:, 2 + d:3 + d]
            icol = gates[:, d:d + 1]
            tot = bcol[ch - 1:ch] if d == 0 else bcol[0:1]
            w_row = jnp.broadcast_to(icol - bcol, (ch, ch)).T
            logd = jnp.where(mask, bcol + w_row, -jnp.inf)
            gstate = bcol + m_prev
            mt = jnp.maximum(jnp.max(logd, axis=-1, keepdims=True), gstate)
            qc = qs[sl, :]
            kc = ks[sl, :]
            w = _dot_nt(qc, kc) * jnp.exp(logd - mt)
            sw = jnp.exp(gstate - mt)
            vc = v_ref[sl, :].astype(BF16)
            vaug = jnp.concatenate([vc, ones_blk], axis=1)
            state = cst[...]
            qstate = _dot(qc, state.astype(BF16))
            num = _dot(w.astype(BF16), vc) + sw * qstate[:, :dv]
            den = jnp.sum(w, axis=-1, keepdims=True) + sw * qstate[:, dv:dv + 1]
            hout = num / jnp.maximum(jnp.abs(den), jnp.exp(-mt))
            logw = tot - bcol + icol
            m_new = jnp.maximum(tot + m_prev, jnp.max(logw, axis=0, keepdims=True))
            ws = jnp.exp(logw - m_new)
            kw = (kc.astype(F32) * ws).astype(BF16)
            cst[...] = jnp.exp(tot + m_prev - m_new) * state + _dot_tn(kw, vaug)
            if d == 0:
                ys[sl, :] = hout
            else:
                hh = ys[sl, :] + hout
                o_ref[sl, :] = (_rms(hh) * nw_ref[...] * jax.nn.sigmoid(og_ref[sl, :])).astype(BF16)
            return m_new

        lax.fori_loop(0, n_chunks, body, jnp.zeros((1, 1), F32), unroll=2)


def _mlstm(u, us, conv_w, conv_b, gate_b, nw, n_lat, n_ctx):
    b, lt, _ = u.shape
    dqk, dv = 256, 512
    nh = MLSTM_HEADS
    seq = lambda w, off: pl.BlockSpec((None, lt, w), lambda b, h, off=off: (b, 0, off + h))
    par = lambda r, off: pl.BlockSpec((r, dqk), lambda b, h, off=off: (0, off + h))
    return pl.pallas_call(
        functools.partial(_mlstm_kernel, n_lat=n_lat, n_ctx=n_ctx),
        grid=(b, nh),
        in_specs=[
            seq(dqk, 0), seq(dqk, nh), seq(dv, 2048 // dv), seq(dv, 4096 // dv), seq(128, 0),
            par(3, 0), par(3, nh), par(1, 0), par(1, nh),
            pl.BlockSpec((None, 1, 128), lambda b, h: (h, 0, 0)),
            pl.BlockSpec((None, 1, dv), lambda b, h: (h, 0, 0)),
            pl.BlockSpec((2, MLSTM_CHUNK, MLSTM_CHUNK), lambda b, h: (0, 0, 0)),
        ],
        out_specs=pl.BlockSpec((None, lt, dv), lambda b, h: (b, 0, h)),
        out_shape=jax.ShapeDtypeStruct((b, lt, nh * dv), BF16),
        scratch_shapes=[
            pltpu.VMEM((lt, dqk), BF16), pltpu.VMEM((lt, dqk), BF16), pltpu.VMEM((lt, 128), F32),
            pltpu.VMEM((lt, dv), F32), pltpu.VMEM((dqk, dv + 128), F32),
        ],
        compiler_params=_cparams("parallel", "parallel"),
        name="mlstm",
    )(u, u, u, u, us, conv_w, conv_w, conv_b, conv_b, gate_b, nw, _tri_consts(MLSTM_CHUNK))


def _pad_lanes(a, width=128):
    return jnp.pad(a, [(0, 0)] * (a.ndim - 1) + [(0, width - a.shape[-1])])


def _even_params(w_in, dt_bias, a_log, d_skip):
    heads = a_log.shape[1]
    hg = heads // SSD_GROUPS
    w_main = jnp.concatenate([w_in[:, :2560], w_in[:, 2560 + 2 * heads:]], axis=1).astype(BF16)
    dt_w = w_in[:, 2560:2560 + 2 * heads]
    per_group = lambda a: [_pad_lanes(jnp.concatenate([a[..., g * hg:(g + 1) * hg], a[..., heads + g * hg:heads + (g + 1) * hg]], axis=-1))
                           for g in range(SSD_GROUPS)]
    w_small = jnp.concatenate(per_group(dt_w), axis=1).astype(BF16)
    flat = lambda a: a.reshape(1, 2 * heads)
    dtb = jnp.stack(per_group(flat(dt_bias)))
    alog = jnp.stack(per_group(flat(a_log)))
    dsk = jnp.repeat(d_skip, SSD_HEAD_DIM).reshape(SSD_GROUPS, 1, hg * SSD_HEAD_DIM)
    return w_main, w_small, dtb, alog, dsk


def _odd_params(w_in, gate_b):
    nh = MLSTM_HEADS
    w_main = w_in[:, :6144].astype(BF16)
    gw = w_in[:, 6144:]
    w_small = jnp.concatenate([_pad_lanes(gw[:, h::nh]) for h in range(nh)], axis=1).astype(BF16)
    gb = jnp.stack([_pad_lanes(gate_b[:, h].reshape(1, 4)) for h in range(nh)])
    return w_main, w_small, gb


def kernel(x, c, ctx, c_ctx, mod_w, mod_b, norm_w, final_norm_w, mlp_w1, mlp_w2, even_w_in, even_w_out, ssd_conv_w, ssd_conv_b, ssd_a_log, ssd_dt_bias, ssd_d, ssd_norm_w, hgrn_lb, hgrn_norm_w, odd_w_in, odd_w_out, mlstm_conv_w, mlstm_conv_b, mlstm_gate_b, mlstm_norm_w):
    bsz, n_lat, d = x.shape
    n_ctx = ctx.shape[1]
    depth = mod_w.shape[0]
    lt = n_lat + n_ctx
    assert bsz < C_ROWS and n_lat % CONV_PIECE == 0 and n_ctx % SSD_CHUNK == 0 and n_ctx & (n_ctx - 1) == 0
    ctx_row = bsz

    c_all = jnp.zeros((C_ROWS, d), F32).at[:bsz].set(c).at[ctx_row].set(c_ctx)
    mods = _modulation(c_all, mod_w, mod_b).reshape(depth, C_ROWS, N_MOD, 1, d)

    lb_all = jnp.cumsum(jax.nn.softmax(hgrn_lb.astype(F32), axis=0), axis=0)
    lb_all = lb_all - lb_all[0]

    tm_in = lt // 2
    tm_full = lt // 4
    tm_lat = min(512, n_lat)

    xx = jnp.concatenate([x, ctx], axis=1)
    for layer in range(depth):
        last = layer == depth - 1
        nw1 = norm_w[layer, 0].reshape(1, d)
        nw2 = norm_w[layer, 1].reshape(1, d)
        if layer % 2 == 0:
            e = layer // 2
            w_main, w_small, dtb, alog, dsk = _even_params(even_w_in[e], ssd_dt_bias[e], ssd_a_log[e], ssd_d[e])
            u, us = _inproj(xx, nw1, mods, layer, ctx_row, w_main, w_small, n_lat, tm_in, 768)
            ya = _ssd(u, us, ssd_conv_w[e], ssd_conv_b[e].reshape(1, -1), dtb, alog, dsk,
                      ssd_norm_w[e].reshape(SSD_GROUPS, 1, -1), n_lat, n_ctx)
            yb = _hgrn(u, lb_all[e].reshape(HGRN_HEADS, 1, -1), hgrn_norm_w[e].reshape(HGRN_HEADS, 1, -1), n_lat, n_ctx)
            w_out = even_w_out[e].astype(BF16)
            ys, ws = [ya, yb], [w_out[:ya.shape[2]], w_out[ya.shape[2]:]]
        else:
            o = layer // 2
            w_main, w_small, gb = _odd_params(odd_w_in[o], mlstm_gate_b[o])
            u, us = _inproj(xx, nw1, mods, layer, ctx_row, w_main, w_small, n_lat, tm_in, 768)
            yc = _mlstm(u, us, mlstm_conv_w[o], mlstm_conv_b[o].reshape(1, -1), gb,
                        mlstm_norm_w[o].reshape(MLSTM_HEADS, 1, -1), n_lat, n_ctx)
            ys, ws = [yc], [odd_w_out[o].astype(BF16)]
        rows, tm = (n_lat, tm_lat) if last else (lt, tm_full)
        xx = _outproj(ys, ws, xx, mods, layer, ctx_row, n_lat, rows, tm // 2 if not last else tm)
        xx = _mlp(xx, nw2, mods, layer, ctx_row, mlp_w1[layer].astype(BF16), mlp_w2[layer].astype(BF16),
                  final_norm_w.reshape(1, d), n_lat, rows, tm, 512, last)
    return xx
```

```python
import functools

import numpy as np
import jax
import jax.numpy as jnp
from jax import lax
from jax.experimental import pallas as pl
from jax.experimental.pallas import tpu as pltpu

F32 = jnp.float32
BF16 = jnp.bfloat16

EPS = 1e-6
GRID_W = 64
N_MOD = 6
LOG2_E = 1.4426950408889634
C_ROWS = 32

V7X_VMEM_BYTES = 64 * 1024 * 1024
VMEM_LIMIT = V7X_VMEM_BYTES - 8 * 1024 * 1024

SSD_CHUNK = 128
SSD_HEAD_DIM = 64
SSD_STATE = 128
SSD_GROUPS = 2
HGRN_CHUNK = 64
HGRN_HEADS = 8
MLSTM_CHUNK = 128
MLSTM_HEADS = 4
MLSTM_INTERLEAVE = 2
CONV_PIECE = 256
NORM_ROWS = 16


def _cparams(*sem):
    return pltpu.CompilerParams(dimension_semantics=sem, vmem_limit_bytes=VMEM_LIMIT)


def _silu(x):
    return x * jax.nn.sigmoid(x)


def _softplus(x):
    return jnp.maximum(x, 0.0) + jnp.log(1.0 + jnp.exp(-jnp.abs(x)))


def _log_sigmoid(x):
    return -_softplus(-x)


def _dot(a, b):
    return jnp.dot(a, b, preferred_element_type=F32)


def _dot_nt(a, b):
    return lax.dot_general(a, b, (((1,), (1,)), ((), ())), preferred_element_type=F32)


def _dot_tn(a, b):
    return lax.dot_general(a, b, (((0,), (0,)), ((), ())), preferred_element_type=F32)


def _dot01(m01x3, x):
    hi = x.astype(BF16)
    r = x - hi.astype(F32)
    mid = r.astype(BF16)
    lo = (r - mid.astype(F32)).astype(BF16)
    return _dot(m01x3, jnp.concatenate([hi, mid, lo], axis=0))


def _rms(x):
    return x * lax.rsqrt(jnp.mean(x * x, axis=-1, keepdims=True) + EPS)


def _conv3(u, w, b, period):
    rows = u.shape[0]
    t = lax.broadcasted_iota(jnp.int32, u.shape, 0) & (period - 1)
    left = jnp.where(t == 0, 0.0, pltpu.roll(u, 1, axis=0))
    right = jnp.where(t == period - 1, 0.0, pltpu.roll(u, rows - 1, axis=0))
    return left * w[0:1] + u * w[1:2] + right * w[2:3] + b


def _modulated_norm(x_ref, h_ref, nw, sh_l, sc_l, sh_c, sc_c, row0, n_lat):
    a_l, a_c = nw * (1.0 + sc_l), nw * (1.0 + sc_c)

    def block(r, carry):
        start = pl.multiple_of(r * NORM_ROWS, NORM_ROWS)
        rs = pl.ds(start, NORM_ROWS)
        is_ctx = row0 + start >= n_lat
        h_ref[rs, :] = (_rms(x_ref[rs, :]) * jnp.where(is_ctx, a_c, a_l) + jnp.where(is_ctx, sh_c, sh_l)).astype(BF16)
        return carry

    lax.fori_loop(0, x_ref.shape[0] // NORM_ROWS, block, 0, unroll=4)


def _mod_kernel(c_ref, w_ref, b_ref, o_ref):
    a = _silu(c_ref[...]).astype(BF16)
    o_ref[...] = _dot(a, w_ref[...].astype(BF16)) + b_ref[...]


def _modulation(c_all, mod_w, mod_b):
    depth, d, n = mod_w.shape
    tn = 1024
    return pl.pallas_call(
        _mod_kernel,
        grid=(depth, n // tn),
        in_specs=[
            pl.BlockSpec((C_ROWS, d), lambda l, j: (0, 0)),
            pl.BlockSpec((None, d, tn), lambda l, j: (l, 0, j)),
            pl.BlockSpec((None, 1, tn), lambda l, j: (l, 0, j)),
        ],
        out_specs=pl.BlockSpec((None, C_ROWS, tn), lambda l, j: (l, 0, j)),
        out_shape=jax.ShapeDtypeStruct((depth, C_ROWS, n), F32),
        compiler_params=_cparams("parallel", "parallel"),
        name="modulation",
    )(c_all, mod_w, mod_b.reshape(depth, 1, n))


def _mod_specs(layer, ctx_row, ks, d, nargs):
    specs = []
    for k in ks:
        if nargs == 3:
            specs.append(pl.BlockSpec((None, None, None, 1, d), lambda b, i, j, k=k: (layer, b, k, 0, 0)))
            specs.append(pl.BlockSpec((None, None, None, 1, d), lambda b, i, j, k=k: (layer, ctx_row, k, 0, 0)))
        else:
            specs.append(pl.BlockSpec((None, None, None, 1, d), lambda b, i, k=k: (layer, b, k, 0, 0)))
            specs.append(pl.BlockSpec((None, None, None, 1, d), lambda b, i, k=k: (layer, ctx_row, k, 0, 0)))
    return specs


def _inproj_kernel(x_ref, nw_ref, shl_ref, shc_ref, scl_ref, scc_ref, w_ref, ws_ref, o_ref, os_ref, h_ref, *, n_lat):
    @pl.when(pl.program_id(2) == 0)
    def _():
        _modulated_norm(x_ref, h_ref, nw_ref[...], shl_ref[...], scl_ref[...], shc_ref[...], scc_ref[...],
                        pl.program_id(1) * x_ref.shape[0], n_lat)
        os_ref[...] = _dot(h_ref[...], ws_ref[...])

    o_ref[...] = _dot(h_ref[...], w_ref[...])


def _inproj(x, nw, mods, layer, ctx_row, w_main, w_small, n_lat, tm, tn):
    b, lt, d = x.shape
    n = w_main.shape[1]
    ns = w_small.shape[1]
    return pl.pallas_call(
        functools.partial(_inproj_kernel, n_lat=n_lat),
        grid=(b, lt // tm, n // tn),
        in_specs=[
            pl.BlockSpec((None, tm, d), lambda b, i, j: (b, i, 0)),
            pl.BlockSpec((1, d), lambda b, i, j: (0, 0)),
            *_mod_specs(layer, ctx_row, (0, 1), d, 3),
            pl.BlockSpec((d, tn), lambda b, i, j: (0, j)),
            pl.BlockSpec((d, ns), lambda b, i, j: (0, 0)),
        ],
        out_specs=[
            pl.BlockSpec((None, tm, tn), lambda b, i, j: (b, i, j)),
            pl.BlockSpec((None, tm, ns), lambda b, i, j: (b, i, 0)),
        ],
        out_shape=[jax.ShapeDtypeStruct((b, lt, n), F32), jax.ShapeDtypeStruct((b, lt, ns), F32)],
        scratch_shapes=[pltpu.VMEM((tm, d), BF16)],
        compiler_params=_cparams("parallel", "parallel", "arbitrary"),
        name="inproj",
    )(x, nw, mods, mods, mods, mods, w_main, w_small)


def _outproj_kernel(*refs, n_y, n_lat):
    y_refs, w_refs = refs[:n_y], refs[n_y:2 * n_y]
    x_ref, gl_ref, gc_ref, o_ref = refs[2 * n_y:]
    acc = _dot(y_refs[0][...], w_refs[0][...])
    for y_ref, w_ref in zip(y_refs[1:], w_refs[1:]):
        acc = acc + _dot(y_ref[...], w_ref[...])
    tm = x_ref.shape[0]
    rows = pl.program_id(1) * tm + lax.broadcasted_iota(jnp.int32, (tm, 1), 0)
    g = jnp.where(rows >= n_lat, gc_ref[...], gl_ref[...])
    o_ref[...] = x_ref[...] + g * acc


def _outproj(ys, w, x, mods, layer, ctx_row, n_lat, rows, tm):
    b, lt, d = x.shape
    n_y = len(ys)
    y_specs = [pl.BlockSpec((None, tm, y.shape[2]), lambda b, i: (b, i, 0)) for y in ys]
    w_specs = [pl.BlockSpec((y.shape[2], d), lambda b, i, k=k: (k, 0)) for k, y in enumerate(ys)]
    return pl.pallas_call(
        functools.partial(_outproj_kernel, n_y=n_y, n_lat=n_lat),
        grid=(b, rows // tm),
        in_specs=[*y_specs, *w_specs,
                  pl.BlockSpec((None, tm, d), lambda b, i: (b, i, 0)),
                  *_mod_specs(layer, ctx_row, (2,), d, 2)],
        out_specs=pl.BlockSpec((None, tm, d), lambda b, i: (b, i, 0)),
        out_shape=jax.ShapeDtypeStruct((b, rows, d), F32),
        compiler_params=_cparams("parallel", "parallel"),
        name="outproj",
    )(*ys, *([w] * n_y), x, mods, mods)


def _mlp_kernel(x_ref, nw_ref, shl_ref, shc_ref, scl_ref, scc_ref, gl_ref, gc_ref, w1_ref, w2_ref, fw_ref,
                o_ref, h_ref, acc_ref, *, n_lat, final_norm):
    f = pl.program_id(2)
    tm = x_ref.shape[0]

    @pl.when(f == 0)
    def _():
        _modulated_norm(x_ref, h_ref, nw_ref[...], shl_ref[...], scl_ref[...], shc_ref[...], scc_ref[...],
                        pl.program_id(1) * tm, n_lat)
        acc_ref[...] = jnp.zeros_like(acc_ref)

    a = jnp.maximum(_dot(h_ref[...], w1_ref[...]), 0.0)
    acc_ref[...] += _dot((a * a).astype(BF16), w2_ref[...])

    @pl.when(f == pl.num_programs(2) - 1)
    def _():
        rows = pl.program_id(1) * tm + lax.broadcasted_iota(jnp.int32, (tm, 1), 0)
        g = jnp.where(rows >= n_lat, gc_ref[...], gl_ref[...])
        y = x_ref[...] + g * acc_ref[...]
        if final_norm:
            y = _rms(y) * fw_ref[...]
        o_ref[...] = y


def _mlp(x, nw, mods, layer, ctx_row, w1, w2, fw, n_lat, rows, tm, tf, final_norm):
    b, _, d = x.shape
    dff = w1.shape[1]
    return pl.pallas_call(
        functools.partial(_mlp_kernel, n_lat=n_lat, final_norm=final_norm),
        grid=(b, rows // tm, dff // tf),
        in_specs=[
            pl.BlockSpec((None, tm, d), lambda b, i, j: (b, i, 0)),
            pl.BlockSpec((1, d), lambda b, i, j: (0, 0)),
            *_mod_specs(layer, ctx_row, (3, 4, 5), d, 3),
            pl.BlockSpec((d, tf), lambda b, i, j: (0, j)),
            pl.BlockSpec((tf, d), lambda b, i, j: (j, 0)),
            pl.BlockSpec((1, d), lambda b, i, j: (0, 0)),
        ],
        out_specs=pl.BlockSpec((None, tm, d), lambda b, i, j: (b, i, 0)),
        out_shape=jax.ShapeDtypeStruct((b, rows, d), F32),
        scratch_shapes=[pltpu.VMEM((tm, d), BF16), pltpu.VMEM((tm, d), F32)],
        compiler_params=_cparams("parallel", "parallel", "arbitrary"),
        name="mlp",
    )(x, nw, mods, mods, mods, mods, mods, mods, w1, w2, fw)


def _tri_consts(c):
    lower = np.tril(np.ones((c, c), np.float32))
    tri = np.stack([lower, lower.T])
    return jnp.asarray(np.concatenate([tri, tri, tri], axis=2), BF16)


def _chunk_index(j, d, n_lat_chunks, n_chunks):
    if d == 0:
        c = j + n_lat_chunks
        return jnp.where(c >= n_chunks, c - n_chunks, c)
    return n_chunks - 1 - j


def _conv_rows(n_lat, n_ctx, fn):
    def body(p, carry):
        fn(pl.multiple_of(p * CONV_PIECE, CONV_PIECE), CONV_PIECE, GRID_W)
        return carry
    lax.fori_loop(0, n_lat // CONV_PIECE, body, 0)
    fn(n_lat, n_ctx, n_ctx)


def _ssd_kernel(x_ref, b_ref, c_ref, z_ref, dt_ref, cwx_ref, cwb_ref, cwc_ref, cbx_ref, cbb_ref, cbc_ref,
                dtb_ref, alog_ref, dsk_ref, nw_ref, tri_ref, o_ref,
                xs, bs, cs, dts, acs, ys, hst, *, n_lat, n_ctx):
    ch = SSD_CHUNK
    p = SSD_HEAD_DIM
    heads = x_ref.shape[1] // p
    n_chunks = (n_lat + n_ctx) // ch
    n_lat_chunks = n_lat // ch

    def conv_piece(r0, rows, period):
        sl = pl.ds(r0, rows)
        xs[sl, :] = _silu(_conv3(x_ref[sl, :], cwx_ref[...], cbx_ref[...], period))
        bs[sl, :] = _silu(_conv3(b_ref[sl, :], cwb_ref[...], cbb_ref[...], period)).astype(BF16)
        cs[sl, :] = _silu(_conv3(c_ref[sl, :], cwc_ref[...], cbc_ref[...], period)).astype(BF16)
        dt = _softplus(dt_ref[sl, :] + dtb_ref[...])
        dts[sl, :] = dt
        acs[sl, :] = -jnp.exp(alog_ref[...]) * dt

    _conv_rows(n_lat, n_ctx, conv_piece)

    ti = lax.broadcasted_iota(jnp.int32, (ch, ch), 0)
    si = lax.broadcasted_iota(jnp.int32, (ch, ch), 1)

    for d in range(2):
        mask = (si <= ti) if d == 0 else (si >= ti)
        hst[...] = jnp.zeros_like(hst)

        def body(j, carry, d=d, mask=mask):
            r0 = pl.multiple_of(_chunk_index(j, d, n_lat_chunks, n_chunks) * ch, ch)
            sl = pl.ds(r0, ch)
            cum = _dot01(tri_ref[d], acs[sl, :])
            cum_t = cum.T
            tot = cum[ch - 1:ch, :] if d == 0 else cum[0:1, :]
            e_cum = jnp.exp(cum)
            e_end = jnp.exp(tot - cum) * dts[sl, :]
            e_tot = jnp.exp(tot)
            dt_c = dts[sl, :]
            bc = bs[sl, :]
            cc = cs[sl, :]
            cb = _dot_nt(cc, bc)
            xch = xs[sl, :]
            for h in range(heads):
                col = d * heads + h
                hs = slice(h * p, (h + 1) * p)
                diff = cum[:, col:col + 1] - cum_t[col:col + 1, :]
                decay = jnp.exp(jnp.where(mask, diff, -jnp.inf))
                xh = xch[:, hs]
                xdt = (xh * dt_c[:, col:col + 1]).astype(BF16)
                state = hst[h]
                y = _dot((cb * decay).astype(BF16), xdt)
                y = y + _dot(cc, state.astype(BF16)) * e_cum[:, col:col + 1]
                xw = (xh * e_end[:, col:col + 1]).astype(BF16)
                hst[h] = state * e_tot[:, col:col + 1] + _dot_tn(bc, xw)
                if d == 0:
                    ys[sl, hs] = y
                else:
                    ys[sl, hs] = ys[sl, hs] + y
            if d == 1:
                g = (ys[sl, :] + xch * dsk_ref[...]) * _silu(z_ref[sl, :])
                o_ref[sl, :] = (_rms(g) * nw_ref[...]).astype(BF16)
            return carry

        lax.fori_loop(0, n_chunks, body, 0)


def _ssd(u, us, conv_w, conv_b, dtb, alog, dsk, nw, n_lat, n_ctx):
    b, lt, _ = u.shape
    gw = SSD_HEAD_DIM * 8
    ns = SSD_STATE
    x_blk, b_blk, c_blk = 1024 // gw, 2048 // ns, 2304 // ns
    cx_blk, cb_blk, cc_blk = 0, 1024 // ns, 1280 // ns
    seq = lambda w, off: pl.BlockSpec((None, lt, w), lambda b, g, off=off: (b, 0, off + g))
    par = lambda r, w, off: pl.BlockSpec((r, w), lambda b, g, off=off: (0, off + g))
    grp = lambda w: pl.BlockSpec((None, 1, w), lambda b, g: (g, 0, 0))
    return pl.pallas_call(
        functools.partial(_ssd_kernel, n_lat=n_lat, n_ctx=n_ctx),
        grid=(b, SSD_GROUPS),
        in_specs=[
            seq(gw, x_blk), seq(ns, b_blk), seq(ns, c_blk), seq(gw, 0),
            pl.BlockSpec((None, lt, 128), lambda b, g: (b, 0, g)),
            par(3, gw, cx_blk), par(3, ns, cb_blk), par(3, ns, cc_blk),
            par(1, gw, cx_blk), par(1, ns, cb_blk), par(1, ns, cc_blk),
            grp(128), grp(128), grp(gw), grp(gw),
            pl.BlockSpec((2, SSD_CHUNK, 3 * SSD_CHUNK), lambda b, g: (0, 0, 0)),
        ],
        out_specs=pl.BlockSpec((None, lt, gw), lambda b, g: (b, 0, g)),
        out_shape=jax.ShapeDtypeStruct((b, lt, SSD_GROUPS * gw), BF16),
        scratch_shapes=[
            pltpu.VMEM((lt, gw), F32), pltpu.VMEM((lt, ns), BF16), pltpu.VMEM((lt, ns), BF16),
            pltpu.VMEM((lt, 128), F32), pltpu.VMEM((lt, 128), F32), pltpu.VMEM((lt, gw), F32),
            pltpu.VMEM((8, ns, SSD_HEAD_DIM), F32),
        ],
        compiler_params=_cparams("parallel", "parallel"),
        name="ssd",
    )(u, u, u, u, us, conv_w, conv_w, conv_w, conv_b, conv_b, conv_b, dtb, alog, dsk, nw, _tri_consts(SSD_CHUNK))


_HGRN_LEVELS = (32, 16, 8, 4, 2, 1)


def _hgrn_consts():
    c = HGRN_CHUNK
    sums = np.zeros((7, c, c), np.float32)
    pairs = np.zeros((7, c, c), np.float32)
    for li, m in enumerate(_HGRN_LEVELS):
        for t in range(c):
            beta = (t // (2 * m)) * 2 * m
            mid = beta + m
            if t >= mid:
                sums[li, t, mid:t + 1] = 1.0
                pairs[li, t, beta:mid] = 1.0
            else:
                sums[li, t, t + 1:mid] = 1.0
    sums[6] = np.tril(np.ones((c, c), np.float32))
    pairs[6] = np.eye(c, dtype=np.float32)
    sums = np.stack([sums, sums[:, ::-1, ::-1]]).reshape(2, 7 * c, c)
    sums = np.concatenate([sums, sums, sums, np.zeros_like(sums)], axis=2)
    pairs = np.stack([pairs, pairs[:, ::-1, ::-1]])
    return jnp.asarray(sums, BF16), jnp.asarray(pairs, F32)


def _hgrn_kernel(q_ref, ff_ref, fb_ref, i_ref, g_ref, lb_ref, nw_ref, sums_ref, pairs_ref, o_ref,
                 ys, w_s, qb_s, kb_s, qd_s, kd_s, et_s, att_s, p_s, *, n_lat, n_ctx):
    ch = HGRN_CHUNK
    n_chunks = (n_lat + n_ctx) // ch
    n_lat_chunks = n_lat // ch
    n_lv = len(_HGRN_LEVELS)
    lb = lb_ref[...]
    log_lb = jnp.log(lb)
    log_1mlb = jnp.log1p(-lb)
    one_m_lb = 1.0 - lb
    dk = q_ref.shape[1]
    dv = i_ref.shape[1]
    t_idx = lax.broadcasted_iota(jnp.int32, (ch, dk), 0)

    for d in range(2):
        f_ref = ff_ref if d == 0 else fb_ref
        later = [((t_idx & m) != 0) == (d == 0) for m in _HGRN_LEVELS]

        def operands(pi, carry, d=d, f_ref=f_ref, later=later):
            r0 = pl.multiple_of(pi * 2 * ch, 2 * ch)
            sl = pl.ds(r0, 2 * ch)
            zf = f_ref[sl, :]
            ls = _log_sigmoid(zf)
            gate = log_1mlb + ls
            logf = jnp.maximum(log_lb, gate) + jnp.log(1.0 + jnp.exp(-jnp.abs(log_lb - gate)))
            kin = one_m_lb * jnp.exp(ls - zf)
            q = _silu(q_ref[sl, :])
            qb_s[sl, :] = q.astype(BF16)
            kb_s[sl, :] = kin.astype(BF16)
            logf2 = logf * LOG2_E
            lf = jnp.concatenate([logf2[:ch], logf2[ch:]], axis=1)
            hi = lf.astype(BF16)
            r1 = lf - hi.astype(F32)
            mid = r1.astype(BF16)
            lo = (r1 - mid.astype(F32)).astype(BF16)
            rel2 = _dot(sums_ref[d], jnp.concatenate([hi, mid, lo, jnp.zeros_like(hi)], axis=0))
            for half in range(2):
                rows = slice(half * ch, (half + 1) * ch)
                rel = rel2[:, half * dk:(half + 1) * dk]
                qh, kh = q[rows], kin[rows]
                c = pi * 2 + half
                for li in range(n_lv):
                    e = jnp.exp2(rel[li * ch:(li + 1) * ch])
                    w_s[c, li * ch:(li + 1) * ch, :] = (jnp.where(later[li], qh, kh) * e).astype(BF16)
                bcum = rel[n_lv * ch:(n_lv + 1) * ch]
                tot = bcum[ch - 1:ch] if d == 0 else bcum[0:1]
                hs = pl.ds(r0 + half * ch, ch)
                qd_s[hs, :] = (qh * jnp.exp2(bcum)).astype(BF16)
                kd_s[hs, :] = (kh * jnp.exp2(tot - bcum)).astype(BF16)
                et_s[c] = jnp.broadcast_to(jnp.exp2(tot), (8, dk))
            return carry

        lax.fori_loop(0, n_chunks // 2, operands, 0, unroll=2)

        def intra(c, carry, d=d):
            sl = pl.ds(pl.multiple_of(c * ch, ch), ch)
            att = pairs_ref[d, n_lv] * _dot_nt(qb_s[sl, :], kb_s[sl, :])
            for li in range(n_lv):
                w = w_s[c, li * ch:(li + 1) * ch, :]
                att = att + pairs_ref[d, li] * _dot_nt(w, w)
            att_s[sl, :] = att.astype(BF16)
            p_s[c] = _dot_tn(i_ref[sl, :].astype(BF16), kd_s[sl, :])
            return carry

        lax.fori_loop(0, n_chunks, intra, 0, unroll=6)

        def scan(j, state_t, d=d):
            c = _chunk_index(j, d, n_lat_chunks, n_chunks)
            sl = pl.ds(pl.multiple_of(c * ch, ch), ch)
            o = _dot(att_s[sl, :], i_ref[sl, :].astype(BF16))
            o = o + _dot_nt(qd_s[sl, :], state_t.astype(BF16))
            if d == 0:
                ys[sl, :] = o
            else:
                o = ys[sl, :] + o
                o_ref[sl, :] = (_rms(o) * nw_ref[...] * _silu(g_ref[sl, :])).astype(BF16)
            return state_t * et_s[c][0:1] + p_s[c]

        lax.fori_loop(0, n_chunks, scan, jnp.zeros((dv, dk), F32), unroll=6)


def _hgrn(u, lb, nw, n_lat, n_ctx):
    b, lt, _ = u.shape
    w = 128
    ch = HGRN_CHUNK
    n_chunks = lt // ch
    sums, pairs = _hgrn_consts()
    seq = lambda off: pl.BlockSpec((None, lt, w), lambda b, h, off=off: (b, 0, off + h))
    head = pl.BlockSpec((None, 1, w), lambda b, h: (h, 0, 0))
    return pl.pallas_call(
        functools.partial(_hgrn_kernel, n_lat=n_lat, n_ctx=n_ctx),
        grid=(b, HGRN_HEADS),
        in_specs=[seq(2560 // w), seq(3584 // w), seq(4608 // w), seq(5632 // w), seq(6656 // w), head, head,
                  pl.BlockSpec(sums.shape, lambda b, h: (0, 0, 0)),
                  pl.BlockSpec(pairs.shape, lambda b, h: (0, 0, 0, 0))],
        out_specs=pl.BlockSpec((None, lt, w), lambda b, h: (b, 0, h)),
        out_shape=jax.ShapeDtypeStruct((b, lt, HGRN_HEADS * w), BF16),
        scratch_shapes=[
            pltpu.VMEM((lt, w), F32),
            pltpu.VMEM((n_chunks, len(_HGRN_LEVELS) * ch, w), BF16),
            pltpu.VMEM((lt, w), BF16), pltpu.VMEM((lt, w), BF16), pltpu.VMEM((lt, w), BF16), pltpu.VMEM((lt, w), BF16),
            pltpu.VMEM((n_chunks, 8, w), F32),
            pltpu.VMEM((lt, ch), BF16),
            pltpu.VMEM((n_chunks, w, w), F32),
        ],
        compiler_params=_cparams("parallel", "parallel"),
        name="hgrn2",
    )(u, u, u, u, u, lb, nw, sums, pairs)


def _mlstm_kernel(q_ref, k_ref, v_ref, og_ref, gt_ref, cwq_ref, cwk_ref, cbq_ref, cbk_ref, gb_ref, nw_ref, tri_ref,
                  o_ref, qs, ks, gs, ys, cst, *, n_lat, n_ctx):
    ch = MLSTM_CHUNK
    n_chunks = (n_lat + n_ctx) // ch
    n_lat_chunks = n_lat // ch
    dqk = q_ref.shape[1]
    dv = v_ref.shape[1]
    k_scale = dqk ** -0.5

    def conv_piece(r0, rows, period):
        sl = pl.ds(r0, rows)
        qs[sl, :] = _silu(_conv3(q_ref[sl, :], cwq_ref[...], cbq_ref[...], period)).astype(BF16)
        ks[sl, :] = (_silu(_conv3(k_ref[sl, :], cwk_ref[...], cbk_ref[...], period)) * k_scale).astype(BF16)
        raw = gt_ref[sl, :] + gb_ref[...]
        lane = lax.broadcasted_iota(jnp.int32, raw.shape, 1)
        gs[sl, :] = jnp.where(lane < 2, raw, _log_sigmoid(raw))

    _conv_rows(n_lat, n_ctx, conv_piece)

    ti = lax.broadcasted_iota(jnp.int32, (ch, ch), 0)
    si = lax.broadcasted_iota(jnp.int32, (ch, ch), 1)
    ones_blk = jnp.ones((ch, 128), BF16)

    for d in range(2):
        mask = (si <= ti) if d == 0 else (si >= ti)
        cst[...] = jnp.zeros_like(cst)

        def body(jj, m_prev, d=d, mask=mask):
            nu = MLSTM_INTERLEAVE
            each = lambda f, *xs: [f(*a) for a in zip(*xs)]
            sls = [pl.ds(pl.multiple_of(_chunk_index(jj * nu + u, d, n_lat_chunks, n_chunks) * ch, ch), ch)
                   for u in range(nu)]
            gates = [gs[sl, :] for sl in sls]
            cum = [_dot01(tri_ref[d], g) for g in gates]
            bcol = [c[:, 2 + d:3 + d] for c in cum]
            icol = [g[:, d:d + 1] for g in gates]
            tot = [b[ch - 1:ch] if d == 0 else b[0:1] for b in bcol]
            qc = [qs[sl, :] for sl in sls]
            kc = [ks[sl, :] for sl in sls]
            qk = each(_dot_nt, qc, kc)
            logd = each(lambda b, i: jnp.where(mask, b + jnp.broadcast_to(i - b, (ch, ch)).T, -jnp.inf), bcol, icol)
            rmax = [jnp.max(x, axis=-1, keepdims=True) for x in logd]
            logw = each(lambda t, b, i: t - b + i, tot, bcol, icol)
            lwmax = [jnp.max(x, axis=0, keepdims=True) for x in logw]
            m_in, m = [], m_prev
            for u in range(nu):
                m_in.append(m)
                m = jnp.maximum(tot[u] + m, lwmax[u])
            m_out = m_in[1:] + [m]
            gstate = each(lambda b, mi: b + mi, bcol, m_in)
            mt = each(jnp.maximum, rmax, gstate)
            w = each(lambda x, l, mx: x * jnp.exp(l - mx), qk, logd, mt)
            sw = each(lambda g, mx: jnp.exp(g - mx), gstate, mt)
            vc = [v_ref[sl, :].astype(BF16) for sl in sls]
            vaug = [jnp.concatenate([v, ones_blk], axis=1) for v in vc]
            wv = each(lambda x, v: _dot(x.astype(BF16), v), w, vc)
            kw = each(lambda k, l, mo: (k.astype(F32) * jnp.exp(l - mo)).astype(BF16), kc, logw, m_out)
            inc = each(_dot_tn, kw, vaug)
            keep = each(lambda t, mi, mo: jnp.exp(t + mi - mo), tot, m_in, m_out)
            state = cst[...]
            for u in range(nu):
                qstate = _dot(qc[u], state.astype(BF16))
                num = wv[u] + sw[u] * qstate[:, :dv]
                den = jnp.sum(w[u], axis=-1, keepdims=True) + sw[u] * qstate[:, dv:dv + 1]
                hout = num / jnp.maximum(jnp.abs(den), jnp.exp(-mt[u]))
                if d == 0:
                    ys[sls[u], :] = hout
                else:
                    hh = ys[sls[u], :] + hout
                    o_ref[sls[u], :] = (_rms(hh) * nw_ref[...] * jax.nn.sigmoid(og_ref[sls[u], :])).astype(BF16)
                state = keep[u] * state + inc[u]
            cst[...] = state
            return m

        lax.fori_loop(0, n_chunks // MLSTM_INTERLEAVE, body, jnp.zeros((1, 1), F32))


def _mlstm(u, us, conv_w, conv_b, gate_b, nw, n_lat, n_ctx):
    b, lt, _ = u.shape
    dqk, dv = 256, 512
    nh = MLSTM_HEADS
    seq = lambda w, off: pl.BlockSpec((None, lt, w), lambda b, h, off=off: (b, 0, off + h))
    par = lambda r, off: pl.BlockSpec((r, dqk), lambda b, h, off=off: (0, off + h))
    return pl.pallas_call(
        functools.partial(_mlstm_kernel, n_lat=n_lat, n_ctx=n_ctx),
        grid=(b, nh),
        in_specs=[
            seq(dqk, 0), seq(dqk, nh), seq(dv, 2048 // dv), seq(dv, 4096 // dv), seq(128, 0),
            par(3, 0), par(3, nh), par(1, 0), par(1, nh),
            pl.BlockSpec((None, 1, 128), lambda b, h: (h, 0, 0)),
            pl.BlockSpec((None, 1, dv), lambda b, h: (h, 0, 0)),
            pl.BlockSpec((2, MLSTM_CHUNK, 3 * MLSTM_CHUNK), lambda b, h: (0, 0, 0)),
        ],
        out_specs=pl.BlockSpec((None, lt, dv), lambda b, h: (b, 0, h)),
        out_shape=jax.ShapeDtypeStruct((b, lt, nh * dv), BF16),
        scratch_shapes=[
            pltpu.VMEM((lt, dqk), BF16), pltpu.VMEM((lt, dqk), BF16), pltpu.VMEM((lt, 128), F32),
            pltpu.VMEM((lt, dv), F32), pltpu.VMEM((dqk, dv + 128), F32),
        ],
        compiler_params=_cparams("parallel", "parallel"),
        name="mlstm",
    )(u, u, u, u, us, conv_w, conv_w, conv_b, conv_b, gate_b, nw, _tri_consts(MLSTM_CHUNK))


def _pad_lanes(a, width=128):
    return jnp.pad(a, [(0, 0)] * (a.ndim - 1) + [(0, width - a.shape[-1])])


def _even_params(w_in, dt_bias, a_log, d_skip):
    heads = a_log.shape[1]
    hg = heads // SSD_GROUPS
    w_main = jnp.concatenate([w_in[:, :2560], w_in[:, 2560 + 2 * heads:]], axis=1).astype(BF16)
    dt_w = w_in[:, 2560:2560 + 2 * heads]
    per_group = lambda a: [_pad_lanes(jnp.concatenate([a[..., g * hg:(g + 1) * hg], a[..., heads + g * hg:heads + (g + 1) * hg]], axis=-1))
                           for g in range(SSD_GROUPS)]
    w_small = jnp.concatenate(per_group(dt_w), axis=1).astype(BF16)
    flat = lambda a: a.reshape(1, 2 * heads)
    dtb = jnp.stack(per_group(flat(dt_bias)))
    alog = jnp.stack(per_group(flat(a_log)))
    dsk = jnp.repeat(d_skip, SSD_HEAD_DIM).reshape(SSD_GROUPS, 1, hg * SSD_HEAD_DIM)
    return w_main, w_small, dtb, alog, dsk


def _odd_params(w_in, gate_b):
    nh = MLSTM_HEADS
    w_main = w_in[:, :6144].astype(BF16)
    gw = w_in[:, 6144:]
    w_small = jnp.concatenate([_pad_lanes(gw[:, h::nh]) for h in range(nh)], axis=1).astype(BF16)
    gb = jnp.stack([_pad_lanes(gate_b[:, h].reshape(1, 4)) for h in range(nh)])
    return w_main, w_small, gb


def kernel(x, c, ctx, c_ctx, mod_w, mod_b, norm_w, final_norm_w, mlp_w1, mlp_w2, even_w_in, even_w_out, ssd_conv_w, ssd_conv_b, ssd_a_log, ssd_dt_bias, ssd_d, ssd_norm_w, hgrn_lb, hgrn_norm_w, odd_w_in, odd_w_out, mlstm_conv_w, mlstm_conv_b, mlstm_gate_b, mlstm_norm_w):
    bsz, n_lat, d = x.shape
    n_ctx = ctx.shape[1]
    depth = mod_w.shape[0]
    lt = n_lat + n_ctx
    assert bsz < C_ROWS and n_lat % CONV_PIECE == 0 and n_ctx % SSD_CHUNK == 0 and n_ctx & (n_ctx - 1) == 0
    ctx_row = bsz

    c_all = jnp.zeros((C_ROWS, d), F32).at[:bsz].set(c).at[ctx_row].set(c_ctx)
    mods = _modulation(c_all, mod_w, mod_b).reshape(depth, C_ROWS, N_MOD, 1, d)

    lb_all = jnp.cumsum(jax.nn.softmax(hgrn_lb.astype(F32), axis=0), axis=0)
    lb_all = lb_all - lb_all[0]

    tm_in = lt // 2
    tm_full = lt // 4
    tm_lat = min(512, n_lat)

    xx = jnp.concatenate([x, ctx], axis=1)
    for layer in range(depth):
        last = layer == depth - 1
        nw1 = norm_w[layer, 0].reshape(1, d)
        nw2 = norm_w[layer, 1].reshape(1, d)
        if layer % 2 == 0:
            e = layer // 2
            w_main, w_small, dtb, alog, dsk = _even_params(even_w_in[e], ssd_dt_bias[e], ssd_a_log[e], ssd_d[e])
            u, us = _inproj(xx, nw1, mods, layer, ctx_row, w_main, w_small, n_lat, tm_in, 768)
            ya = _ssd(u, us, ssd_conv_w[e], ssd_conv_b[e].reshape(1, -1), dtb, alog, dsk,
                      ssd_norm_w[e].reshape(SSD_GROUPS, 1, -1), n_lat, n_ctx)
            yb = _hgrn(u, lb_all[e].reshape(HGRN_HEADS, 1, -1), hgrn_norm_w[e].reshape(HGRN_HEADS, 1, -1), n_lat, n_ctx)
            ys, w_out = [ya, yb], even_w_out[e].astype(BF16)
        else:
            o = layer // 2
            w_main, w_small, gb = _odd_params(odd_w_in[o], mlstm_gate_b[o])
            u, us = _inproj(xx, nw1, mods, layer, ctx_row, w_main, w_small, n_lat, tm_in, 768)
            yc = _mlstm(u, us, mlstm_conv_w[o], mlstm_conv_b[o].reshape(1, -1), gb,
                        mlstm_norm_w[o].reshape(MLSTM_HEADS, 1, -1), n_lat, n_ctx)
            ys, w_out = [yc], odd_w_out[o].astype(BF16)
        rows, tm = (n_lat, tm_lat) if last else (lt, tm_full)
        xx = _outproj(ys, w_out, xx, mods, layer, ctx_row, n_lat, rows, tm // 2 if not last else tm)
        xx = _mlp(xx, nw2, mods, layer, ctx_row, mlp_w1[layer].astype(BF16), mlp_w2[layer].astype(BF16),
                  final_norm_w.reshape(1, d), n_lat, rows, tm, 512, last)
    return xx
```

```python
import functools

import numpy as np
import jax
import jax.numpy as jnp
from jax import lax
from jax.experimental import pallas as pl
from jax.experimental.pallas import tpu as pltpu

F32 = jnp.float32
BF16 = jnp.bfloat16

EPS = 1e-6
GRID_W = 64
N_MOD = 6
LOG2_E = 1.4426950408889634
C_ROWS = 32

V7X_VMEM_BYTES = 64 * 1024 * 1024
VMEM_LIMIT = V7X_VMEM_BYTES - 8 * 1024 * 1024

SSD_CHUNK = 128
SSD_HEAD_DIM = 64
SSD_STATE = 128
SSD_GROUPS = 2
HGRN_CHUNK = 64
HGRN_HEADS = 8
MLSTM_CHUNK = 128
MLSTM_HEADS = 4
MLSTM_INTERLEAVE = 2
CONV_PIECE = 256
NORM_ROWS = 16


def _cparams(*sem):
    return pltpu.CompilerParams(dimension_semantics=sem, vmem_limit_bytes=VMEM_LIMIT)


def _silu(x):
    return x * jax.nn.sigmoid(x)


def _softplus(x):
    return jnp.maximum(x, 0.0) + jnp.log(1.0 + jnp.exp(-jnp.abs(x)))


def _log_sigmoid(x):
    return -_softplus(-x)


def _dot(a, b):
    return jnp.dot(a, b, preferred_element_type=F32)


def _dot_nt(a, b):
    return lax.dot_general(a, b, (((1,), (1,)), ((), ())), preferred_element_type=F32)


def _dot_tn(a, b):
    return lax.dot_general(a, b, (((0,), (0,)), ((), ())), preferred_element_type=F32)


def _dot01(m01x3, x):
    hi = x.astype(BF16)
    r = x - hi.astype(F32)
    mid = r.astype(BF16)
    lo = (r - mid.astype(F32)).astype(BF16)
    return _dot(m01x3, jnp.concatenate([hi, mid, lo], axis=0))


def _rms(x):
    return x * lax.rsqrt(jnp.mean(x * x, axis=-1, keepdims=True) + EPS)


def _conv3(u, w, b, period):
    rows = u.shape[0]
    t = lax.broadcasted_iota(jnp.int32, u.shape, 0) & (period - 1)
    left = jnp.where(t == 0, 0.0, pltpu.roll(u, 1, axis=0))
    right = jnp.where(t == period - 1, 0.0, pltpu.roll(u, rows - 1, axis=0))
    return left * w[0:1] + u * w[1:2] + right * w[2:3] + b


def _modulated_norm(x_ref, h_ref, nw, sh_l, sc_l, sh_c, sc_c, row0, n_lat):
    a_l, a_c = nw * (1.0 + sc_l), nw * (1.0 + sc_c)

    def block(r, carry):
        start = pl.multiple_of(r * NORM_ROWS, NORM_ROWS)
        rs = pl.ds(start, NORM_ROWS)
        is_ctx = row0 + start >= n_lat
        h_ref[rs, :] = (_rms(x_ref[rs, :]) * jnp.where(is_ctx, a_c, a_l) + jnp.where(is_ctx, sh_c, sh_l)).astype(BF16)
        return carry

    lax.fori_loop(0, x_ref.shape[0] // NORM_ROWS, block, 0, unroll=4)


def _mod_kernel(c_ref, w_ref, b_ref, o_ref):
    a = _silu(c_ref[...]).astype(BF16)
    o_ref[...] = _dot(a, w_ref[...].astype(BF16)) + b_ref[...]


def _modulation(c_all, mod_w, mod_b):
    depth, d, n = mod_w.shape
    tn = 1024
    return pl.pallas_call(
        _mod_kernel,
        grid=(depth, n // tn),
        in_specs=[
            pl.BlockSpec((C_ROWS, d), lambda l, j: (0, 0)),
            pl.BlockSpec((None, d, tn), lambda l, j: (l, 0, j)),
            pl.BlockSpec((None, 1, tn), lambda l, j: (l, 0, j)),
        ],
        out_specs=pl.BlockSpec((None, C_ROWS, tn), lambda l, j: (l, 0, j)),
        out_shape=jax.ShapeDtypeStruct((depth, C_ROWS, n), F32),
        compiler_params=_cparams("parallel", "parallel"),
        name="modulation",
    )(c_all, mod_w, mod_b.reshape(depth, 1, n))


def _mod_specs(layer, ctx_row, ks, d, nargs):
    specs = []
    for k in ks:
        if nargs == 3:
            specs.append(pl.BlockSpec((None, None, None, 1, d), lambda b, i, j, k=k: (layer, b, k, 0, 0)))
            specs.append(pl.BlockSpec((None, None, None, 1, d), lambda b, i, j, k=k: (layer, ctx_row, k, 0, 0)))
        else:
            specs.append(pl.BlockSpec((None, None, None, 1, d), lambda b, i, k=k: (layer, b, k, 0, 0)))
            specs.append(pl.BlockSpec((None, None, None, 1, d), lambda b, i, k=k: (layer, ctx_row, k, 0, 0)))
    return specs


def _inproj_kernel(x_ref, nw_ref, shl_ref, shc_ref, scl_ref, scc_ref, w_ref, ws_ref, o_ref, os_ref, h_ref, *, n_lat):
    @pl.when(pl.program_id(2) == 0)
    def _():
        _modulated_norm(x_ref, h_ref, nw_ref[...], shl_ref[...], scl_ref[...], shc_ref[...], scc_ref[...],
                        pl.program_id(1) * x_ref.shape[0], n_lat)
        os_ref[...] = _dot(h_ref[...], ws_ref[...])

    o_ref[...] = _dot(h_ref[...], w_ref[...])


def _inproj(x, nw, mods, layer, ctx_row, w_main, w_small, n_lat, tm, tn):
    b, lt, d = x.shape
    n = w_main.shape[1]
    ns = w_small.shape[1]
    return pl.pallas_call(
        functools.partial(_inproj_kernel, n_lat=n_lat),
        grid=(b, lt // tm, n // tn),
        in_specs=[
            pl.BlockSpec((None, tm, d), lambda b, i, j: (b, i, 0)),
            pl.BlockSpec((1, d), lambda b, i, j: (0, 0)),
            *_mod_specs(layer, ctx_row, (0, 1), d, 3),
            pl.BlockSpec((d, tn), lambda b, i, j: (0, j)),
            pl.BlockSpec((d, ns), lambda b, i, j: (0, 0)),
        ],
        out_specs=[
            pl.BlockSpec((None, tm, tn), lambda b, i, j: (b, i, j)),
            pl.BlockSpec((None, tm, ns), lambda b, i, j: (b, i, 0)),
        ],
        out_shape=[jax.ShapeDtypeStruct((b, lt, n), F32), jax.ShapeDtypeStruct((b, lt, ns), F32)],
        scratch_shapes=[pltpu.VMEM((tm, d), BF16)],
        compiler_params=_cparams("parallel", "parallel", "arbitrary"),
        name="inproj",
    )(x, nw, mods, mods, mods, mods, w_main, w_small)


def _outproj_kernel(*refs, n_y, n_lat):
    y_refs, w_refs = refs[:n_y], refs[n_y:2 * n_y]
    x_ref, gl_ref, gc_ref, o_ref = refs[2 * n_y:]
    acc = _dot(y_refs[0][...], w_refs[0][...])
    for y_ref, w_ref in zip(y_refs[1:], w_refs[1:]):
        acc = acc + _dot(y_ref[...], w_ref[...])
    tm = x_ref.shape[0]
    rows = pl.program_id(1) * tm + lax.broadcasted_iota(jnp.int32, (tm, 1), 0)
    g = jnp.where(rows >= n_lat, gc_ref[...], gl_ref[...])
    o_ref[...] = x_ref[...] + g * acc


def _outproj(ys, w, x, mods, layer, ctx_row, n_lat, rows, tm):
    b, lt, d = x.shape
    n_y = len(ys)
    y_specs = [pl.BlockSpec((None, tm, y.shape[2]), lambda b, i: (b, i, 0)) for y in ys]
    w_specs = [pl.BlockSpec((y.shape[2], d), lambda b, i, k=k: (k, 0)) for k, y in enumerate(ys)]
    return pl.pallas_call(
        functools.partial(_outproj_kernel, n_y=n_y, n_lat=n_lat),
        grid=(b, rows // tm),
        in_specs=[*y_specs, *w_specs,
                  pl.BlockSpec((None, tm, d), lambda b, i: (b, i, 0)),
                  *_mod_specs(layer, ctx_row, (2,), d, 2)],
        out_specs=pl.BlockSpec((None, tm, d), lambda b, i: (b, i, 0)),
        out_shape=jax.ShapeDtypeStruct((b, rows, d), F32),
        compiler_params=_cparams("parallel", "parallel"),
        name="outproj",
    )(*ys, *([w] * n_y), x, mods, mods)


def _mlp_kernel(x_ref, nw_ref, shl_ref, shc_ref, scl_ref, scc_ref, gl_ref, gc_ref, w1_ref, w2_ref, fw_ref,
                o_ref, h_ref, acc_ref, *, n_lat, final_norm):
    f = pl.program_id(2)
    tm = x_ref.shape[0]

    @pl.when(f == 0)
    def _():
        _modulated_norm(x_ref, h_ref, nw_ref[...], shl_ref[...], scl_ref[...], shc_ref[...], scc_ref[...],
                        pl.program_id(1) * tm, n_lat)
        acc_ref[...] = jnp.zeros_like(acc_ref)

    a = jnp.maximum(_dot(h_ref[...], w1_ref[...]), 0.0)
    acc_ref[...] += _dot((a * a).astype(BF16), w2_ref[...])

    @pl.when(f == pl.num_programs(2) - 1)
    def _():
        rows = pl.program_id(1) * tm + lax.broadcasted_iota(jnp.int32, (tm, 1), 0)
        g = jnp.where(rows >= n_lat, gc_ref[...], gl_ref[...])
        y = x_ref[...] + g * acc_ref[...]
        if final_norm:
            y = _rms(y) * fw_ref[...]
        o_ref[...] = y


def _mlp(x, nw, mods, layer, ctx_row, w1, w2, fw, n_lat, rows, tm, tf, final_norm):
    b, _, d = x.shape
    dff = w1.shape[1]
    return pl.pallas_call(
        functools.partial(_mlp_kernel, n_lat=n_lat, final_norm=final_norm),
        grid=(b, rows // tm, dff // tf),
        in_specs=[
            pl.BlockSpec((None, tm, d), lambda b, i, j: (b, i, 0)),
            pl.BlockSpec((1, d), lambda b, i, j: (0, 0)),
            *_mod_specs(layer, ctx_row, (3, 4, 5), d, 3),
            pl.BlockSpec((d, tf), lambda b, i, j: (0, j)),
            pl.BlockSpec((tf, d), lambda b, i, j: (j, 0)),
            pl.BlockSpec((1, d), lambda b, i, j: (0, 0)),
        ],
        out_specs=pl.BlockSpec((None, tm, d), lambda b, i, j: (b, i, 0)),
        out_shape=jax.ShapeDtypeStruct((b, rows, d), F32),
        scratch_shapes=[pltpu.VMEM((tm, d), BF16), pltpu.VMEM((tm, d), F32)],
        compiler_params=_cparams("parallel", "parallel", "arbitrary"),
        name="mlp",
    )(x, nw, mods, mods, mods, mods, mods, mods, w1, w2, fw)


def _tri_consts(c):
    lower = np.tril(np.ones((c, c), np.float32))
    tri = np.stack([lower, lower.T])
    return jnp.asarray(np.concatenate([tri, tri, tri], axis=2), BF16)


def _chunk_index(j, d, n_lat_chunks, n_chunks):
    if d == 0:
        c = j + n_lat_chunks
        return jnp.where(c >= n_chunks, c - n_chunks, c)
    return n_chunks - 1 - j


def _conv_rows(n_lat, n_ctx, fn):
    def body(p, carry):
        fn(pl.multiple_of(p * CONV_PIECE, CONV_PIECE), CONV_PIECE, GRID_W)
        return carry
    lax.fori_loop(0, n_lat // CONV_PIECE, body, 0)
    fn(n_lat, n_ctx, n_ctx)


def _ssd_kernel(x_ref, b_ref, c_ref, z_ref, dt_ref, cwx_ref, cwb_ref, cwc_ref, cbx_ref, cbb_ref, cbc_ref,
                dtb_ref, alog_ref, dsk_ref, nw_ref, tri_ref, o_ref,
                xst, bs, cst, dtt, acs, yst, hst, *, n_lat, n_ctx):
    ch = SSD_CHUNK
    p = SSD_HEAD_DIM
    gw = x_ref.shape[1]
    heads = gw // p
    n_chunks = (n_lat + n_ctx) // ch
    n_lat_chunks = n_lat // ch

    def conv_piece(r0, rows, period):
        sl = pl.ds(r0, rows)
        c0 = r0 // ch
        x = _silu(_conv3(x_ref[sl, :], cwx_ref[...], cbx_ref[...], period))
        c = _silu(_conv3(c_ref[sl, :], cwc_ref[...], cbc_ref[...], period))
        bs[sl, :] = _silu(_conv3(b_ref[sl, :], cwb_ref[...], cbb_ref[...], period)).astype(BF16)
        dt = _softplus(dt_ref[sl, :] + dtb_ref[...])
        acs[sl, :] = -jnp.exp(alog_ref[...]) * dt
        for k in range(rows // ch):
            rk = slice(k * ch, (k + 1) * ch)
            for j in range(gw // ch):
                xst[c0 + k, j * ch:(j + 1) * ch, :] = x[rk, j * ch:(j + 1) * ch].T
            cst[c0 + k] = c[rk].T.astype(BF16)
            dtt[c0 + k] = dt[rk].T

    _conv_rows(n_lat, n_ctx, conv_piece)

    si = lax.broadcasted_iota(jnp.int32, (ch, ch), 0)
    ti = lax.broadcasted_iota(jnp.int32, (ch, ch), 1)

    for d in range(2):
        mask = (si <= ti) if d == 0 else (si >= ti)
        hst[...] = jnp.zeros_like(hst)

        def body(j, carry, d=d, mask=mask):
            c = _chunk_index(j, d, n_lat_chunks, n_chunks)
            sl = pl.ds(pl.multiple_of(c * ch, ch), ch)
            cum = _dot01(tri_ref[d], acs[sl, :])
            cum_t = cum.T
            tot_c = cum_t[:, ch - 1:ch] if d == 0 else cum_t[:, 0:1]
            e_tot = jnp.exp(cum[ch - 1:ch, :] if d == 0 else cum[0:1, :])
            dt_t = dtt[c]
            e_cum_t = jnp.exp(cum_t)
            e_end_t = jnp.exp(tot_c - cum_t) * dt_t
            bc = bs[sl, :]
            cc_t = cst[c]
            cb_t = _dot(bc, cc_t)
            for h in range(heads):
                col = d * heads + h
                hs = slice(h * p, (h + 1) * p)
                diff = cum_t[col:col + 1, :] - cum[:, col:col + 1]
                decay = jnp.exp(jnp.where(mask, diff, -jnp.inf))
                xh = xst[c, hs, :]
                state = hst[h]
                y = _dot((xh * dt_t[col:col + 1, :]).astype(BF16), (cb_t * decay).astype(BF16))
                y = y + _dot(state.astype(BF16), cc_t) * e_cum_t[col:col + 1, :]
                xw = (xh * e_end_t[col:col + 1, :]).astype(BF16)
                hst[h] = state * e_tot[:, col:col + 1] + _dot(xw, bc)
                if d == 0:
                    yst[c, hs, :] = y
                else:
                    yst[c, hs, :] = yst[c, hs, :] + y
            if d == 1:
                g_t = yst[c] + xst[c] * dsk_ref[...]
                g = jnp.concatenate([g_t[j * ch:(j + 1) * ch].T for j in range(gw // ch)], axis=1)
                g = g * _silu(z_ref[sl, :])
                o_ref[sl, :] = (_rms(g) * nw_ref[...]).astype(BF16)
            return carry

        lax.fori_loop(0, n_chunks, body, 0, unroll=2)


def _ssd(u, us, conv_w, conv_b, dtb, alog, dsk, nw, n_lat, n_ctx):
    b, lt, _ = u.shape
    gw = SSD_HEAD_DIM * 8
    ns = SSD_STATE
    nc = lt // SSD_CHUNK
    x_blk, b_blk, c_blk = 1024 // gw, 2048 // ns, 2304 // ns
    cx_blk, cb_blk, cc_blk = 0, 1024 // ns, 1280 // ns
    seq = lambda w, off: pl.BlockSpec((None, lt, w), lambda b, g, off=off: (b, 0, off + g))
    par = lambda r, w, off: pl.BlockSpec((r, w), lambda b, g, off=off: (0, off + g))
    grp = lambda w: pl.BlockSpec((None, 1, w), lambda b, g: (g, 0, 0))
    return pl.pallas_call(
        functools.partial(_ssd_kernel, n_lat=n_lat, n_ctx=n_ctx),
        grid=(b, SSD_GROUPS),
        in_specs=[
            seq(gw, x_blk), seq(ns, b_blk), seq(ns, c_blk), seq(gw, 0),
            pl.BlockSpec((None, lt, 128), lambda b, g: (b, 0, g)),
            par(3, gw, cx_blk), par(3, ns, cb_blk), par(3, ns, cc_blk),
            par(1, gw, cx_blk), par(1, ns, cb_blk), par(1, ns, cc_blk),
            grp(128), grp(128), pl.BlockSpec((None, gw, 128), lambda b, g: (g, 0, 0)), grp(gw),
            pl.BlockSpec((2, SSD_CHUNK, 3 * SSD_CHUNK), lambda b, g: (0, 0, 0)),
        ],
        out_specs=pl.BlockSpec((None, lt, gw), lambda b, g: (b, 0, g)),
        out_shape=jax.ShapeDtypeStruct((b, lt, SSD_GROUPS * gw), BF16),
        scratch_shapes=[
            pltpu.VMEM((nc, gw, SSD_CHUNK), F32), pltpu.VMEM((lt, ns), BF16), pltpu.VMEM((nc, ns, SSD_CHUNK), BF16),
            pltpu.VMEM((nc, 128, SSD_CHUNK), F32), pltpu.VMEM((lt, 128), F32), pltpu.VMEM((nc, gw, SSD_CHUNK), F32),
            pltpu.VMEM((gw // SSD_HEAD_DIM, SSD_HEAD_DIM, ns), F32),
        ],
        compiler_params=_cparams("parallel", "parallel"),
        name="ssd",
    )(u, u, u, u, us, conv_w, conv_w, conv_w, conv_b, conv_b, conv_b, dtb, alog, dsk, nw, _tri_consts(SSD_CHUNK))


_HGRN_LEVELS = (32, 16, 8, 4, 2, 1)


def _hgrn_consts():
    c = HGRN_CHUNK
    sums = np.zeros((7, c, c), np.float32)
    pairs = np.zeros((7, c, c), np.float32)
    for li, m in enumerate(_HGRN_LEVELS):
        for t in range(c):
            beta = (t // (2 * m)) * 2 * m
            mid = beta + m
            if t >= mid:
                sums[li, t, mid:t + 1] = 1.0
                pairs[li, t, beta:mid] = 1.0
            else:
                sums[li, t, t + 1:mid] = 1.0
    sums[6] = np.tril(np.ones((c, c), np.float32))
    pairs[6] = np.eye(c, dtype=np.float32)
    sums = np.stack([sums, sums[:, ::-1, ::-1]]).reshape(2, 7 * c, c)
    sums = np.concatenate([sums, sums, sums, np.zeros_like(sums)], axis=2)
    pairs = np.stack([pairs, pairs[:, ::-1, ::-1]])
    return jnp.asarray(sums, BF16), jnp.asarray(pairs, F32)


def _hgrn_kernel(q_ref, ff_ref, fb_ref, i_ref, g_ref, lb_ref, nw_ref, sums_ref, pairs_ref, o_ref,
                 ys, w_s, qb_s, kb_s, qd_s, kd_s, et_s, att_s, p_s, *, n_lat, n_ctx):
    ch = HGRN_CHUNK
    n_chunks = (n_lat + n_ctx) // ch
    n_lat_chunks = n_lat // ch
    n_lv = len(_HGRN_LEVELS)
    lb = lb_ref[...]
    log_lb = jnp.log(lb)
    log_1mlb = jnp.log1p(-lb)
    one_m_lb = 1.0 - lb
    dk = q_ref.shape[1]
    dv = i_ref.shape[1]
    t_idx = lax.broadcasted_iota(jnp.int32, (ch, dk), 0)

    for d in range(2):
        f_ref = ff_ref if d == 0 else fb_ref
        later = [((t_idx & m) != 0) == (d == 0) for m in _HGRN_LEVELS]

        def operands(pi, carry, d=d, f_ref=f_ref, later=later):
            r0 = pl.multiple_of(pi * 2 * ch, 2 * ch)
            sl = pl.ds(r0, 2 * ch)
            zf = f_ref[sl, :]
            ls = _log_sigmoid(zf)
            gate = log_1mlb + ls
            logf = jnp.maximum(log_lb, gate) + jnp.log(1.0 + jnp.exp(-jnp.abs(log_lb - gate)))
            kin = one_m_lb * jnp.exp(ls - zf)
            q = _silu(q_ref[sl, :])
            qb_s[sl, :] = q.astype(BF16)
            kb_s[sl, :] = kin.astype(BF16)
            logf2 = logf * LOG2_E
            lf = jnp.concatenate([logf2[:ch], logf2[ch:]], axis=1)
            hi = lf.astype(BF16)
            r1 = lf - hi.astype(F32)
            mid = r1.astype(BF16)
            lo = (r1 - mid.astype(F32)).astype(BF16)
            rel2 = _dot(sums_ref[d], jnp.concatenate([hi, mid, lo, jnp.zeros_like(hi)], axis=0))
            for half in range(2):
                rows = slice(half * ch, (half + 1) * ch)
                rel = rel2[:, half * dk:(half + 1) * dk]
                qh, kh = q[rows], kin[rows]
                c = pi * 2 + half
                for li in range(n_lv):
                    e = jnp.exp2(rel[li * ch:(li + 1) * ch])
                    w_s[c, li * ch:(li + 1) * ch, :] = (jnp.where(later[li], qh, kh) * e).astype(BF16)
                bcum = rel[n_lv * ch:(n_lv + 1) * ch]
                tot = bcum[ch - 1:ch] if d == 0 else bcum[0:1]
                hs = pl.ds(r0 + half * ch, ch)
                qd_s[hs, :] = (qh * jnp.exp2(bcum)).astype(BF16)
                kd_s[hs, :] = (kh * jnp.exp2(tot - bcum)).astype(BF16)
                et_s[c] = jnp.broadcast_to(jnp.exp2(tot), (8, dk))
            return carry

        lax.fori_loop(0, n_chunks // 2, operands, 0, unroll=2)

        def intra(c, carry, d=d):
            sl = pl.ds(pl.multiple_of(c * ch, ch), ch)
            att = pairs_ref[d, n_lv] * _dot_nt(qb_s[sl, :], kb_s[sl, :])
            for li in range(n_lv):
                w = w_s[c, li * ch:(li + 1) * ch, :]
                att = att + pairs_ref[d, li] * _dot_nt(w, w)
            att_s[sl, :] = att.astype(BF16)
            p_s[c] = _dot_tn(i_ref[sl, :].astype(BF16), kd_s[sl, :])
            return carry

        lax.fori_loop(0, n_chunks, intra, 0, unroll=6)

        def scan(j, state_t, d=d):
            c = _chunk_index(j, d, n_lat_chunks, n_chunks)
            sl = pl.ds(pl.multiple_of(c * ch, ch), ch)
            o = _dot(att_s[sl, :], i_ref[sl, :].astype(BF16))
            o = o + _dot_nt(qd_s[sl, :], state_t.astype(BF16))
            if d == 0:
                ys[sl, :] = o
            else:
                o = ys[sl, :] + o
                o_ref[sl, :] = (_rms(o) * nw_ref[...] * _silu(g_ref[sl, :])).astype(BF16)
            return state_t * et_s[c][0:1] + p_s[c]

        lax.fori_loop(0, n_chunks, scan, jnp.zeros((dv, dk), F32), unroll=6)


def _hgrn(u, lb, nw, n_lat, n_ctx):
    b, lt, _ = u.shape
    w = 128
    ch = HGRN_CHUNK
    n_chunks = lt // ch
    sums, pairs = _hgrn_consts()
    seq = lambda off: pl.BlockSpec((None, lt, w), lambda b, h, off=off: (b, 0, off + h))
    head = pl.BlockSpec((None, 1, w), lambda b, h: (h, 0, 0))
    return pl.pallas_call(
        functools.partial(_hgrn_kernel, n_lat=n_lat, n_ctx=n_ctx),
        grid=(b, HGRN_HEADS),
        in_specs=[seq(2560 // w), seq(3584 // w), seq(4608 // w), seq(5632 // w), seq(6656 // w), head, head,
                  pl.BlockSpec(sums.shape, lambda b, h: (0, 0, 0)),
                  pl.BlockSpec(pairs.shape, lambda b, h: (0, 0, 0, 0))],
        out_specs=pl.BlockSpec((None, lt, w), lambda b, h: (b, 0, h)),
        out_shape=jax.ShapeDtypeStruct((b, lt, HGRN_HEADS * w), BF16),
        scratch_shapes=[
            pltpu.VMEM((lt, w), F32),
            pltpu.VMEM((n_chunks, len(_HGRN_LEVELS) * ch, w), BF16),
            pltpu.VMEM((lt, w), BF16), pltpu.VMEM((lt, w), BF16), pltpu.VMEM((lt, w), BF16), pltpu.VMEM((lt, w), BF16),
            pltpu.VMEM((n_chunks, 8, w), F32),
            pltpu.VMEM((lt, ch), BF16),
            pltpu.VMEM((n_chunks, w, w), F32),
        ],
        compiler_params=_cparams("parallel", "parallel"),
        name="hgrn2",
    )(u, u, u, u, u, lb, nw, sums, pairs)


def _mlstm_kernel(q_ref, k_ref, v_ref, og_ref, gt_ref, cwq_ref, cwk_ref, cbq_ref, cbk_ref, gb_ref, nw_ref, tri_ref,
                  o_ref, qs, ks, gs, ys, cst, *, n_lat, n_ctx):
    ch = MLSTM_CHUNK
    n_chunks = (n_lat + n_ctx) // ch
    n_lat_chunks = n_lat // ch
    dqk = q_ref.shape[1]
    dv = v_ref.shape[1]
    k_scale = dqk ** -0.5

    def conv_piece(r0, rows, period):
        sl = pl.ds(r0, rows)
        qs[sl, :] = _silu(_conv3(q_ref[sl, :], cwq_ref[...], cbq_ref[...], period)).astype(BF16)
        ks[sl, :] = (_silu(_conv3(k_ref[sl, :], cwk_ref[...], cbk_ref[...], period)) * k_scale).astype(BF16)
        raw = gt_ref[sl, :] + gb_ref[...]
        lane = lax.broadcasted_iota(jnp.int32, raw.shape, 1)
        gs[sl, :] = jnp.where(lane < 2, raw, _log_sigmoid(raw))

    _conv_rows(n_lat, n_ctx, conv_piece)

    ti = lax.broadcasted_iota(jnp.int32, (ch, ch), 0)
    si = lax.broadcasted_iota(jnp.int32, (ch, ch), 1)
    ones_blk = jnp.ones((ch, 128), BF16)

    for d in range(2):
        mask = (si <= ti) if d == 0 else (si >= ti)
        cst[...] = jnp.zeros_like(cst)

        def body(jj, m_prev, d=d, mask=mask):
            nu = MLSTM_INTERLEAVE
            each = lambda f, *xs: [f(*a) for a in zip(*xs)]
            sls = [pl.ds(pl.multiple_of(_chunk_index(jj * nu + u, d, n_lat_chunks, n_chunks) * ch, ch), ch)
                   for u in range(nu)]
            gates = [gs[sl, :] for sl in sls]
            cum = [_dot01(tri_ref[d], g) for g in gates]
            bcol = [c[:, 2 + d:3 + d] for c in cum]
            icol = [g[:, d:d + 1] for g in gates]
            tot = [b[ch - 1:ch] if d == 0 else b[0:1] for b in bcol]
            qc = [qs[sl, :] for sl in sls]
            kc = [ks[sl, :] for sl in sls]
            qk = each(_dot_nt, qc, kc)
            logd = each(lambda b, i: jnp.where(mask, b + jnp.broadcast_to(i - b, (ch, ch)).T, -jnp.inf), bcol, icol)
            rmax = [jnp.max(x, axis=-1, keepdims=True) for x in logd]
            logw = each(lambda t, b, i: t - b + i, tot, bcol, icol)
            lwmax = [jnp.max(x, axis=0, keepdims=True) for x in logw]
            m_in, m = [], m_prev
            for u in range(nu):
                m_in.append(m)
                m = jnp.maximum(tot[u] + m, lwmax[u])
            m_out = m_in[1:] + [m]
            gstate = each(lambda b, mi: b + mi, bcol, m_in)
            mt = each(jnp.maximum, rmax, gstate)
            w = each(lambda x, l, mx: x * jnp.exp(l - mx), qk, logd, mt)
            sw = each(lambda g, mx: jnp.exp(g - mx), gstate, mt)
            vc = [v_ref[sl, :].astype(BF16) for sl in sls]
            vaug = [jnp.concatenate([v, ones_blk], axis=1) for v in vc]
            wv = each(lambda x, v: _dot(x.astype(BF16), v), w, vc)
            kw = each(lambda k, l, mo: (k.astype(F32) * jnp.exp(l - mo)).astype(BF16), kc, logw, m_out)
            inc = each(_dot_tn, kw, vaug)
            keep = each(lambda t, mi, mo: jnp.exp(t + mi - mo), tot, m_in, m_out)
            state = cst[...]
            for u in range(nu):
                qstate = _dot(qc[u], state.astype(BF16))
                num = wv[u] + sw[u] * qstate[:, :dv]
                den = jnp.sum(w[u], axis=-1, keepdims=True) + sw[u] * qstate[:, dv:dv + 1]
                hout = num / jnp.maximum(jnp.abs(den), jnp.exp(-mt[u]))
                if d == 0:
                    ys[sls[u], :] = hout
                else:
                    hh = ys[sls[u], :] + hout
                    o_ref[sls[u], :] = (_rms(hh) * nw_ref[...] * jax.nn.sigmoid(og_ref[sls[u], :])).astype(BF16)
                state = keep[u] * state + inc[u]
            cst[...] = state
            return m

        lax.fori_loop(0, n_chunks // MLSTM_INTERLEAVE, body, jnp.zeros((1, 1), F32))


def _mlstm(u, us, conv_w, conv_b, gate_b, nw, n_lat, n_ctx):
    b, lt, _ = u.shape
    dqk, dv = 256, 512
    nh = MLSTM_HEADS
    seq = lambda w, off: pl.BlockSpec((None, lt, w), lambda b, h, off=off: (b, 0, off + h))
    par = lambda r, off: pl.BlockSpec((r, dqk), lambda b, h, off=off: (0, off + h))
    return pl.pallas_call(
        functools.partial(_mlstm_kernel, n_lat=n_lat, n_ctx=n_ctx),
        grid=(b, nh),
        in_specs=[
            seq(dqk, 0), seq(dqk, nh), seq(dv, 2048 // dv), seq(dv, 4096 // dv), seq(128, 0),
            par(3, 0), par(3, nh), par(1, 0), par(1, nh),
            pl.BlockSpec((None, 1, 128), lambda b, h: (h, 0, 0)),
            pl.BlockSpec((None, 1, dv), lambda b, h: (h, 0, 0)),
            pl.BlockSpec((2, MLSTM_CHUNK, 3 * MLSTM_CHUNK), lambda b, h: (0, 0, 0)),
        ],
        out_specs=pl.BlockSpec((None, lt, dv), lambda b, h: (b, 0, h)),
        out_shape=jax.ShapeDtypeStruct((b, lt, nh * dv), BF16),
        scratch_shapes=[
            pltpu.VMEM((lt, dqk), BF16), pltpu.VMEM((lt, dqk), BF16), pltpu.VMEM((lt, 128), F32),
            pltpu.VMEM((lt, dv), F32), pltpu.VMEM((dqk, dv + 128), F32),
        ],
        compiler_params=_cparams("parallel", "parallel"),
        name="mlstm",
    )(u, u, u, u, us, conv_w, conv_w, conv_b, conv_b, gate_b, nw, _tri_consts(MLSTM_CHUNK))


def _pad_lanes(a, width=128):
    return jnp.pad(a, [(0, 0)] * (a.ndim - 1) + [(0, width - a.shape[-1])])


def _even_params(w_in, dt_bias, a_log, d_skip):
    heads = a_log.shape[1]
    hg = heads // SSD_GROUPS
    w_main = jnp.concatenate([w_in[:, :2560], w_in[:, 2560 + 2 * heads:]], axis=1).astype(BF16)
    dt_w = w_in[:, 2560:2560 + 2 * heads]
    per_group = lambda a: [_pad_lanes(jnp.concatenate([a[..., g * hg:(g + 1) * hg], a[..., heads + g * hg:heads + (g + 1) * hg]], axis=-1))
                           for g in range(SSD_GROUPS)]
    w_small = jnp.concatenate(per_group(dt_w), axis=1).astype(BF16)
    flat = lambda a: a.reshape(1, 2 * heads)
    dtb = jnp.stack(per_group(flat(dt_bias)))
    alog = jnp.stack(per_group(flat(a_log)))
    dsk = jnp.broadcast_to(jnp.repeat(d_skip, SSD_HEAD_DIM).reshape(SSD_GROUPS, hg * SSD_HEAD_DIM, 1),
                           (SSD_GROUPS, hg * SSD_HEAD_DIM, 128))
    return w_main, w_small, dtb, alog, dsk


def _odd_params(w_in, gate_b):
    nh = MLSTM_HEADS
    w_main = w_in[:, :6144].astype(BF16)
    gw = w_in[:, 6144:]
    w_small = jnp.concatenate([_pad_lanes(gw[:, h::nh]) for h in range(nh)], axis=1).astype(BF16)
    gb = jnp.stack([_pad_lanes(gate_b[:, h].reshape(1, 4)) for h in range(nh)])
    return w_main, w_small, gb


def kernel(x, c, ctx, c_ctx, mod_w, mod_b, norm_w, final_norm_w, mlp_w1, mlp_w2, even_w_in, even_w_out, ssd_conv_w, ssd_conv_b, ssd_a_log, ssd_dt_bias, ssd_d, ssd_norm_w, hgrn_lb, hgrn_norm_w, odd_w_in, odd_w_out, mlstm_conv_w, mlstm_conv_b, mlstm_gate_b, mlstm_norm_w):
    bsz, n_lat, d = x.shape
    n_ctx = ctx.shape[1]
    depth = mod_w.shape[0]
    lt = n_lat + n_ctx
    assert bsz < C_ROWS and n_lat % CONV_PIECE == 0 and n_ctx % SSD_CHUNK == 0 and n_ctx & (n_ctx - 1) == 0
    ctx_row = bsz

    c_all = jnp.zeros((C_ROWS, d), F32).at[:bsz].set(c).at[ctx_row].set(c_ctx)
    mods = _modulation(c_all, mod_w, mod_b).reshape(depth, C_ROWS, N_MOD, 1, d)

    lb_all = jnp.cumsum(jax.nn.softmax(hgrn_lb.astype(F32), axis=0), axis=0)
    lb_all = lb_all - lb_all[0]

    tm_in = lt // 2
    tm_full = lt // 4
    tm_lat = min(512, n_lat)

    xx = jnp.concatenate([x, ctx], axis=1)
    for layer in range(depth):
        last = layer == depth - 1
        nw1 = norm_w[layer, 0].reshape(1, d)
        nw2 = norm_w[layer, 1].reshape(1, d)
        if layer % 2 == 0:
            e = layer // 2
            w_main, w_small, dtb, alog, dsk = _even_params(even_w_in[e], ssd_dt_bias[e], ssd_a_log[e], ssd_d[e])
            u, us = _inproj(xx, nw1, mods, layer, ctx_row, w_main, w_small, n_lat, tm_in, 768)
            ya = _ssd(u, us, ssd_conv_w[e], ssd_conv_b[e].reshape(1, -1), dtb, alog, dsk,
                      ssd_norm_w[e].reshape(SSD_GROUPS, 1, -1), n_lat, n_ctx)
            yb = _hgrn(u, lb_all[e].reshape(HGRN_HEADS, 1, -1), hgrn_norm_w[e].reshape(HGRN_HEADS, 1, -1), n_lat, n_ctx)
            ys, w_out = [ya, yb], even_w_out[e].astype(BF16)
        else:
            o = layer // 2
            w_main, w_small, gb = _odd_params(odd_w_in[o], mlstm_gate_b[o])
            u, us = _inproj(xx, nw1, mods, layer, ctx_row, w_main, w_small, n_lat, tm_in, 768)
            yc = _mlstm(u, us, mlstm_conv_w[o], mlstm_conv_b[o].reshape(1, -1), gb,
                        mlstm_norm_w[o].reshape(MLSTM_HEADS, 1, -1), n_lat, n_ctx)
            ys, w_out = [yc], odd_w_out[o].astype(BF16)
        rows, tm = (n_lat, tm_lat) if last else (lt, tm_full)
        xx = _outproj(ys, w_out, xx, mods, layer, ctx_row, n_lat, rows, tm // 2 if not last else tm)
        xx = _mlp(xx, nw2, mods, layer, ctx_row, mlp_w1[layer].astype(BF16), mlp_w2[layer].astype(BF16),
                  final_norm_w.reshape(1, d), n_lat, rows, tm, 1024, last)
    return xx
```

```python
import functools

import numpy as np
import jax
import jax.numpy as jnp
from jax import lax
from jax.experimental import pallas as pl
from jax.experimental.pallas import tpu as pltpu

F32 = jnp.float32
BF16 = jnp.bfloat16

EPS = 1e-6
GRID_W = 64
N_MOD = 6
LOG2_E = 1.4426950408889634
C_ROWS = 32

V7X_VMEM_BYTES = 64 * 1024 * 1024
VMEM_LIMIT = V7X_VMEM_BYTES - 8 * 1024 * 1024

SSD_CHUNK = 128
SSD_HEAD_DIM = 64
SSD_STATE = 128
SSD_GROUPS = 2
HGRN_CHUNK = 64
HGRN_HEADS = 8
MLSTM_CHUNK = 128
MLSTM_HEADS = 4
MLSTM_ONES_ROWS = 16
CONV_PIECE = 256
NORM_ROWS = 16


def _cparams(*sem):
    return pltpu.CompilerParams(dimension_semantics=sem, vmem_limit_bytes=VMEM_LIMIT)


def _silu(x):
    return x * jax.nn.sigmoid(x)


def _softplus(x):
    return jnp.maximum(x, 0.0) + jnp.log(1.0 + jnp.exp(-jnp.abs(x)))


def _log_sigmoid(x):
    return -_softplus(-x)


def _dot(a, b):
    return jnp.dot(a, b, preferred_element_type=F32)


def _dot_nt(a, b):
    return lax.dot_general(a, b, (((1,), (1,)), ((), ())), preferred_element_type=F32)


def _dot_tn(a, b):
    return lax.dot_general(a, b, (((0,), (0,)), ((), ())), preferred_element_type=F32)


def _dot01(m01x3, x):
    hi = x.astype(BF16)
    r = x - hi.astype(F32)
    mid = r.astype(BF16)
    lo = (r - mid.astype(F32)).astype(BF16)
    return _dot(m01x3, jnp.concatenate([hi, mid, lo], axis=0))


def _dot01_rows(x, m01x3):
    hi = x.astype(BF16)
    r = x - hi.astype(F32)
    mid = r.astype(BF16)
    lo = (r - mid.astype(F32)).astype(BF16)
    return _dot(jnp.concatenate([hi, mid, lo], axis=1), m01x3)


def _rms(x):
    return x * lax.rsqrt(jnp.mean(x * x, axis=-1, keepdims=True) + EPS)


def _conv3(u, w, b, period):
    rows = u.shape[0]
    t = lax.broadcasted_iota(jnp.int32, u.shape, 0) & (period - 1)
    left = jnp.where(t == 0, 0.0, pltpu.roll(u, 1, axis=0))
    right = jnp.where(t == period - 1, 0.0, pltpu.roll(u, rows - 1, axis=0))
    return left * w[0:1] + u * w[1:2] + right * w[2:3] + b


def _modulated_norm(x_ref, h_ref, nw, sh_l, sc_l, sh_c, sc_c, row0, n_lat):
    a_l, a_c = nw * (1.0 + sc_l), nw * (1.0 + sc_c)

    def block(r, carry):
        start = pl.multiple_of(r * NORM_ROWS, NORM_ROWS)
        rs = pl.ds(start, NORM_ROWS)
        is_ctx = row0 + start >= n_lat
        h_ref[rs, :] = (_rms(x_ref[rs, :]) * jnp.where(is_ctx, a_c, a_l) + jnp.where(is_ctx, sh_c, sh_l)).astype(BF16)
        return carry

    lax.fori_loop(0, x_ref.shape[0] // NORM_ROWS, block, 0, unroll=4)


def _mod_kernel(c_ref, w_ref, b_ref, o_ref):
    a = _silu(c_ref[...]).astype(BF16)
    o_ref[...] = _dot(a, w_ref[...].astype(BF16)) + b_ref[...]


def _modulation(c_all, mod_w, mod_b):
    depth, d, n = mod_w.shape
    tn = 1024
    return pl.pallas_call(
        _mod_kernel,
        grid=(depth, n // tn),
        in_specs=[
            pl.BlockSpec((C_ROWS, d), lambda l, j: (0, 0)),
            pl.BlockSpec((None, d, tn), lambda l, j: (l, 0, j)),
            pl.BlockSpec((None, 1, tn), lambda l, j: (l, 0, j)),
        ],
        out_specs=pl.BlockSpec((None, C_ROWS, tn), lambda l, j: (l, 0, j)),
        out_shape=jax.ShapeDtypeStruct((depth, C_ROWS, n), F32),
        compiler_params=_cparams("parallel", "parallel"),
        name="modulation",
    )(c_all, mod_w, mod_b.reshape(depth, 1, n))


def _mod_specs(layer, ctx_row, ks, d, nargs):
    specs = []
    for k in ks:
        if nargs == 3:
            specs.append(pl.BlockSpec((None, None, None, 1, d), lambda b, i, j, k=k: (layer, b, k, 0, 0)))
            specs.append(pl.BlockSpec((None, None, None, 1, d), lambda b, i, j, k=k: (layer, ctx_row, k, 0, 0)))
        else:
            specs.append(pl.BlockSpec((None, None, None, 1, d), lambda b, i, k=k: (layer, b, k, 0, 0)))
            specs.append(pl.BlockSpec((None, None, None, 1, d), lambda b, i, k=k: (layer, ctx_row, k, 0, 0)))
    return specs


def _inproj_kernel(x_ref, nw_ref, shl_ref, shc_ref, scl_ref, scc_ref, w_ref, ws_ref, o_ref, os_ref, h_ref, *, n_lat):
    @pl.when(pl.program_id(2) == 0)
    def _():
        _modulated_norm(x_ref, h_ref, nw_ref[...], shl_ref[...], scl_ref[...], shc_ref[...], scc_ref[...],
                        pl.program_id(1) * x_ref.shape[0], n_lat)
        os_ref[...] = _dot(h_ref[...], ws_ref[...])

    o_ref[...] = _dot(h_ref[...], w_ref[...])


def _inproj(x, nw, mods, layer, ctx_row, w_main, w_small, n_lat, tm, tn):
    b, lt, d = x.shape
    n = w_main.shape[1]
    ns = w_small.shape[1]
    return pl.pallas_call(
        functools.partial(_inproj_kernel, n_lat=n_lat),
        grid=(b, lt // tm, n // tn),
        in_specs=[
            pl.BlockSpec((None, tm, d), lambda b, i, j: (b, i, 0)),
            pl.BlockSpec((1, d), lambda b, i, j: (0, 0)),
            *_mod_specs(layer, ctx_row, (0, 1), d, 3),
            pl.BlockSpec((d, tn), lambda b, i, j: (0, j)),
            pl.BlockSpec((d, ns), lambda b, i, j: (0, 0)),
        ],
        out_specs=[
            pl.BlockSpec((None, tm, tn), lambda b, i, j: (b, i, j)),
            pl.BlockSpec((None, tm, ns), lambda b, i, j: (b, i, 0)),
        ],
        out_shape=[jax.ShapeDtypeStruct((b, lt, n), F32), jax.ShapeDtypeStruct((b, lt, ns), F32)],
        scratch_shapes=[pltpu.VMEM((tm, d), BF16)],
        compiler_params=_cparams("parallel", "parallel", "arbitrary"),
        name="inproj",
    )(x, nw, mods, mods, mods, mods, w_main, w_small)


def _outproj_kernel(*refs, n_y, n_lat):
    y_refs, w_refs = refs[:n_y], refs[n_y:2 * n_y]
    x_ref, gl_ref, gc_ref, o_ref = refs[2 * n_y:]
    acc = _dot(y_refs[0][...], w_refs[0][...])
    for y_ref, w_ref in zip(y_refs[1:], w_refs[1:]):
        acc = acc + _dot(y_ref[...], w_ref[...])
    tm = x_ref.shape[0]
    rows = pl.program_id(1) * tm + lax.broadcasted_iota(jnp.int32, (tm, 1), 0)
    g = jnp.where(rows >= n_lat, gc_ref[...], gl_ref[...])
    o_ref[...] = x_ref[...] + g * acc


def _outproj(ys, w, x, mods, layer, ctx_row, n_lat, rows, tm):
    b, lt, d = x.shape
    n_y = len(ys)
    y_specs = [pl.BlockSpec((None, tm, y.shape[2]), lambda b, i: (b, i, 0)) for y in ys]
    w_specs = [pl.BlockSpec((y.shape[2], d), lambda b, i, k=k: (k, 0)) for k, y in enumerate(ys)]
    return pl.pallas_call(
        functools.partial(_outproj_kernel, n_y=n_y, n_lat=n_lat),
        grid=(b, rows // tm),
        in_specs=[*y_specs, *w_specs,
                  pl.BlockSpec((None, tm, d), lambda b, i: (b, i, 0)),
                  *_mod_specs(layer, ctx_row, (2,), d, 2)],
        out_specs=pl.BlockSpec((None, tm, d), lambda b, i: (b, i, 0)),
        out_shape=jax.ShapeDtypeStruct((b, rows, d), F32),
        compiler_params=_cparams("parallel", "parallel"),
        name="outproj",
    )(*ys, *([w] * n_y), x, mods, mods)


def _mlp_kernel(x_ref, nw_ref, shl_ref, shc_ref, scl_ref, scc_ref, gl_ref, gc_ref, w1_ref, w2_ref, fw_ref,
                o_ref, h_ref, acc_ref, *, n_lat, final_norm):
    f = pl.program_id(2)
    tm = x_ref.shape[0]

    @pl.when(f == 0)
    def _():
        _modulated_norm(x_ref, h_ref, nw_ref[...], shl_ref[...], scl_ref[...], shc_ref[...], scc_ref[...],
                        pl.program_id(1) * tm, n_lat)
        acc_ref[...] = jnp.zeros_like(acc_ref)

    a = jnp.maximum(_dot(h_ref[...], w1_ref[...]), 0.0)
    acc_ref[...] += _dot((a * a).astype(BF16), w2_ref[...])

    @pl.when(f == pl.num_programs(2) - 1)
    def _():
        rows = pl.program_id(1) * tm + lax.broadcasted_iota(jnp.int32, (tm, 1), 0)
        g = jnp.where(rows >= n_lat, gc_ref[...], gl_ref[...])
        y = x_ref[...] + g * acc_ref[...]
        if final_norm:
            y = _rms(y) * fw_ref[...]
        o_ref[...] = y


def _mlp(x, nw, mods, layer, ctx_row, w1, w2, fw, n_lat, rows, tm, tf, final_norm):
    b, _, d = x.shape
    dff = w1.shape[1]
    return pl.pallas_call(
        functools.partial(_mlp_kernel, n_lat=n_lat, final_norm=final_norm),
        grid=(b, rows // tm, dff // tf),
        in_specs=[
            pl.BlockSpec((None, tm, d), lambda b, i, j: (b, i, 0)),
            pl.BlockSpec((1, d), lambda b, i, j: (0, 0)),
            *_mod_specs(layer, ctx_row, (3, 4, 5), d, 3),
            pl.BlockSpec((d, tf), lambda b, i, j: (0, j)),
            pl.BlockSpec((tf, d), lambda b, i, j: (j, 0)),
            pl.BlockSpec((1, d), lambda b, i, j: (0, 0)),
        ],
        out_specs=pl.BlockSpec((None, tm, d), lambda b, i, j: (b, i, 0)),
        out_shape=jax.ShapeDtypeStruct((b, rows, d), F32),
        scratch_shapes=[pltpu.VMEM((tm, d), BF16), pltpu.VMEM((tm, d), F32)],
        compiler_params=_cparams("parallel", "parallel", "arbitrary"),
        name="mlp",
    )(x, nw, mods, mods, mods, mods, mods, mods, w1, w2, fw)


def _tri_consts(c):
    lower = np.tril(np.ones((c, c), np.float32))
    tri = np.stack([lower, lower.T])
    return jnp.asarray(np.concatenate([tri, tri, tri], axis=2), BF16)


def _chunk_index(j, d, n_lat_chunks, n_chunks):
    if d == 0:
        c = j + n_lat_chunks
        return jnp.where(c >= n_chunks, c - n_chunks, c)
    return n_chunks - 1 - j


def _conv_rows(n_lat, n_ctx, fn):
    def body(p, carry):
        fn(pl.multiple_of(p * CONV_PIECE, CONV_PIECE), CONV_PIECE, GRID_W)
        return carry
    lax.fori_loop(0, n_lat // CONV_PIECE, body, 0)
    fn(n_lat, n_ctx, n_ctx)


def _ssd_kernel(x_ref, b_ref, c_ref, z_ref, dt_ref, cwx_ref, cwb_ref, cwc_ref, cbx_ref, cbb_ref, cbc_ref,
                dtb_ref, alog_ref, dsk_ref, nw_ref, tri_ref, o_ref,
                xst, bs, cst, dtt, acs, yst, hst, *, n_lat, n_ctx):
    ch = SSD_CHUNK
    p = SSD_HEAD_DIM
    gw = x_ref.shape[1]
    heads = gw // p
    n_chunks = (n_lat + n_ctx) // ch
    n_lat_chunks = n_lat // ch

    def conv_piece(r0, rows, period):
        sl = pl.ds(r0, rows)
        c0 = r0 // ch
        x = _silu(_conv3(x_ref[sl, :], cwx_ref[...], cbx_ref[...], period))
        c = _silu(_conv3(c_ref[sl, :], cwc_ref[...], cbc_ref[...], period))
        bs[sl, :] = _silu(_conv3(b_ref[sl, :], cwb_ref[...], cbb_ref[...], period)).astype(BF16)
        dt = _softplus(dt_ref[sl, :] + dtb_ref[...])
        acs[sl, :] = -jnp.exp(alog_ref[...]) * dt
        for k in range(rows // ch):
            rk = slice(k * ch, (k + 1) * ch)
            for j in range(gw // ch):
                xst[c0 + k, j * ch:(j + 1) * ch, :] = x[rk, j * ch:(j + 1) * ch].T
            cst[c0 + k] = c[rk].T.astype(BF16)
            dtt[c0 + k] = dt[rk].T

    _conv_rows(n_lat, n_ctx, conv_piece)

    si = lax.broadcasted_iota(jnp.int32, (ch, ch), 0)
    ti = lax.broadcasted_iota(jnp.int32, (ch, ch), 1)

    for d in range(2):
        mask = (si <= ti) if d == 0 else (si >= ti)
        hst[...] = jnp.zeros_like(hst)

        def body(j, carry, d=d, mask=mask):
            c = _chunk_index(j, d, n_lat_chunks, n_chunks)
            sl = pl.ds(pl.multiple_of(c * ch, ch), ch)
            cum = _dot01(tri_ref[d], acs[sl, :])
            cum_t = cum.T
            tot_c = cum_t[:, ch - 1:ch] if d == 0 else cum_t[:, 0:1]
            e_tot = jnp.exp(cum[ch - 1:ch, :] if d == 0 else cum[0:1, :])
            dt_t = dtt[c]
            e_cum_t = jnp.exp(cum_t)
            e_end_t = jnp.exp(tot_c - cum_t) * dt_t
            bc = bs[sl, :]
            cc_t = cst[c]
            cb_t = _dot(bc, cc_t)
            for h in range(heads):
                col = d * heads + h
                hs = slice(h * p, (h + 1) * p)
                diff = cum_t[col:col + 1, :] - cum[:, col:col + 1]
                decay = jnp.exp(jnp.where(mask, diff, -jnp.inf))
                xh = xst[c, hs, :]
                state = hst[h]
                y = _dot((xh * dt_t[col:col + 1, :]).astype(BF16), (cb_t * decay).astype(BF16))
                y = y + _dot(state.astype(BF16), cc_t) * e_cum_t[col:col + 1, :]
                xw = (xh * e_end_t[col:col + 1, :]).astype(BF16)
                hst[h] = state * e_tot[:, col:col + 1] + _dot(xw, bc)
                if d == 0:
                    yst[c, hs, :] = y
                else:
                    yst[c, hs, :] = yst[c, hs, :] + y
            if d == 1:
                g_t = yst[c] + xst[c] * dsk_ref[...]
                g = jnp.concatenate([g_t[j * ch:(j + 1) * ch].T for j in range(gw // ch)], axis=1)
                g = g * _silu(z_ref[sl, :])
                o_ref[sl, :] = (_rms(g) * nw_ref[...]).astype(BF16)
            return carry

        lax.fori_loop(0, n_chunks, body, 0, unroll=2)


def _ssd(u, us, conv_w, conv_b, dtb, alog, dsk, nw, n_lat, n_ctx):
    b, lt, _ = u.shape
    gw = SSD_HEAD_DIM * 8
    ns = SSD_STATE
    nc = lt // SSD_CHUNK
    x_blk, b_blk, c_blk = 1024 // gw, 2048 // ns, 2304 // ns
    cx_blk, cb_blk, cc_blk = 0, 1024 // ns, 1280 // ns
    seq = lambda w, off: pl.BlockSpec((None, lt, w), lambda b, g, off=off: (b, 0, off + g))
    par = lambda r, w, off: pl.BlockSpec((r, w), lambda b, g, off=off: (0, off + g))
    grp = lambda w: pl.BlockSpec((None, 1, w), lambda b, g: (g, 0, 0))
    return pl.pallas_call(
        functools.partial(_ssd_kernel, n_lat=n_lat, n_ctx=n_ctx),
        grid=(b, SSD_GROUPS),
        in_specs=[
            seq(gw, x_blk), seq(ns, b_blk), seq(ns, c_blk), seq(gw, 0),
            pl.BlockSpec((None, lt, 128), lambda b, g: (b, 0, g)),
            par(3, gw, cx_blk), par(3, ns, cb_blk), par(3, ns, cc_blk),
            par(1, gw, cx_blk), par(1, ns, cb_blk), par(1, ns, cc_blk),
            grp(128), grp(128), pl.BlockSpec((None, gw, 128), lambda b, g: (g, 0, 0)), grp(gw),
            pl.BlockSpec((2, SSD_CHUNK, 3 * SSD_CHUNK), lambda b, g: (0, 0, 0)),
        ],
        out_specs=pl.BlockSpec((None, lt, gw), lambda b, g: (b, 0, g)),
        out_shape=jax.ShapeDtypeStruct((b, lt, SSD_GROUPS * gw), BF16),
        scratch_shapes=[
            pltpu.VMEM((nc, gw, SSD_CHUNK), F32), pltpu.VMEM((lt, ns), BF16), pltpu.VMEM((nc, ns, SSD_CHUNK), BF16),
            pltpu.VMEM((nc, 128, SSD_CHUNK), F32), pltpu.VMEM((lt, 128), F32), pltpu.VMEM((nc, gw, SSD_CHUNK), F32),
            pltpu.VMEM((gw // SSD_HEAD_DIM, SSD_HEAD_DIM, ns), F32),
        ],
        compiler_params=_cparams("parallel", "parallel"),
        name="ssd",
    )(u, u, u, u, us, conv_w, conv_w, conv_w, conv_b, conv_b, conv_b, dtb, alog, dsk, nw, _tri_consts(SSD_CHUNK))


_HGRN_LEVELS = (32, 16, 8, 4, 2, 1)


def _hgrn_consts():
    c = HGRN_CHUNK
    sums = np.zeros((7, c, c), np.float32)
    pairs = np.zeros((7, c, c), np.float32)
    for li, m in enumerate(_HGRN_LEVELS):
        for t in range(c):
            beta = (t // (2 * m)) * 2 * m
            mid = beta + m
            if t >= mid:
                sums[li, t, mid:t + 1] = 1.0
                pairs[li, t, beta:mid] = 1.0
            else:
                sums[li, t, t + 1:mid] = 1.0
    sums[6] = np.tril(np.ones((c, c), np.float32))
    pairs[6] = np.eye(c, dtype=np.float32)
    sums = np.stack([sums, sums[:, ::-1, ::-1]]).reshape(2, 7 * c, c)
    sums = np.concatenate([sums, sums, sums, np.zeros_like(sums)], axis=2)
    pairs = np.stack([pairs, pairs[:, ::-1, ::-1]])
    return jnp.asarray(sums, BF16), jnp.asarray(pairs, F32)


def _hgrn_kernel(q_ref, ff_ref, fb_ref, i_ref, g_ref, lb_ref, nw_ref, sums_ref, pairs_ref, o_ref,
                 ys, w_s, qb_s, kb_s, qd_s, kd_s, et_s, att_s, p_s, *, n_lat, n_ctx):
    ch = HGRN_CHUNK
    n_chunks = (n_lat + n_ctx) // ch
    n_lat_chunks = n_lat // ch
    n_lv = len(_HGRN_LEVELS)
    lb = lb_ref[...]
    log_lb = jnp.log(lb)
    log_1mlb = jnp.log1p(-lb)
    one_m_lb = 1.0 - lb
    dk = q_ref.shape[1]
    dv = i_ref.shape[1]
    t_idx = lax.broadcasted_iota(jnp.int32, (ch, dk), 0)

    laters = [[((t_idx & m) != 0) == (d == 0) for m in _HGRN_LEVELS] for d in range(2)]

    def operands(d, pi):
        f_ref = ff_ref if d == 0 else fb_ref
        r0 = pl.multiple_of(pi * 2 * ch, 2 * ch)
        sl = pl.ds(r0, 2 * ch)
        zf = f_ref[sl, :]
        ls = _log_sigmoid(zf)
        gate = log_1mlb + ls
        logf = jnp.maximum(log_lb, gate) + jnp.log(1.0 + jnp.exp(-jnp.abs(log_lb - gate)))
        kin = one_m_lb * jnp.exp(ls - zf)
        q = _silu(q_ref[sl, :])
        if d == 0:
            qb_s[sl, :] = q.astype(BF16)
        kb_s[d, sl, :] = kin.astype(BF16)
        logf2 = logf * LOG2_E
        lf = jnp.concatenate([logf2[:ch], logf2[ch:]], axis=1)
        hi = lf.astype(BF16)
        r1 = lf - hi.astype(F32)
        mid = r1.astype(BF16)
        lo = (r1 - mid.astype(F32)).astype(BF16)
        rel2 = _dot(sums_ref[d], jnp.concatenate([hi, mid, lo, jnp.zeros_like(hi)], axis=0))
        for half in range(2):
            rows = slice(half * ch, (half + 1) * ch)
            rel = rel2[:, half * dk:(half + 1) * dk]
            qh, kh = q[rows], kin[rows]
            c = pi * 2 + half
            for li in range(n_lv):
                e = jnp.exp2(rel[li * ch:(li + 1) * ch])
                w_s[d, c, li * ch:(li + 1) * ch, :] = (jnp.where(laters[d][li], qh, kh) * e).astype(BF16)
            bcum = rel[n_lv * ch:(n_lv + 1) * ch]
            tot = bcum[ch - 1:ch] if d == 0 else bcum[0:1]
            hs = pl.ds(r0 + half * ch, ch)
            qd_s[d, hs, :] = (qh * jnp.exp2(bcum)).astype(BF16)
            kd_s[d, hs, :] = (kh * jnp.exp2(tot - bcum)).astype(BF16)
            et_s[d, c] = jnp.broadcast_to(jnp.exp2(tot), (8, dk))

    def intra(d, c):
        sl = pl.ds(pl.multiple_of(c * ch, ch), ch)
        att = pairs_ref[d, n_lv] * _dot_nt(qb_s[sl, :], kb_s[d, sl, :])
        for li in range(n_lv):
            w = w_s[d, c, li * ch:(li + 1) * ch, :]
            att = att + pairs_ref[d, li] * _dot_nt(w, w)
        att_s[d, sl, :] = att.astype(BF16)
        p_s[d, c] = _dot_tn(i_ref[sl, :].astype(BF16), kd_s[d, sl, :])

    def scan(d, j, state_t):
        c = _chunk_index(j, d, n_lat_chunks, n_chunks)
        sl = pl.ds(pl.multiple_of(c * ch, ch), ch)
        o = _dot(att_s[d, sl, :], i_ref[sl, :].astype(BF16))
        o = o + _dot_nt(qd_s[d, sl, :], state_t.astype(BF16))
        if d == 0:
            ys[sl, :] = o
        else:
            o = ys[sl, :] + o
            o_ref[sl, :] = (_rms(o) * nw_ref[...] * _silu(g_ref[sl, :])).astype(BF16)
        return state_t * et_s[d, c][0:1] + p_s[d, c]

    n_pairs = n_chunks // 2
    zero_state = jnp.zeros((dv, dk), F32)

    def stage_a(pi, carry):
        operands(0, pi)
        return carry

    def stage_b(pi, carry):
        intra(0, 2 * pi)
        intra(0, 2 * pi + 1)
        operands(1, pi)
        return carry

    def stage_c(c, carry):
        intra(1, c)
        return carry

    lax.fori_loop(0, n_pairs, stage_a, 0, unroll=2)
    lax.fori_loop(0, n_pairs, stage_b, 0, unroll=2)
    lax.fori_loop(0, n_chunks, stage_c, 0, unroll=6)
    lax.fori_loop(0, n_chunks, functools.partial(scan, 0), zero_state, unroll=6)
    lax.fori_loop(0, n_chunks, functools.partial(scan, 1), zero_state, unroll=6)


def _hgrn(u, lb, nw, n_lat, n_ctx):
    b, lt, _ = u.shape
    w = 128
    ch = HGRN_CHUNK
    n_chunks = lt // ch
    sums, pairs = _hgrn_consts()
    seq = lambda off: pl.BlockSpec((None, lt, w), lambda b, h, off=off: (b, 0, off + h))
    head = pl.BlockSpec((None, 1, w), lambda b, h: (h, 0, 0))
    return pl.pallas_call(
        functools.partial(_hgrn_kernel, n_lat=n_lat, n_ctx=n_ctx),
        grid=(b, HGRN_HEADS),
        in_specs=[seq(2560 // w), seq(3584 // w), seq(4608 // w), seq(5632 // w), seq(6656 // w), head, head,
                  pl.BlockSpec(sums.shape, lambda b, h: (0, 0, 0)),
                  pl.BlockSpec(pairs.shape, lambda b, h: (0, 0, 0, 0))],
        out_specs=pl.BlockSpec((None, lt, w), lambda b, h: (b, 0, h)),
        out_shape=jax.ShapeDtypeStruct((b, lt, HGRN_HEADS * w), BF16),
        scratch_shapes=[
            pltpu.VMEM((lt, w), F32),
            pltpu.VMEM((2, n_chunks, len(_HGRN_LEVELS) * ch, w), BF16),
            pltpu.VMEM((lt, w), BF16), pltpu.VMEM((2, lt, w), BF16), pltpu.VMEM((2, lt, w), BF16),
            pltpu.VMEM((2, lt, w), BF16),
            pltpu.VMEM((2, n_chunks, 8, w), F32),
            pltpu.VMEM((2, lt, ch), BF16),
            pltpu.VMEM((2, n_chunks, w, w), F32),
        ],
        compiler_params=_cparams("parallel", "parallel"),
        name="hgrn2",
    )(u, u, u, u, u, lb, nw, sums, pairs)


def _mlstm_kernel(q_ref, k_ref, v_ref, og_ref, gt_ref, cwq_ref, cwk_ref, cbq_ref, cbk_ref, gb_ref, nw_ref, tri_ref,
                  o_ref, qst, ks, vst, gst, yst, cst, *, n_lat, n_ctx):
    ch = MLSTM_CHUNK
    n_chunks = (n_lat + n_ctx) // ch
    n_lat_chunks = n_lat // ch
    dqk = q_ref.shape[1]
    dv = v_ref.shape[1]
    k_scale = dqk ** -0.5

    def conv_piece(r0, rows, period):
        sl = pl.ds(r0, rows)
        c0 = r0 // ch
        q = _silu(_conv3(q_ref[sl, :], cwq_ref[...], cbq_ref[...], period))
        ks[sl, :] = (_silu(_conv3(k_ref[sl, :], cwk_ref[...], cbk_ref[...], period)) * k_scale).astype(BF16)
        raw = gt_ref[sl, :] + gb_ref[...]
        lane = lax.broadcasted_iota(jnp.int32, raw.shape, 1)
        g = jnp.where(lane < 2, raw, _log_sigmoid(raw))
        v = v_ref[sl, :]
        for k in range(rows // ch):
            rk = slice(k * ch, (k + 1) * ch)
            for j in range(dqk // ch):
                qst[c0 + k, j * ch:(j + 1) * ch, :] = q[rk, j * ch:(j + 1) * ch].T.astype(BF16)
            for j in range(dv // ch):
                vst[c0 + k, j * ch:(j + 1) * ch, :] = v[rk, j * ch:(j + 1) * ch].T.astype(BF16)
            vst[c0 + k, dv:, :] = jnp.ones((MLSTM_ONES_ROWS, ch), BF16)
            gst[c0 + k] = g[rk].T

    _conv_rows(n_lat, n_ctx, conv_piece)

    si = lax.broadcasted_iota(jnp.int32, (ch, ch), 0)
    ti = lax.broadcasted_iota(jnp.int32, (ch, ch), 1)

    for d in range(2):
        mask = (si <= ti) if d == 0 else (si >= ti)
        cst[...] = jnp.zeros_like(cst)

        def body(j, m_prev, d=d, mask=mask):
            c = _chunk_index(j, d, n_lat_chunks, n_chunks)
            sl = pl.ds(pl.multiple_of(c * ch, ch), ch)
            g_t = gst[c]
            cum_t = _dot01_rows(g_t, tri_ref[d])
            brow = cum_t[2 + d:3 + d, :]
            irow = g_t[d:d + 1, :]
            tot = brow[:, ch - 1:ch] if d == 0 else brow[:, 0:1]
            logd = jnp.where(mask, brow + jnp.broadcast_to(irow - brow, (ch, ch)).T, -jnp.inf)
            gstate = brow + m_prev
            mt = jnp.maximum(jnp.max(logd, axis=0, keepdims=True), gstate)
            q_t = qst[c]
            kc = ks[sl, :]
            w = _dot(kc, q_t) * jnp.exp(logd - mt)
            sw = jnp.exp(gstate - mt)
            v_t = vst[c]
            state = cst[...]
            qstate = _dot(state.astype(BF16), q_t)
            num = _dot(v_t[:dv], w.astype(BF16)) + sw * qstate[:dv]
            den = jnp.sum(w, axis=0, keepdims=True) + sw * qstate[dv:dv + 1]
            hout = num * (1.0 / jnp.maximum(jnp.abs(den), jnp.exp(-mt)))
            logw = tot - brow + irow
            m_new = jnp.maximum(tot + m_prev, jnp.max(logw, axis=1, keepdims=True))
            ws = jnp.exp(logw - m_new).astype(BF16)
            cst[...] = jnp.exp(tot + m_prev - m_new) * state + _dot(v_t * ws, kc)
            if d == 0:
                yst[c] = hout
            else:
                hh_t = yst[c] + hout
                hh = jnp.concatenate([hh_t[i * ch:(i + 1) * ch].T for i in range(dv // ch)], axis=1)
                o_ref[sl, :] = (_rms(hh) * nw_ref[...] * jax.nn.sigmoid(og_ref[sl, :])).astype(BF16)
            return m_new

        lax.fori_loop(0, n_chunks, body, jnp.zeros((1, 1), F32), unroll=2)


def _mlstm(u, us, conv_w, conv_b, gate_b, nw, n_lat, n_ctx):
    b, lt, _ = u.shape
    dqk, dv = 256, 512
    nh = MLSTM_HEADS
    ch = MLSTM_CHUNK
    nc = lt // ch
    upper = np.triu(np.ones((ch, ch), np.float32))
    tri = np.stack([upper, upper.T])
    tri = jnp.asarray(np.concatenate([tri, tri, tri], axis=1), BF16)
    seq = lambda w, off: pl.BlockSpec((None, lt, w), lambda b, h, off=off: (b, 0, off + h))
    par = lambda r, off: pl.BlockSpec((r, dqk), lambda b, h, off=off: (0, off + h))
    return pl.pallas_call(
        functools.partial(_mlstm_kernel, n_lat=n_lat, n_ctx=n_ctx),
        grid=(b, nh),
        in_specs=[
            seq(dqk, 0), seq(dqk, nh), seq(dv, 2048 // dv), seq(dv, 4096 // dv), seq(128, 0),
            par(3, 0), par(3, nh), par(1, 0), par(1, nh),
            pl.BlockSpec((None, 1, 128), lambda b, h: (h, 0, 0)),
            pl.BlockSpec((None, 1, dv), lambda b, h: (h, 0, 0)),
            pl.BlockSpec(tri.shape, lambda b, h: (0, 0, 0)),
        ],
        out_specs=pl.BlockSpec((None, lt, dv), lambda b, h: (b, 0, h)),
        out_shape=jax.ShapeDtypeStruct((b, lt, nh * dv), BF16),
        scratch_shapes=[
            pltpu.VMEM((nc, dqk, ch), BF16), pltpu.VMEM((lt, dqk), BF16),
            pltpu.VMEM((nc, dv + MLSTM_ONES_ROWS, ch), BF16), pltpu.VMEM((nc, 128, ch), F32),
            pltpu.VMEM((nc, dv, ch), F32), pltpu.VMEM((dv + MLSTM_ONES_ROWS, dqk), F32),
        ],
        compiler_params=_cparams("parallel", "parallel"),
        name="mlstm",
    )(u, u, u, u, us, conv_w, conv_w, conv_b, conv_b, gate_b, nw, tri)


def _pad_lanes(a, width=128):
    return jnp.pad(a, [(0, 0)] * (a.ndim - 1) + [(0, width - a.shape[-1])])


def _even_params(w_in, dt_bias, a_log, d_skip):
    heads = a_log.shape[1]
    hg = heads // SSD_GROUPS
    w_main = jnp.concatenate([w_in[:, :2560], w_in[:, 2560 + 2 * heads:]], axis=1).astype(BF16)
    dt_w = w_in[:, 2560:2560 + 2 * heads]
    per_group = lambda a: [_pad_lanes(jnp.concatenate([a[..., g * hg:(g + 1) * hg], a[..., heads + g * hg:heads + (g + 1) * hg]], axis=-1))
                           for g in range(SSD_GROUPS)]
    w_small = jnp.concatenate(per_group(dt_w), axis=1).astype(BF16)
    flat = lambda a: a.reshape(1, 2 * heads)
    dtb = jnp.stack(per_group(flat(dt_bias)))
    alog = jnp.stack(per_group(flat(a_log)))
    dsk = jnp.broadcast_to(jnp.repeat(d_skip, SSD_HEAD_DIM).reshape(SSD_GROUPS, hg * SSD_HEAD_DIM, 1),
                           (SSD_GROUPS, hg * SSD_HEAD_DIM, 128))
    return w_main, w_small, dtb, alog, dsk


def _odd_params(w_in, gate_b):
    nh = MLSTM_HEADS
    w_main = w_in[:, :6144].astype(BF16)
    gw = w_in[:, 6144:]
    w_small = jnp.concatenate([_pad_lanes(gw[:, h::nh]) for h in range(nh)], axis=1).astype(BF16)
    gb = jnp.stack([_pad_lanes(gate_b[:, h].reshape(1, 4)) for h in range(nh)])
    return w_main, w_small, gb


def kernel(x, c, ctx, c_ctx, mod_w, mod_b, norm_w, final_norm_w, mlp_w1, mlp_w2, even_w_in, even_w_out, ssd_conv_w, ssd_conv_b, ssd_a_log, ssd_dt_bias, ssd_d, ssd_norm_w, hgrn_lb, hgrn_norm_w, odd_w_in, odd_w_out, mlstm_conv_w, mlstm_conv_b, mlstm_gate_b, mlstm_norm_w):
    bsz, n_lat, d = x.shape
    n_ctx = ctx.shape[1]
    depth = mod_w.shape[0]
    lt = n_lat + n_ctx
    assert bsz < C_ROWS and n_lat % CONV_PIECE == 0 and n_ctx % SSD_CHUNK == 0 and n_ctx & (n_ctx - 1) == 0
    ctx_row = bsz

    c_all = jnp.zeros((C_ROWS, d), F32).at[:bsz].set(c).at[ctx_row].set(c_ctx)
    mods = _modulation(c_all, mod_w, mod_b).reshape(depth, C_ROWS, N_MOD, 1, d)

    lb_all = jnp.cumsum(jax.nn.softmax(hgrn_lb.astype(F32), axis=0), axis=0)
    lb_all = lb_all - lb_all[0]

    tm_in = lt // 2
    tm_full = lt // 4
    tm_lat = min(512, n_lat)

    xx = jnp.concatenate([x, ctx], axis=1)
    for layer in range(depth):
        last = layer == depth - 1
        nw1 = norm_w[layer, 0].reshape(1, d)
        nw2 = norm_w[layer, 1].reshape(1, d)
        if layer % 2 == 0:
            e = layer // 2
            w_main, w_small, dtb, alog, dsk = _even_params(even_w_in[e], ssd_dt_bias[e], ssd_a_log[e], ssd_d[e])
            u, us = _inproj(xx, nw1, mods, layer, ctx_row, w_main, w_small, n_lat, tm_in, 768)
            ya = _ssd(u, us, ssd_conv_w[e], ssd_conv_b[e].reshape(1, -1), dtb, alog, dsk,
                      ssd_norm_w[e].reshape(SSD_GROUPS, 1, -1), n_lat, n_ctx)
            yb = _hgrn(u, lb_all[e].reshape(HGRN_HEADS, 1, -1), hgrn_norm_w[e].reshape(HGRN_HEADS, 1, -1), n_lat, n_ctx)
            ys, w_out = [ya, yb], even_w_out[e].astype(BF16)
        else:
            o = layer // 2
            w_main, w_small, gb = _odd_params(odd_w_in[o], mlstm_gate_b[o])
            u, us = _inproj(xx, nw1, mods, layer, ctx_row, w_main, w_small, n_lat, tm_in, 768)
            yc = _mlstm(u, us, mlstm_conv_w[o], mlstm_conv_b[o].reshape(1, -1), gb,
                        mlstm_norm_w[o].reshape(MLSTM_HEADS, 1, -1), n_lat, n_ctx)
            ys, w_out = [yc], odd_w_out[o].astype(BF16)
        rows, tm = (n_lat, tm_lat) if last else (lt, tm_full)
        xx = _outproj(ys, w_out, xx, mods, layer, ctx_row, n_lat, rows, tm // 2 if not last else tm)
        xx = _mlp(xx, nw2, mods, layer, ctx_row, mlp_w1[layer].astype(BF16), mlp_w2[layer].astype(BF16),
                  final_norm_w.reshape(1, d), n_lat, rows, tm, 1024, last)
    return xx
```

```python
import functools

import numpy as np
import jax
import jax.numpy as jnp
from jax import lax
from jax.experimental import pallas as pl
from jax.experimental.pallas import tpu as pltpu

F32 = jnp.float32
BF16 = jnp.bfloat16

EPS = 1e-6
GRID_W = 64
N_MOD = 6
HGRN_LOG2_FLOOR = -1e5
C_ROWS = 32

V7X_VMEM_BYTES = 64 * 1024 * 1024
VMEM_LIMIT = V7X_VMEM_BYTES - 8 * 1024 * 1024

SSD_CHUNK = 128
SSD_HEAD_DIM = 64
SSD_STATE = 128
SSD_GROUPS = 2
HGRN_CHUNK = 64
HGRN_HEADS = 8
MLSTM_CHUNK = 128
MLSTM_HEADS = 4
MLSTM_ONES_ROWS = 16
CONV_PIECE = 256
NORM_ROWS = 16


def _cparams(*sem):
    return pltpu.CompilerParams(dimension_semantics=sem, vmem_limit_bytes=VMEM_LIMIT)


def _silu(x):
    return x * jax.nn.sigmoid(x)


def _softplus(x):
    return jnp.maximum(x, 0.0) + jnp.log(1.0 + jnp.exp(-jnp.abs(x)))


def _log_sigmoid(x):
    return -_softplus(-x)


def _dot(a, b):
    return jnp.dot(a, b, preferred_element_type=F32)


def _dot_nt(a, b):
    return lax.dot_general(a, b, (((1,), (1,)), ((), ())), preferred_element_type=F32)


def _dot_tn(a, b):
    return lax.dot_general(a, b, (((0,), (0,)), ((), ())), preferred_element_type=F32)


def _dot01(m01x3, x):
    hi = x.astype(BF16)
    r = x - hi.astype(F32)
    mid = r.astype(BF16)
    lo = (r - mid.astype(F32)).astype(BF16)
    return _dot(m01x3, jnp.concatenate([hi, mid, lo], axis=0))


def _dot01_rows(x, m01x3):
    hi = x.astype(BF16)
    r = x - hi.astype(F32)
    mid = r.astype(BF16)
    lo = (r - mid.astype(F32)).astype(BF16)
    return _dot(jnp.concatenate([hi, mid, lo], axis=1), m01x3)


def _rms(x):
    return x * lax.rsqrt(jnp.mean(x * x, axis=-1, keepdims=True) + EPS)


def _conv3(u, w, b, period):
    rows = u.shape[0]
    t = lax.broadcasted_iota(jnp.int32, u.shape, 0) & (period - 1)
    left = jnp.where(t == 0, 0.0, pltpu.roll(u, 1, axis=0))
    right = jnp.where(t == period - 1, 0.0, pltpu.roll(u, rows - 1, axis=0))
    return left * w[0:1] + u * w[1:2] + right * w[2:3] + b


def _modulated_norm(x_ref, h_ref, nw, sh_l, sc_l, sh_c, sc_c, row0, n_lat):
    a_l, a_c = nw * (1.0 + sc_l), nw * (1.0 + sc_c)

    def block(r, carry):
        start = pl.multiple_of(r * NORM_ROWS, NORM_ROWS)
        rs = pl.ds(start, NORM_ROWS)
        is_ctx = row0 + start >= n_lat
        h_ref[rs, :] = (_rms(x_ref[rs, :]) * jnp.where(is_ctx, a_c, a_l) + jnp.where(is_ctx, sh_c, sh_l)).astype(BF16)
        return carry

    lax.fori_loop(0, x_ref.shape[0] // NORM_ROWS, block, 0, unroll=4)


def _mod_kernel(c_ref, w_ref, b_ref, o_ref):
    a = _silu(c_ref[...]).astype(BF16)
    o_ref[...] = _dot(a, w_ref[...].astype(BF16)) + b_ref[...]


def _modulation(c_all, mod_w, mod_b):
    depth, d, n = mod_w.shape
    tn = 1024
    return pl.pallas_call(
        _mod_kernel,
        grid=(depth, n // tn),
        in_specs=[
            pl.BlockSpec((C_ROWS, d), lambda l, j: (0, 0)),
            pl.BlockSpec((None, d, tn), lambda l, j: (l, 0, j)),
            pl.BlockSpec((None, 1, tn), lambda l, j: (l, 0, j)),
        ],
        out_specs=pl.BlockSpec((None, C_ROWS, tn), lambda l, j: (l, 0, j)),
        out_shape=jax.ShapeDtypeStruct((depth, C_ROWS, n), F32),
        compiler_params=_cparams("parallel", "parallel"),
        name="modulation",
    )(c_all, mod_w, mod_b.reshape(depth, 1, n))


def _mod_specs(layer, ctx_row, ks, d, nargs):
    specs = []
    for k in ks:
        if nargs == 3:
            specs.append(pl.BlockSpec((None, None, None, 1, d), lambda b, i, j, k=k: (layer, b, k, 0, 0)))
            specs.append(pl.BlockSpec((None, None, None, 1, d), lambda b, i, j, k=k: (layer, ctx_row, k, 0, 0)))
        else:
            specs.append(pl.BlockSpec((None, None, None, 1, d), lambda b, i, k=k: (layer, b, k, 0, 0)))
            specs.append(pl.BlockSpec((None, None, None, 1, d), lambda b, i, k=k: (layer, ctx_row, k, 0, 0)))
    return specs


def _inproj_kernel(x_ref, nw_ref, shl_ref, shc_ref, scl_ref, scc_ref, w_ref, ws_ref, o_ref, os_ref, h_ref, *, n_lat):
    @pl.when(pl.program_id(2) == 0)
    def _():
        _modulated_norm(x_ref, h_ref, nw_ref[...], shl_ref[...], scl_ref[...], shc_ref[...], scc_ref[...],
                        pl.program_id(1) * x_ref.shape[0], n_lat)
        os_ref[...] = _dot(h_ref[...], ws_ref[...])

    o_ref[...] = _dot(h_ref[...], w_ref[...]).astype(o_ref.dtype)


def _inproj(x, nw, mods, layer, ctx_row, w_main, w_small, n_lat, tm, tn):
    b, lt, d = x.shape
    n = w_main.shape[1]
    ns = w_small.shape[1]
    return pl.pallas_call(
        functools.partial(_inproj_kernel, n_lat=n_lat),
        grid=(b, lt // tm, n // tn),
        in_specs=[
            pl.BlockSpec((None, tm, d), lambda b, i, j: (b, i, 0)),
            pl.BlockSpec((1, d), lambda b, i, j: (0, 0)),
            *_mod_specs(layer, ctx_row, (0, 1), d, 3),
            pl.BlockSpec((d, tn), lambda b, i, j: (0, j)),
            pl.BlockSpec((d, ns), lambda b, i, j: (0, 0)),
        ],
        out_specs=[
            pl.BlockSpec((None, tm, tn), lambda b, i, j: (b, i, j)),
            pl.BlockSpec((None, tm, ns), lambda b, i, j: (b, i, 0)),
        ],
        out_shape=[jax.ShapeDtypeStruct((b, lt, n), BF16), jax.ShapeDtypeStruct((b, lt, ns), F32)],
        scratch_shapes=[pltpu.VMEM((tm, d), BF16)],
        compiler_params=_cparams("parallel", "parallel", "arbitrary"),
        name="inproj",
    )(x, nw, mods, mods, mods, mods, w_main, w_small)


def _outproj_kernel(*refs, n_y, n_lat):
    y_refs, w_refs = refs[:n_y], refs[n_y:2 * n_y]
    x_ref, gl_ref, gc_ref, o_ref = refs[2 * n_y:]
    acc = _dot(y_refs[0][...], w_refs[0][...])
    for y_ref, w_ref in zip(y_refs[1:], w_refs[1:]):
        acc = acc + _dot(y_ref[...], w_ref[...])
    tm = x_ref.shape[0]
    rows = pl.program_id(1) * tm + lax.broadcasted_iota(jnp.int32, (tm, 1), 0)
    g = jnp.where(rows >= n_lat, gc_ref[...], gl_ref[...])
    o_ref[...] = x_ref[...] + g * acc


def _outproj(ys, w, x, mods, layer, ctx_row, n_lat, rows, tm):
    b, lt, d = x.shape
    n_y = len(ys)
    y_specs = [pl.BlockSpec((None, tm, y.shape[2]), lambda b, i: (b, i, 0)) for y in ys]
    w_specs = [pl.BlockSpec((y.shape[2], d), lambda b, i, k=k: (k, 0)) for k, y in enumerate(ys)]
    return pl.pallas_call(
        functools.partial(_outproj_kernel, n_y=n_y, n_lat=n_lat),
        grid=(b, rows // tm),
        in_specs=[*y_specs, *w_specs,
                  pl.BlockSpec((None, tm, d), lambda b, i: (b, i, 0)),
                  *_mod_specs(layer, ctx_row, (2,), d, 2)],
        out_specs=pl.BlockSpec((None, tm, d), lambda b, i: (b, i, 0)),
        out_shape=jax.ShapeDtypeStruct((b, rows, d), F32),
        compiler_params=_cparams("parallel", "parallel"),
        name="outproj",
    )(*ys, *([w] * n_y), x, mods, mods)


def _mlp_kernel(x_ref, nw_ref, shl_ref, shc_ref, scl_ref, scc_ref, gl_ref, gc_ref, w1_ref, w2_ref, fw_ref,
                o_ref, h_ref, acc_ref, *, n_lat, final_norm):
    f = pl.program_id(2)
    tm = x_ref.shape[0]

    @pl.when(f == 0)
    def _():
        _modulated_norm(x_ref, h_ref, nw_ref[...], shl_ref[...], scl_ref[...], shc_ref[...], scc_ref[...],
                        pl.program_id(1) * tm, n_lat)
        acc_ref[...] = jnp.zeros_like(acc_ref)

    a = jnp.maximum(_dot(h_ref[...], w1_ref[...]), 0.0)
    acc_ref[...] += _dot((a * a).astype(BF16), w2_ref[...])

    @pl.when(f == pl.num_programs(2) - 1)
    def _():
        rows = pl.program_id(1) * tm + lax.broadcasted_iota(jnp.int32, (tm, 1), 0)
        g = jnp.where(rows >= n_lat, gc_ref[...], gl_ref[...])
        y = x_ref[...] + g * acc_ref[...]
        if final_norm:
            y = _rms(y) * fw_ref[...]
        o_ref[...] = y


def _mlp(x, nw, mods, layer, ctx_row, w1, w2, fw, n_lat, rows, tm, tf, final_norm):
    b, _, d = x.shape
    dff = w1.shape[1]
    return pl.pallas_call(
        functools.partial(_mlp_kernel, n_lat=n_lat, final_norm=final_norm),
        grid=(b, rows // tm, dff // tf),
        in_specs=[
            pl.BlockSpec((None, tm, d), lambda b, i, j: (b, i, 0)),
            pl.BlockSpec((1, d), lambda b, i, j: (0, 0)),
            *_mod_specs(layer, ctx_row, (3, 4, 5), d, 3),
            pl.BlockSpec((d, tf), lambda b, i, j: (0, j)),
            pl.BlockSpec((tf, d), lambda b, i, j: (j, 0)),
            pl.BlockSpec((1, d), lambda b, i, j: (0, 0)),
        ],
        out_specs=pl.BlockSpec((None, tm, d), lambda b, i, j: (b, i, 0)),
        out_shape=jax.ShapeDtypeStruct((b, rows, d), F32),
        scratch_shapes=[pltpu.VMEM((tm, d), BF16), pltpu.VMEM((tm, d), F32)],
        compiler_params=_cparams("parallel", "parallel", "arbitrary"),
        name="mlp",
    )(x, nw, mods, mods, mods, mods, mods, mods, w1, w2, fw)


def _tri_consts(c):
    lower = np.tril(np.ones((c, c), np.float32))
    tri = np.stack([lower, lower.T])
    return jnp.asarray(np.concatenate([tri, tri, tri], axis=2), BF16)


def _chunk_index(j, d, n_lat_chunks, n_chunks):
    if d == 0:
        c = j + n_lat_chunks
        return jnp.where(c >= n_chunks, c - n_chunks, c)
    return n_chunks - 1 - j


def _conv_rows(n_lat, n_ctx, fn):
    def body(p, carry):
        fn(pl.multiple_of(p * CONV_PIECE, CONV_PIECE), CONV_PIECE, GRID_W)
        return carry
    lax.fori_loop(0, n_lat // CONV_PIECE, body, 0)
    fn(n_lat, n_ctx, n_ctx)


def _ssd_kernel(x_ref, b_ref, c_ref, z_ref, dt_ref, cwx_ref, cwb_ref, cwc_ref, cbx_ref, cbb_ref, cbc_ref,
                dtb_ref, alog_ref, dsk_ref, nw_ref, tri_ref, o_ref,
                xst, bs, cst, dtt, acs, yst, hst, *, n_lat, n_ctx):
    ch = SSD_CHUNK
    p = SSD_HEAD_DIM
    gw = x_ref.shape[1]
    heads = gw // p
    n_chunks = (n_lat + n_ctx) // ch
    n_lat_chunks = n_lat // ch

    def conv_piece(r0, rows, period):
        sl = pl.ds(r0, rows)
        c0 = r0 // ch
        x = _silu(_conv3(x_ref[sl, :].astype(F32), cwx_ref[...], cbx_ref[...], period))
        c = _silu(_conv3(c_ref[sl, :].astype(F32), cwc_ref[...], cbc_ref[...], period))
        bs[sl, :] = _silu(_conv3(b_ref[sl, :].astype(F32), cwb_ref[...], cbb_ref[...], period)).astype(BF16)
        dt = _softplus(dt_ref[sl, :] + dtb_ref[...])
        acs[sl, :] = -jnp.exp(alog_ref[...]) * dt
        for k in range(rows // ch):
            rk = slice(k * ch, (k + 1) * ch)
            for j in range(gw // ch):
                xst[c0 + k, j * ch:(j + 1) * ch, :] = x[rk, j * ch:(j + 1) * ch].T
            cst[c0 + k] = c[rk].T.astype(BF16)
            dtt[c0 + k] = dt[rk].T

    _conv_rows(n_lat, n_ctx, conv_piece)

    si = lax.broadcasted_iota(jnp.int32, (ch, ch), 0)
    ti = lax.broadcasted_iota(jnp.int32, (ch, ch), 1)

    for d in range(2):
        mask = (si <= ti) if d == 0 else (si >= ti)
        hst[...] = jnp.zeros_like(hst)

        def body(j, carry, d=d, mask=mask):
            c = _chunk_index(j, d, n_lat_chunks, n_chunks)
            sl = pl.ds(pl.multiple_of(c * ch, ch), ch)
            cum = _dot01(tri_ref[d], acs[sl, :])
            cum_t = cum.T
            tot_c = cum_t[:, ch - 1:ch] if d == 0 else cum_t[:, 0:1]
            e_tot = jnp.exp(cum[ch - 1:ch, :] if d == 0 else cum[0:1, :])
            dt_t = dtt[c]
            e_cum_t = jnp.exp(cum_t)
            e_end_t = jnp.exp(tot_c - cum_t) * dt_t
            bc = bs[sl, :]
            cc_t = cst[c]
            cb_t = _dot(bc, cc_t)
            for h in range(heads):
                col = d * heads + h
                hs = slice(h * p, (h + 1) * p)
                diff = cum_t[col:col + 1, :] - cum[:, col:col + 1]
                decay = jnp.exp(jnp.where(mask, diff, -jnp.inf))
                xh = xst[c, hs, :]
                state = hst[h]
                y = _dot((xh * dt_t[col:col + 1, :]).astype(BF16), (cb_t * decay).astype(BF16))
                y = y + _dot(state.astype(BF16), cc_t) * e_cum_t[col:col + 1, :]
                xw = (xh * e_end_t[col:col + 1, :]).astype(BF16)
                hst[h] = state * e_tot[:, col:col + 1] + _dot(xw, bc)
                if d == 0:
                    yst[c, hs, :] = y
                else:
                    yst[c, hs, :] = yst[c, hs, :] + y
            if d == 1:
                g_t = yst[c] + xst[c] * dsk_ref[...]
                g = jnp.concatenate([g_t[j * ch:(j + 1) * ch].T for j in range(gw // ch)], axis=1)
                g = g * _silu(z_ref[sl, :].astype(F32))
                o_ref[sl, :] = (_rms(g) * nw_ref[...]).astype(BF16)
            return carry

        lax.fori_loop(0, n_chunks, body, 0, unroll=2)


def _ssd(u, us, conv_w, conv_b, dtb, alog, dsk, nw, n_lat, n_ctx):
    b, lt, _ = u.shape
    gw = SSD_HEAD_DIM * 8
    ns = SSD_STATE
    nc = lt // SSD_CHUNK
    x_blk, b_blk, c_blk = 1024 // gw, 2048 // ns, 2304 // ns
    cx_blk, cb_blk, cc_blk = 0, 1024 // ns, 1280 // ns
    seq = lambda w, off: pl.BlockSpec((None, lt, w), lambda b, g, off=off: (b, 0, off + g))
    par = lambda r, w, off: pl.BlockSpec((r, w), lambda b, g, off=off: (0, off + g))
    grp = lambda w: pl.BlockSpec((None, 1, w), lambda b, g: (g, 0, 0))
    return pl.pallas_call(
        functools.partial(_ssd_kernel, n_lat=n_lat, n_ctx=n_ctx),
        grid=(b, SSD_GROUPS),
        in_specs=[
            seq(gw, x_blk), seq(ns, b_blk), seq(ns, c_blk), seq(gw, 0),
            pl.BlockSpec((None, lt, 128), lambda b, g: (b, 0, g)),
            par(3, gw, cx_blk), par(3, ns, cb_blk), par(3, ns, cc_blk),
            par(1, gw, cx_blk), par(1, ns, cb_blk), par(1, ns, cc_blk),
            grp(128), grp(128), pl.BlockSpec((None, gw, 128), lambda b, g: (g, 0, 0)), grp(gw),
            pl.BlockSpec((2, SSD_CHUNK, 3 * SSD_CHUNK), lambda b, g: (0, 0, 0)),
        ],
        out_specs=pl.BlockSpec((None, lt, gw), lambda b, g: (b, 0, g)),
        out_shape=jax.ShapeDtypeStruct((b, lt, SSD_GROUPS * gw), BF16),
        scratch_shapes=[
            pltpu.VMEM((nc, gw, SSD_CHUNK), F32), pltpu.VMEM((lt, ns), BF16), pltpu.VMEM((nc, ns, SSD_CHUNK), BF16),
            pltpu.VMEM((nc, 128, SSD_CHUNK), F32), pltpu.VMEM((lt, 128), F32), pltpu.VMEM((nc, gw, SSD_CHUNK), F32),
            pltpu.VMEM((gw // SSD_HEAD_DIM, SSD_HEAD_DIM, ns), F32),
        ],
        compiler_params=_cparams("parallel", "parallel"),
        name="ssd",
    )(u, u, u, u, us, conv_w, conv_w, conv_w, conv_b, conv_b, conv_b, dtb, alog, dsk, nw, _tri_consts(SSD_CHUNK))


_HGRN_LEVELS = (32, 16, 8, 4, 2, 1)


def _hgrn_consts():
    c = HGRN_CHUNK
    sums = np.zeros((7, c, c), np.float32)
    pairs = np.zeros((7, c, c), np.float32)
    for li, m in enumerate(_HGRN_LEVELS):
        for t in range(c):
            beta = (t // (2 * m)) * 2 * m
            mid = beta + m
            if t >= mid:
                sums[li, t, mid:t + 1] = 1.0
                pairs[li, t, beta:mid] = 1.0
            else:
                sums[li, t, t + 1:mid] = 1.0
    sums[6] = np.tril(np.ones((c, c), np.float32))
    pairs[6] = np.eye(c, dtype=np.float32)
    sums = np.stack([sums, sums[:, ::-1, ::-1]]).reshape(2, 7 * c, c)
    sums = np.concatenate([sums, sums, sums, np.zeros_like(sums)], axis=2)
    pairs = np.stack([pairs, pairs[:, ::-1, ::-1]])
    return jnp.asarray(sums, BF16), jnp.asarray(pairs, F32)


def _hgrn_kernel(q_ref, ff_ref, fb_ref, i_ref, g_ref, lb_ref, nw_ref, sums_ref, pairs_ref, o_ref,
                 ys, w_s, qb_s, kb_s, qd_s, kd_s, et_s, att_s, p_s, *, n_lat, n_ctx):
    ch = HGRN_CHUNK
    n_chunks = (n_lat + n_ctx) // ch
    n_lat_chunks = n_lat // ch
    n_lv = len(_HGRN_LEVELS)
    lb = lb_ref[...]
    one_m_lb = 1.0 - lb
    dk = q_ref.shape[1]
    dv = i_ref.shape[1]
    t_idx = lax.broadcasted_iota(jnp.int32, (ch, dk), 0)

    laters = [[((t_idx & m) != 0) == (d == 0) for m in _HGRN_LEVELS] for d in range(2)]

    def operands(d, pi):
        f_ref = ff_ref if d == 0 else fb_ref
        r0 = pl.multiple_of(pi * 2 * ch, 2 * ch)
        sl = pl.ds(r0, 2 * ch)
        gate = jax.nn.sigmoid(f_ref[sl, :].astype(F32))
        kin = one_m_lb * (1.0 - gate)
        logf2 = jnp.maximum(jnp.log2(lb + one_m_lb * gate), HGRN_LOG2_FLOOR)
        q = _silu(q_ref[sl, :].astype(F32))
        if d == 0:
            qb_s[sl, :] = q.astype(BF16)
        kb_s[d, sl, :] = kin.astype(BF16)
        lf = jnp.concatenate([logf2[:ch], logf2[ch:]], axis=1)
        hi = lf.astype(BF16)
        r1 = lf - hi.astype(F32)
        mid = r1.astype(BF16)
        lo = (r1 - mid.astype(F32)).astype(BF16)
        rel2 = _dot(sums_ref[d], jnp.concatenate([hi, mid, lo, jnp.zeros_like(hi)], axis=0))
        for half in range(2):
            rows = slice(half * ch, (half + 1) * ch)
            rel = rel2[:, half * dk:(half + 1) * dk]
            qh, kh = q[rows], kin[rows]
            c = pi * 2 + half
            for li in range(n_lv):
                e = jnp.exp2(rel[li * ch:(li + 1) * ch])
                w_s[d, c, li * ch:(li + 1) * ch, :] = (jnp.where(laters[d][li], qh, kh) * e).astype(BF16)
            bcum = rel[n_lv * ch:(n_lv + 1) * ch]
            tot = bcum[ch - 1:ch] if d == 0 else bcum[0:1]
            hs = pl.ds(r0 + half * ch, ch)
            qd_s[d, hs, :] = (qh * jnp.exp2(bcum)).astype(BF16)
            kd_s[d, hs, :] = (kh * jnp.exp2(tot - bcum)).astype(BF16)
            et_s[d, c] = jnp.broadcast_to(jnp.exp2(tot), (8, dk))

    def intra(d, c):
        sl = pl.ds(pl.multiple_of(c * ch, ch), ch)
        att = pairs_ref[d, n_lv] * _dot_nt(qb_s[sl, :], kb_s[d, sl, :])
        for li in range(n_lv):
            w = w_s[d, c, li * ch:(li + 1) * ch, :]
            att = att + pairs_ref[d, li] * _dot_nt(w, w)
        att_s[d, sl, :] = att.astype(BF16)
        p_s[d, c] = _dot_tn(i_ref[sl, :], kd_s[d, sl, :])

    def scan(d, j, state_t):
        c = _chunk_index(j, d, n_lat_chunks, n_chunks)
        sl = pl.ds(pl.multiple_of(c * ch, ch), ch)
        o = _dot(att_s[d, sl, :], i_ref[sl, :])
        o = o + _dot_nt(qd_s[d, sl, :], state_t.astype(BF16))
        if d == 0:
            ys[sl, :] = o
        else:
            o = ys[sl, :] + o
            o_ref[sl, :] = (_rms(o) * nw_ref[...] * _silu(g_ref[sl, :].astype(F32))).astype(BF16)
        return state_t * et_s[d, c][0:1] + p_s[d, c]

    n_pairs = n_chunks // 2
    zero_state = jnp.zeros((dv, dk), F32)

    def stage_a(pi, carry):
        operands(0, pi)
        return carry

    def stage_b(pi, carry):
        intra(0, 2 * pi)
        intra(0, 2 * pi + 1)
        operands(1, pi)
        return carry

    def stage_c(c, carry):
        intra(1, c)
        return carry

    lax.fori_loop(0, n_pairs, stage_a, 0, unroll=2)
    lax.fori_loop(0, n_pairs, stage_b, 0, unroll=2)
    lax.fori_loop(0, n_chunks, stage_c, 0, unroll=6)
    lax.fori_loop(0, n_chunks, functools.partial(scan, 0), zero_state, unroll=6)
    lax.fori_loop(0, n_chunks, functools.partial(scan, 1), zero_state, unroll=6)


def _hgrn(u, lb, nw, n_lat, n_ctx):
    b, lt, _ = u.shape
    w = 128
    ch = HGRN_CHUNK
    n_chunks = lt // ch
    sums, pairs = _hgrn_consts()
    seq = lambda off: pl.BlockSpec((None, lt, w), lambda b, h, off=off: (b, 0, off + h))
    head = pl.BlockSpec((None, 1, w), lambda b, h: (h, 0, 0))
    return pl.pallas_call(
        functools.partial(_hgrn_kernel, n_lat=n_lat, n_ctx=n_ctx),
        grid=(b, HGRN_HEADS),
        in_specs=[seq(2560 // w), seq(3584 // w), seq(4608 // w), seq(5632 // w), seq(6656 // w), head, head,
                  pl.BlockSpec(sums.shape, lambda b, h: (0, 0, 0)),
                  pl.BlockSpec(pairs.shape, lambda b, h: (0, 0, 0, 0))],
        out_specs=pl.BlockSpec((None, lt, w), lambda b, h: (b, 0, h)),
        out_shape=jax.ShapeDtypeStruct((b, lt, HGRN_HEADS * w), BF16),
        scratch_shapes=[
            pltpu.VMEM((lt, w), F32),
            pltpu.VMEM((2, n_chunks, len(_HGRN_LEVELS) * ch, w), BF16),
            pltpu.VMEM((lt, w), BF16), pltpu.VMEM((2, lt, w), BF16), pltpu.VMEM((2, lt, w), BF16),
            pltpu.VMEM((2, lt, w), BF16),
            pltpu.VMEM((2, n_chunks, 8, w), F32),
            pltpu.VMEM((2, lt, ch), BF16),
            pltpu.VMEM((2, n_chunks, w, w), F32),
        ],
        compiler_params=_cparams("parallel", "parallel"),
        name="hgrn2",
    )(u, u, u, u, u, lb, nw, sums, pairs)


def _mlstm_kernel(q_ref, k_ref, v_ref, og_ref, gt_ref, cwq_ref, cwk_ref, cbq_ref, cbk_ref, gb_ref, nw_ref, tri_ref,
                  o_ref, qst, ks, vst, gst, yst, cst, *, n_lat, n_ctx):
    ch = MLSTM_CHUNK
    n_chunks = (n_lat + n_ctx) // ch
    n_lat_chunks = n_lat // ch
    dqk = q_ref.shape[1]
    dv = v_ref.shape[1]
    k_scale = dqk ** -0.5

    def conv_piece(r0, rows, period):
        sl = pl.ds(r0, rows)
        c0 = r0 // ch
        q = _silu(_conv3(q_ref[sl, :].astype(F32), cwq_ref[...], cbq_ref[...], period))
        ks[sl, :] = (_silu(_conv3(k_ref[sl, :].astype(F32), cwk_ref[...], cbk_ref[...], period)) * k_scale).astype(BF16)
        raw = gt_ref[sl, :] + gb_ref[...]
        lane = lax.broadcasted_iota(jnp.int32, raw.shape, 1)
        g = jnp.where(lane < 2, raw, _log_sigmoid(raw))
        v = v_ref[sl, :].astype(F32)
        for k in range(rows // ch):
            rk = slice(k * ch, (k + 1) * ch)
            for j in range(dqk // ch):
                qst[c0 + k, j * ch:(j + 1) * ch, :] = q[rk, j * ch:(j + 1) * ch].T.astype(BF16)
            for j in range(dv // ch):
                vst[c0 + k, j * ch:(j + 1) * ch, :] = v[rk, j * ch:(j + 1) * ch].T.astype(BF16)
            vst[c0 + k, dv:, :] = jnp.ones((MLSTM_ONES_ROWS, ch), BF16)
            gst[c0 + k] = g[rk].T

    _conv_rows(n_lat, n_ctx, conv_piece)

    si = lax.broadcasted_iota(jnp.int32, (ch, ch), 0)
    ti = lax.broadcasted_iota(jnp.int32, (ch, ch), 1)

    for d in range(2):
        mask = (si <= ti) if d == 0 else (si >= ti)
        cst[...] = jnp.zeros_like(cst)

        def body(j, m_prev, d=d, mask=mask):
            c = _chunk_index(j, d, n_lat_chunks, n_chunks)
            sl = pl.ds(pl.multiple_of(c * ch, ch), ch)
            g_t = gst[c]
            cum_t = _dot01_rows(g_t, tri_ref[d])
            brow = cum_t[2 + d:3 + d, :]
            irow = g_t[d:d + 1, :]
            tot = brow[:, ch - 1:ch] if d == 0 else brow[:, 0:1]
            logd = jnp.where(mask, brow + jnp.broadcast_to(irow - brow, (ch, ch)).T, -jnp.inf)
            gstate = brow + m_prev
            mt = jnp.maximum(jnp.max(logd, axis=0, keepdims=True), gstate)
            q_t = qst[c]
            kc = ks[sl, :]
            w = _dot(kc, q_t) * jnp.exp(logd - mt)
            sw = jnp.exp(gstate - mt)
            v_t = vst[c]
            state = cst[...]
            qstate = _dot(state.astype(BF16), q_t)
            num = _dot(v_t[:dv], w.astype(BF16)) + sw * qstate[:dv]
            den = jnp.sum(w, axis=0, keepdims=True) + sw * qstate[dv:dv + 1]
            hout = num * (1.0 / jnp.maximum(jnp.abs(den), jnp.exp(-mt)))
            logw = tot - brow + irow
            m_new = jnp.maximum(tot + m_prev, jnp.max(logw, axis=1, keepdims=True))
            ws = jnp.exp(logw - m_new).astype(BF16)
            cst[...] = jnp.exp(tot + m_prev - m_new) * state + _dot(v_t * ws, kc)
            if d == 0:
                yst[c] = hout
            else:
                hh_t = yst[c] + hout
                hh = jnp.concatenate([hh_t[i * ch:(i + 1) * ch].T for i in range(dv // ch)], axis=1)
                o_ref[sl, :] = (_rms(hh) * nw_ref[...] * jax.nn.sigmoid(og_ref[sl, :].astype(F32))).astype(BF16)
            return m_new

        lax.fori_loop(0, n_chunks, body, jnp.zeros((1, 1), F32), unroll=2)


def _mlstm(u, us, conv_w, conv_b, gate_b, nw, n_lat, n_ctx):
    b, lt, _ = u.shape
    dqk, dv = 256, 512
    nh = MLSTM_HEADS
    ch = MLSTM_CHUNK
    nc = lt // ch
    upper = np.triu(np.ones((ch, ch), np.float32))
    tri = np.stack([upper, upper.T])
    tri = jnp.asarray(np.concatenate([tri, tri, tri], axis=1), BF16)
    seq = lambda w, off: pl.BlockSpec((None, lt, w), lambda b, h, off=off: (b, 0, off + h))
    par = lambda r, off: pl.BlockSpec((r, dqk), lambda b, h, off=off: (0, off + h))
    return pl.pallas_call(
        functools.partial(_mlstm_kernel, n_lat=n_lat, n_ctx=n_ctx),
        grid=(b, nh),
        in_specs=[
            seq(dqk, 0), seq(dqk, nh), seq(dv, 2048 // dv), seq(dv, 4096 // dv), seq(128, 0),
            par(3, 0), par(3, nh), par(1, 0), par(1, nh),
            pl.BlockSpec((None, 1, 128), lambda b, h: (h, 0, 0)),
            pl.BlockSpec((None, 1, dv), lambda b, h: (h, 0, 0)),
            pl.BlockSpec(tri.shape, lambda b, h: (0, 0, 0)),
        ],
        out_specs=pl.BlockSpec((None, lt, dv), lambda b, h: (b, 0, h)),
        out_shape=jax.ShapeDtypeStruct((b, lt, nh * dv), BF16),
        scratch_shapes=[
            pltpu.VMEM((nc, dqk, ch), BF16), pltpu.VMEM((lt, dqk), BF16),
            pltpu.VMEM((nc, dv + MLSTM_ONES_ROWS, ch), BF16), pltpu.VMEM((nc, 128, ch), F32),
            pltpu.VMEM((nc, dv, ch), F32), pltpu.VMEM((dv + MLSTM_ONES_ROWS, dqk), F32),
        ],
        compiler_params=_cparams("parallel", "parallel"),
        name="mlstm",
    )(u, u, u, u, us, conv_w, conv_w, conv_b, conv_b, gate_b, nw, tri)


def _pad_lanes(a, width=128):
    return jnp.pad(a, [(0, 0)] * (a.ndim - 1) + [(0, width - a.shape[-1])])


def _even_params(w_in, dt_bias, a_log, d_skip):
    heads = a_log.shape[1]
    hg = heads // SSD_GROUPS
    w_main = jnp.concatenate([w_in[:, :2560], w_in[:, 2560 + 2 * heads:]], axis=1).astype(BF16)
    dt_w = w_in[:, 2560:2560 + 2 * heads]
    per_group = lambda a: [_pad_lanes(jnp.concatenate([a[..., g * hg:(g + 1) * hg], a[..., heads + g * hg:heads + (g + 1) * hg]], axis=-1))
                           for g in range(SSD_GROUPS)]
    w_small = jnp.concatenate(per_group(dt_w), axis=1).astype(BF16)
    flat = lambda a: a.reshape(1, 2 * heads)
    dtb = jnp.stack(per_group(flat(dt_bias)))
    alog = jnp.stack(per_group(flat(a_log)))
    dsk = jnp.broadcast_to(jnp.repeat(d_skip, SSD_HEAD_DIM).reshape(SSD_GROUPS, hg * SSD_HEAD_DIM, 1),
                           (SSD_GROUPS, hg * SSD_HEAD_DIM, 128))
    return w_main, w_small, dtb, alog, dsk


def _odd_params(w_in, gate_b):
    nh = MLSTM_HEADS
    w_main = w_in[:, :6144].astype(BF16)
    gw = w_in[:, 6144:]
    w_small = jnp.concatenate([_pad_lanes(gw[:, h::nh]) for h in range(nh)], axis=1).astype(BF16)
    gb = jnp.stack([_pad_lanes(gate_b[:, h].reshape(1, 4)) for h in range(nh)])
    return w_main, w_small, gb


def kernel(x, c, ctx, c_ctx, mod_w, mod_b, norm_w, final_norm_w, mlp_w1, mlp_w2, even_w_in, even_w_out, ssd_conv_w, ssd_conv_b, ssd_a_log, ssd_dt_bias, ssd_d, ssd_norm_w, hgrn_lb, hgrn_norm_w, odd_w_in, odd_w_out, mlstm_conv_w, mlstm_conv_b, mlstm_gate_b, mlstm_norm_w):
    bsz, n_lat, d = x.shape
    n_ctx = ctx.shape[1]
    depth = mod_w.shape[0]
    lt = n_lat + n_ctx
    assert bsz < C_ROWS and n_lat % CONV_PIECE == 0 and n_ctx % SSD_CHUNK == 0 and n_ctx & (n_ctx - 1) == 0
    ctx_row = bsz

    c_all = jnp.zeros((C_ROWS, d), F32).at[:bsz].set(c).at[ctx_row].set(c_ctx)
    mods = _modulation(c_all, mod_w, mod_b).reshape(depth, C_ROWS, N_MOD, 1, d)

    lb_all = jnp.cumsum(jax.nn.softmax(hgrn_lb.astype(F32), axis=0), axis=0)
    lb_all = lb_all - lb_all[0]

    tm_in = lt // 2
    tm_full = lt // 4
    tm_lat = min(512, n_lat)

    xx = jnp.concatenate([x, ctx], axis=1)
    for layer in range(depth):
        last = layer == depth - 1
        nw1 = norm_w[layer, 0].reshape(1, d)
        nw2 = norm_w[layer, 1].reshape(1, d)
        if layer % 2 == 0:
            e = layer // 2
            w_main, w_small, dtb, alog, dsk = _even_params(even_w_in[e], ssd_dt_bias[e], ssd_a_log[e], ssd_d[e])
            u, us = _inproj(xx, nw1, mods, layer, ctx_row, w_main, w_small, n_lat, tm_in, 1536)
            ya = _ssd(u, us, ssd_conv_w[e], ssd_conv_b[e].reshape(1, -1), dtb, alog, dsk,
                      ssd_norm_w[e].reshape(SSD_GROUPS, 1, -1), n_lat, n_ctx)
            yb = _hgrn(u, lb_all[e].reshape(HGRN_HEADS, 1, -1), hgrn_norm_w[e].reshape(HGRN_HEADS, 1, -1), n_lat, n_ctx)
            ys, w_out = [ya, yb], even_w_out[e].astype(BF16)
        else:
            o = layer // 2
            w_main, w_small, gb = _odd_params(odd_w_in[o], mlstm_gate_b[o])
            u, us = _inproj(xx, nw1, mods, layer, ctx_row, w_main, w_small, n_lat, tm_in, 1024)
            yc = _mlstm(u, us, mlstm_conv_w[o], mlstm_conv_b[o].reshape(1, -1), gb,
                        mlstm_norm_w[o].reshape(MLSTM_HEADS, 1, -1), n_lat, n_ctx)
            ys, w_out = [yc], odd_w_out[o].astype(BF16)
        rows, tm = (n_lat, tm_lat) if last else (lt, tm_full)
        xx = _outproj(ys, w_out, xx, mods, layer, ctx_row, n_lat, rows, tm // 2 if not last else tm)
        xx = _mlp(xx, nw2, mods, layer, ctx_row, mlp_w1[layer].astype(BF16), mlp_w2[layer].astype(BF16),
                  final_norm_w.reshape(1, d), n_lat, rows, tm, 1024, last)
    return xx
```

```python
import functools

import numpy as np
import jax
import jax.numpy as jnp
from jax import lax
from jax.experimental import pallas as pl
from jax.experimental.pallas import tpu as pltpu

F32 = jnp.float32
BF16 = jnp.bfloat16

EPS = 1e-6
GRID_W = 64
N_MOD = 6
HGRN_LOG2_FLOOR = -1e5
C_ROWS = 32

V7X_VMEM_BYTES = 64 * 1024 * 1024
VMEM_LIMIT = V7X_VMEM_BYTES - 8 * 1024 * 1024

SSD_CHUNK = 128
SSD_HEAD_DIM = 64
SSD_STATE = 128
SSD_GROUPS = 2
HGRN_CHUNK = 64
HGRN_HEADS = 8
MLSTM_CHUNK = 128
MLSTM_HEADS = 4
MLSTM_ONES_ROWS = 16
CONV_PIECE = 256
NORM_ROWS = 16


def _cparams(*sem):
    return pltpu.CompilerParams(dimension_semantics=sem, vmem_limit_bytes=VMEM_LIMIT)


def _silu(x):
    return x * jax.nn.sigmoid(x)


def _softplus(x):
    return jnp.maximum(x, 0.0) + jnp.log(1.0 + jnp.exp(-jnp.abs(x)))


def _log_sigmoid(x):
    return -_softplus(-x)


def _dot(a, b):
    return jnp.dot(a, b, preferred_element_type=F32)


def _dot_nt(a, b):
    return lax.dot_general(a, b, (((1,), (1,)), ((), ())), preferred_element_type=F32)


def _dot_tn(a, b):
    return lax.dot_general(a, b, (((0,), (0,)), ((), ())), preferred_element_type=F32)


def _dot01(m01x3, x):
    hi = x.astype(BF16)
    r = x - hi.astype(F32)
    mid = r.astype(BF16)
    lo = (r - mid.astype(F32)).astype(BF16)
    return _dot(m01x3, jnp.concatenate([hi, mid, lo], axis=0))


def _dot01_rows(x, m01x3):
    hi = x.astype(BF16)
    r = x - hi.astype(F32)
    mid = r.astype(BF16)
    lo = (r - mid.astype(F32)).astype(BF16)
    return _dot(jnp.concatenate([hi, mid, lo], axis=1), m01x3)


def _rms(x):
    return x * lax.rsqrt(jnp.mean(x * x, axis=-1, keepdims=True) + EPS)


def _conv3(u, w, b, period):
    rows = u.shape[0]
    t = lax.broadcasted_iota(jnp.int32, u.shape, 0) & (period - 1)
    left = jnp.where(t == 0, 0.0, pltpu.roll(u, 1, axis=0))
    right = jnp.where(t == period - 1, 0.0, pltpu.roll(u, rows - 1, axis=0))
    return left * w[0:1] + u * w[1:2] + right * w[2:3] + b


def _modulated_norm(x_ref, h_ref, nw, sh_l, sc_l, sh_c, sc_c, row0, n_lat):
    a_l, a_c = nw * (1.0 + sc_l), nw * (1.0 + sc_c)

    def block(r, carry):
        start = pl.multiple_of(r * NORM_ROWS, NORM_ROWS)
        rs = pl.ds(start, NORM_ROWS)
        is_ctx = row0 + start >= n_lat
        h_ref[rs, :] = (_rms(x_ref[rs, :]) * jnp.where(is_ctx, a_c, a_l) + jnp.where(is_ctx, sh_c, sh_l)).astype(BF16)
        return carry

    lax.fori_loop(0, x_ref.shape[0] // NORM_ROWS, block, 0, unroll=4)


def _mod_kernel(c_ref, w_ref, b_ref, o_ref):
    a = _silu(c_ref[...]).astype(BF16)
    o_ref[...] = _dot(a, w_ref[...].astype(BF16)) + b_ref[...]


def _modulation(c_all, mod_w, mod_b):
    depth, d, n = mod_w.shape
    tn = 1024
    return pl.pallas_call(
        _mod_kernel,
        grid=(depth, n // tn),
        in_specs=[
            pl.BlockSpec((C_ROWS, d), lambda l, j: (0, 0)),
            pl.BlockSpec((None, d, tn), lambda l, j: (l, 0, j)),
            pl.BlockSpec((None, 1, tn), lambda l, j: (l, 0, j)),
        ],
        out_specs=pl.BlockSpec((None, C_ROWS, tn), lambda l, j: (l, 0, j)),
        out_shape=jax.ShapeDtypeStruct((depth, C_ROWS, n), F32),
        compiler_params=_cparams("parallel", "parallel"),
        name="modulation",
    )(c_all, mod_w, mod_b.reshape(depth, 1, n))


def _mod_specs(layer, ctx_row, ks, d, nargs):
    specs = []
    for k in ks:
        if nargs == 3:
            specs.append(pl.BlockSpec((None, None, None, 1, d), lambda b, i, j, k=k: (layer, b, k, 0, 0)))
            specs.append(pl.BlockSpec((None, None, None, 1, d), lambda b, i, j, k=k: (layer, ctx_row, k, 0, 0)))
        else:
            specs.append(pl.BlockSpec((None, None, None, 1, d), lambda b, i, k=k: (layer, b, k, 0, 0)))
            specs.append(pl.BlockSpec((None, None, None, 1, d), lambda b, i, k=k: (layer, ctx_row, k, 0, 0)))
    return specs


def _inproj_kernel(x_ref, nw_ref, shl_ref, shc_ref, scl_ref, scc_ref, w_ref, ws_ref, o_ref, os_ref, h_ref, *, n_lat):
    @pl.when(pl.program_id(2) == 0)
    def _():
        _modulated_norm(x_ref, h_ref, nw_ref[...], shl_ref[...], scl_ref[...], shc_ref[...], scc_ref[...],
                        pl.program_id(1) * x_ref.shape[0], n_lat)
        os_ref[...] = _dot(h_ref[...], ws_ref[...])

    o_ref[...] = _dot(h_ref[...], w_ref[...]).astype(o_ref.dtype)


def _inproj(x, nw, mods, layer, ctx_row, w_main, w_small, n_lat, tm, tn):
    b, lt, d = x.shape
    n = w_main.shape[1]
    ns = w_small.shape[1]
    return pl.pallas_call(
        functools.partial(_inproj_kernel, n_lat=n_lat),
        grid=(b, lt // tm, n // tn),
        in_specs=[
            pl.BlockSpec((None, tm, d), lambda b, i, j: (b, i, 0)),
            pl.BlockSpec((1, d), lambda b, i, j: (0, 0)),
            *_mod_specs(layer, ctx_row, (0, 1), d, 3),
            pl.BlockSpec((d, tn), lambda b, i, j: (0, j)),
            pl.BlockSpec((d, ns), lambda b, i, j: (0, 0)),
        ],
        out_specs=[
            pl.BlockSpec((None, tm, tn), lambda b, i, j: (b, i, j)),
            pl.BlockSpec((None, tm, ns), lambda b, i, j: (b, i, 0)),
        ],
        out_shape=[jax.ShapeDtypeStruct((b, lt, n), BF16), jax.ShapeDtypeStruct((b, lt, ns), F32)],
        scratch_shapes=[pltpu.VMEM((tm, d), BF16)],
        compiler_params=_cparams("parallel", "parallel", "arbitrary"),
        name="inproj",
    )(x, nw, mods, mods, mods, mods, w_main, w_small)


def _outproj_kernel(*refs, n_y, n_lat):
    y_refs, w_refs = refs[:n_y], refs[n_y:2 * n_y]
    x_ref, gl_ref, gc_ref, o_ref = refs[2 * n_y:]
    acc = _dot(y_refs[0][...], w_refs[0][...])
    for y_ref, w_ref in zip(y_refs[1:], w_refs[1:]):
        acc = acc + _dot(y_ref[...], w_ref[...])
    tm = x_ref.shape[0]
    rows = pl.program_id(1) * tm + lax.broadcasted_iota(jnp.int32, (tm, 1), 0)
    g = jnp.where(rows >= n_lat, gc_ref[...], gl_ref[...])
    o_ref[...] = x_ref[...] + g * acc


def _outproj(ys, w, x, mods, layer, ctx_row, n_lat, rows, tm):
    b, lt, d = x.shape
    n_y = len(ys)
    y_specs = [pl.BlockSpec((None, tm, y.shape[2]), lambda b, i: (b, i, 0)) for y in ys]
    w_specs = [pl.BlockSpec((y.shape[2], d), lambda b, i, k=k: (k, 0)) for k, y in enumerate(ys)]
    return pl.pallas_call(
        functools.partial(_outproj_kernel, n_y=n_y, n_lat=n_lat),
        grid=(b, rows // tm),
        in_specs=[*y_specs, *w_specs,
                  pl.BlockSpec((None, tm, d), lambda b, i: (b, i, 0)),
                  *_mod_specs(layer, ctx_row, (2,), d, 2)],
        out_specs=pl.BlockSpec((None, tm, d), lambda b, i: (b, i, 0)),
        out_shape=jax.ShapeDtypeStruct((b, rows, d), F32),
        compiler_params=_cparams("parallel", "parallel"),
        name="outproj",
    )(*ys, *([w] * n_y), x, mods, mods)


def _mlp_kernel(x_ref, nw_ref, shl_ref, shc_ref, scl_ref, scc_ref, gl_ref, gc_ref, w1_ref, w2_ref, fw_ref,
                o_ref, h_ref, acc_ref, *, n_lat, final_norm):
    f = pl.program_id(2)
    tm = x_ref.shape[0]

    @pl.when(f == 0)
    def _():
        _modulated_norm(x_ref, h_ref, nw_ref[...], shl_ref[...], scl_ref[...], shc_ref[...], scc_ref[...],
                        pl.program_id(1) * tm, n_lat)
        acc_ref[...] = jnp.zeros_like(acc_ref)

    a = jnp.maximum(_dot(h_ref[...], w1_ref[...]), 0.0)
    acc_ref[...] += _dot((a * a).astype(BF16), w2_ref[...])

    @pl.when(f == pl.num_programs(2) - 1)
    def _():
        rows = pl.program_id(1) * tm + lax.broadcasted_iota(jnp.int32, (tm, 1), 0)
        g = jnp.where(rows >= n_lat, gc_ref[...], gl_ref[...])
        y = x_ref[...] + g * acc_ref[...]
        if final_norm:
            y = _rms(y) * fw_ref[...]
        o_ref[...] = y


def _mlp(x, nw, mods, layer, ctx_row, w1, w2, fw, n_lat, rows, tm, tf, final_norm):
    b, _, d = x.shape
    dff = w1.shape[1]
    return pl.pallas_call(
        functools.partial(_mlp_kernel, n_lat=n_lat, final_norm=final_norm),
        grid=(b, rows // tm, dff // tf),
        in_specs=[
            pl.BlockSpec((None, tm, d), lambda b, i, j: (b, i, 0)),
            pl.BlockSpec((1, d), lambda b, i, j: (0, 0)),
            *_mod_specs(layer, ctx_row, (3, 4, 5), d, 3),
            pl.BlockSpec((d, tf), lambda b, i, j: (0, j)),
            pl.BlockSpec((tf, d), lambda b, i, j: (j, 0)),
            pl.BlockSpec((1, d), lambda b, i, j: (0, 0)),
        ],
        out_specs=pl.BlockSpec((None, tm, d), lambda b, i, j: (b, i, 0)),
        out_shape=jax.ShapeDtypeStruct((b, rows, d), F32),
        scratch_shapes=[pltpu.VMEM((tm, d), BF16), pltpu.VMEM((tm, d), F32)],
        compiler_params=_cparams("parallel", "parallel", "arbitrary"),
        name="mlp",
    )(x, nw, mods, mods, mods, mods, mods, mods, w1, w2, fw)


def _tri_consts(c):
    lower = np.tril(np.ones((c, c), np.float32))
    tri = np.stack([lower, lower.T])
    return jnp.asarray(np.concatenate([tri, tri, tri], axis=2), BF16)


def _chunk_index(j, d, n_lat_chunks, n_chunks):
    if d == 0:
        c = j + n_lat_chunks
        return jnp.where(c >= n_chunks, c - n_chunks, c)
    return n_chunks - 1 - j


def _conv_rows(n_lat, n_ctx, fn):
    def body(p, carry):
        fn(pl.multiple_of(p * CONV_PIECE, CONV_PIECE), CONV_PIECE, GRID_W)
        return carry
    lax.fori_loop(0, n_lat // CONV_PIECE, body, 0)
    fn(n_lat, n_ctx, n_ctx)


def _ssd_kernel(x_ref, b_ref, c_ref, z_ref, dt_ref, cwx_ref, cwb_ref, cwc_ref, cbx_ref, cbb_ref, cbc_ref,
                dtb_ref, alog_ref, dsk_ref, nw_ref, tri_ref, o_ref,
                xst, bs, cst, dtt, acs, yst, hst, *, n_lat, n_ctx):
    ch = SSD_CHUNK
    p = SSD_HEAD_DIM
    gw = x_ref.shape[1]
    heads = gw // p
    n_chunks = (n_lat + n_ctx) // ch
    n_lat_chunks = n_lat // ch

    def conv_piece(r0, rows, period):
        sl = pl.ds(r0, rows)
        c0 = r0 // ch
        x = _silu(_conv3(x_ref[sl, :].astype(F32), cwx_ref[...], cbx_ref[...], period))
        c = _silu(_conv3(c_ref[sl, :].astype(F32), cwc_ref[...], cbc_ref[...], period))
        bs[sl, :] = _silu(_conv3(b_ref[sl, :].astype(F32), cwb_ref[...], cbb_ref[...], period)).astype(BF16)
        dt = _softplus(dt_ref[sl, :] + dtb_ref[...])
        acs[sl, :] = -jnp.exp(alog_ref[...]) * dt
        for k in range(rows // ch):
            rk = slice(k * ch, (k + 1) * ch)
            for j in range(gw // ch):
                xst[c0 + k, j * ch:(j + 1) * ch, :] = x[rk, j * ch:(j + 1) * ch].T
            cst[c0 + k] = c[rk].T.astype(BF16)
            dtt[c0 + k] = dt[rk].T

    _conv_rows(n_lat, n_ctx, conv_piece)

    si = lax.broadcasted_iota(jnp.int32, (ch, ch), 0)
    ti = lax.broadcasted_iota(jnp.int32, (ch, ch), 1)

    for d in range(2):
        mask = (si <= ti) if d == 0 else (si >= ti)
        hst[...] = jnp.zeros_like(hst)

        def body(j, carry, d=d, mask=mask):
            c = _chunk_index(j, d, n_lat_chunks, n_chunks)
            sl = pl.ds(pl.multiple_of(c * ch, ch), ch)
            cum = _dot01(tri_ref[d], acs[sl, :])
            cum_t = cum.T
            tot_c = cum_t[:, ch - 1:ch] if d == 0 else cum_t[:, 0:1]
            e_tot = jnp.exp(cum[ch - 1:ch, :] if d == 0 else cum[0:1, :])
            dt_t = dtt[c]
            e_cum_t = jnp.exp(cum_t)
            e_end_t = jnp.exp(tot_c - cum_t) * dt_t
            bc = bs[sl, :]
            cc_t = cst[c]
            cb_t = _dot(bc, cc_t)
            for h in range(heads):
                col = d * heads + h
                hs = slice(h * p, (h + 1) * p)
                diff = cum_t[col:col + 1, :] - cum[:, col:col + 1]
                decay = jnp.exp(jnp.where(mask, diff, -jnp.inf))
                xh = xst[c, hs, :]
                state = hst[h]
                y = _dot((xh * dt_t[col:col + 1, :]).astype(BF16), (cb_t * decay).astype(BF16))
                y = y + _dot(state.astype(BF16), cc_t) * e_cum_t[col:col + 1, :]
                xw = (xh * e_end_t[col:col + 1, :]).astype(BF16)
                hst[h] = state * e_tot[:, col:col + 1] + _dot(xw, bc)
                if d == 0:
                    yst[c, hs, :] = y
                else:
                    yst[c, hs, :] = yst[c, hs, :] + y
            if d == 1:
                g_t = yst[c] + xst[c] * dsk_ref[...]
                g = jnp.concatenate([g_t[j * ch:(j + 1) * ch].T for j in range(gw // ch)], axis=1)
                g = g * _silu(z_ref[sl, :].astype(F32))
                o_ref[sl, :] = (_rms(g) * nw_ref[...]).astype(BF16)
            return carry

        lax.fori_loop(0, n_chunks, body, 0, unroll=2)


def _ssd(u, us, conv_w, conv_b, dtb, alog, dsk, nw, n_lat, n_ctx):
    b, lt, _ = u.shape
    gw = SSD_HEAD_DIM * 8
    ns = SSD_STATE
    nc = lt // SSD_CHUNK
    x_blk, b_blk, c_blk = 1024 // gw, 2048 // ns, 2304 // ns
    cx_blk, cb_blk, cc_blk = 0, 1024 // ns, 1280 // ns
    seq = lambda w, off: pl.BlockSpec((None, lt, w), lambda b, g, off=off: (b, 0, off + g))
    par = lambda r, w, off: pl.BlockSpec((r, w), lambda b, g, off=off: (0, off + g))
    grp = lambda w: pl.BlockSpec((None, 1, w), lambda b, g: (g, 0, 0))
    return pl.pallas_call(
        functools.partial(_ssd_kernel, n_lat=n_lat, n_ctx=n_ctx),
        grid=(b, SSD_GROUPS),
        in_specs=[
            seq(gw, x_blk), seq(ns, b_blk), seq(ns, c_blk), seq(gw, 0),
            pl.BlockSpec((None, lt, 128), lambda b, g: (b, 0, g)),
            par(3, gw, cx_blk), par(3, ns, cb_blk), par(3, ns, cc_blk),
            par(1, gw, cx_blk), par(1, ns, cb_blk), par(1, ns, cc_blk),
            grp(128), grp(128), pl.BlockSpec((None, gw, 128), lambda b, g: (g, 0, 0)), grp(gw),
            pl.BlockSpec((2, SSD_CHUNK, 3 * SSD_CHUNK), lambda b, g: (0, 0, 0)),
        ],
        out_specs=pl.BlockSpec((None, lt, gw), lambda b, g: (b, 0, g)),
        out_shape=jax.ShapeDtypeStruct((b, lt, SSD_GROUPS * gw), BF16),
        scratch_shapes=[
            pltpu.VMEM((nc, gw, SSD_CHUNK), F32), pltpu.VMEM((lt, ns), BF16), pltpu.VMEM((nc, ns, SSD_CHUNK), BF16),
            pltpu.VMEM((nc, 128, SSD_CHUNK), F32), pltpu.VMEM((lt, 128), F32), pltpu.VMEM((nc, gw, SSD_CHUNK), F32),
            pltpu.VMEM((gw // SSD_HEAD_DIM, SSD_HEAD_DIM, ns), F32),
        ],
        compiler_params=_cparams("parallel", "parallel"),
        name="ssd",
    )(u, u, u, u, us, conv_w, conv_w, conv_w, conv_b, conv_b, conv_b, dtb, alog, dsk, nw, _tri_consts(SSD_CHUNK))


_HGRN_LEVELS = (32, 16, 8, 4, 2, 1)


def _hgrn_consts():
    c = HGRN_CHUNK
    sums = np.zeros((7, c, c), np.float32)
    pairs = np.zeros((7, c, c), np.float32)
    for li, m in enumerate(_HGRN_LEVELS):
        for t in range(c):
            beta = (t // (2 * m)) * 2 * m
            mid = beta + m
            if t >= mid:
                sums[li, t, mid:t + 1] = 1.0
                pairs[li, t, beta:mid] = 1.0
            else:
                sums[li, t, t + 1:mid] = 1.0
    sums[6] = np.tril(np.ones((c, c), np.float32))
    pairs[6] = np.eye(c, dtype=np.float32)
    sums = np.stack([sums, sums[:, ::-1, ::-1]]).reshape(2, 7 * c, c)
    sums = np.concatenate([sums, sums, sums, np.zeros_like(sums)], axis=2)
    pairs = np.stack([pairs, pairs[:, ::-1, ::-1]])
    zero = np.zeros_like(pairs)
    pairs = np.concatenate([np.concatenate([pairs, zero], axis=3), np.concatenate([zero, pairs], axis=3)], axis=2)
    return jnp.asarray(sums, BF16), jnp.asarray(pairs, F32)


def _hgrn_kernel(q_ref, ff_ref, fb_ref, i_ref, g_ref, lb_ref, nw_ref, sums_ref, pairs_ref, o_ref,
                 ys, qb_s, *dir_scratch, n_lat, n_ctx):
    w_s, kb_s, qd_s, kd_s, et_s, att_s, p_s = zip(dir_scratch[:7], dir_scratch[7:])
    ch = HGRN_CHUNK
    n_chunks = (n_lat + n_ctx) // ch
    n_lat_chunks = n_lat // ch
    n_pairs = n_chunks // 2
    n_lv = len(_HGRN_LEVELS)
    lb = lb_ref[...]
    one_m_lb = 1.0 - lb
    dk = q_ref.shape[1]
    dv = i_ref.shape[1]
    t_idx = lax.broadcasted_iota(jnp.int32, (ch, dk), 0)

    laters = [[((t_idx & m) != 0) == (d == 0) for m in _HGRN_LEVELS] for d in range(2)]

    def operands(d, pi):
        f_ref = ff_ref if d == 0 else fb_ref
        r0 = pl.multiple_of(pi * 2 * ch, 2 * ch)
        sl = pl.ds(r0, 2 * ch)
        gate = jax.nn.sigmoid(f_ref[sl, :].astype(F32))
        kin = one_m_lb * (1.0 - gate)
        logf2 = jnp.maximum(jnp.log2(lb + one_m_lb * gate), HGRN_LOG2_FLOOR)
        q = _silu(q_ref[sl, :].astype(F32))
        if d == 0:
            qb_s[sl, :] = q.astype(BF16)
        kb_s[d][sl, :] = kin.astype(BF16)
        lf = jnp.concatenate([logf2[:ch], logf2[ch:]], axis=1)
        hi = lf.astype(BF16)
        r1 = lf - hi.astype(F32)
        mid = r1.astype(BF16)
        lo = (r1 - mid.astype(F32)).astype(BF16)
        rel2 = _dot(sums_ref[d], jnp.concatenate([hi, mid, lo, jnp.zeros_like(hi)], axis=0))
        for half in range(2):
            rows = slice(half * ch, (half + 1) * ch)
            rel = rel2[:, half * dk:(half + 1) * dk]
            qh, kh = q[rows], kin[rows]
            c = pi * 2 + half
            for li in range(n_lv):
                e = jnp.exp2(rel[li * ch:(li + 1) * ch])
                w_s[d][pi, li, rows, :] = (jnp.where(laters[d][li], qh, kh) * e).astype(BF16)
            bcum = rel[n_lv * ch:(n_lv + 1) * ch]
            tot = bcum[ch - 1:ch] if d == 0 else bcum[0:1]
            hs = pl.ds(r0 + half * ch, ch)
            qd_s[d][hs, :] = (qh * jnp.exp2(bcum)).astype(BF16)
            kd_s[d][hs, :] = (kh * jnp.exp2(tot - bcum)).astype(BF16)
            et_s[d][c] = jnp.broadcast_to(jnp.exp2(tot), (8, dk))

    def intra(d, pi):
        sl = pl.ds(pl.multiple_of(pi * 2 * ch, 2 * ch), 2 * ch)
        att = pairs_ref[d, n_lv] * _dot_nt(qb_s[sl, :], kb_s[d][sl, :])
        for li in range(n_lv):
            w = w_s[d][pi, li]
            att = att + pairs_ref[d, li] * _dot_nt(w, w)
        att_s[d][sl, :] = att.astype(BF16)
        for half in range(2):
            hs = pl.ds(pl.multiple_of(pi * 2 * ch, 2 * ch) + half * ch, ch)
            p_s[d][pi * 2 + half] = _dot_tn(i_ref[hs, :], kd_s[d][hs, :])

    def scan(d, j, state_t):
        pi = _chunk_index(j, d, n_lat_chunks // 2, n_pairs)
        r0 = pl.multiple_of(pi * 2 * ch, 2 * ch)
        sl = pl.ds(r0, 2 * ch)
        o_intra = _dot(att_s[d][sl, :], i_ref[sl, :])
        o_halves = [None, None]
        for half in ((0, 1) if d == 0 else (1, 0)):
            hs = pl.ds(r0 + half * ch, ch)
            o_halves[half] = (o_intra[half * ch:(half + 1) * ch]
                              + _dot_nt(qd_s[d][hs, :], state_t.astype(BF16)))
            c = pi * 2 + half
            state_t = state_t * et_s[d][c][0:1] + p_s[d][c]
        o = jnp.concatenate(o_halves, axis=0)
        if d == 0:
            ys[sl, :] = o
        else:
            o = ys[sl, :] + o
            o_ref[sl, :] = (_rms(o) * nw_ref[...] * _silu(g_ref[sl, :].astype(F32))).astype(BF16)
        return state_t

    zero_state = jnp.zeros((dv, dk), F32)

    def stage_a(pi, carry):
        operands(0, pi)
        return carry

    def stage_b(pi, carry):
        intra(0, pi)
        operands(1, pi)
        return carry

    def stage_c(j, state_t):
        intra(1, j)
        return scan(0, j, state_t)

    lax.fori_loop(0, n_pairs, stage_a, 0, unroll=2)
    lax.fori_loop(0, n_pairs, stage_b, 0, unroll=2)
    lax.fori_loop(0, n_pairs, stage_c, zero_state, unroll=3)
    lax.fori_loop(0, n_pairs, functools.partial(scan, 1), zero_state, unroll=3)


def _hgrn(u, lb, nw, n_lat, n_ctx):
    b, lt, _ = u.shape
    w = 128
    ch = HGRN_CHUNK
    n_chunks = lt // ch
    sums, pairs = _hgrn_consts()
    seq = lambda off: pl.BlockSpec((None, lt, w), lambda b, h, off=off: (b, 0, off + h))
    head = pl.BlockSpec((None, 1, w), lambda b, h: (h, 0, 0))
    return pl.pallas_call(
        functools.partial(_hgrn_kernel, n_lat=n_lat, n_ctx=n_ctx),
        grid=(b, HGRN_HEADS),
        in_specs=[seq(2560 // w), seq(3584 // w), seq(4608 // w), seq(5632 // w), seq(6656 // w), head, head,
                  pl.BlockSpec(sums.shape, lambda b, h: (0, 0, 0)),
                  pl.BlockSpec(pairs.shape, lambda b, h: (0, 0, 0, 0))],
        out_specs=pl.BlockSpec((None, lt, w), lambda b, h: (b, 0, h)),
        out_shape=jax.ShapeDtypeStruct((b, lt, HGRN_HEADS * w), BF16),
        scratch_shapes=[
            pltpu.VMEM((lt, w), F32), pltpu.VMEM((lt, w), BF16),
            *([pltpu.VMEM((n_chunks // 2, len(_HGRN_LEVELS), 2 * ch, w), BF16),
               pltpu.VMEM((lt, w), BF16), pltpu.VMEM((lt, w), BF16), pltpu.VMEM((lt, w), BF16),
               pltpu.VMEM((n_chunks, 8, w), F32), pltpu.VMEM((lt, 2 * ch), BF16),
               pltpu.VMEM((n_chunks, w, w), F32)] * 2),
        ],
        compiler_params=_cparams("parallel", "parallel"),
        name="hgrn2",
    )(u, u, u, u, u, lb, nw, sums, pairs)


def _mlstm_kernel(q_ref, k_ref, v_ref, og_ref, gt_ref, cwq_ref, cwk_ref, cbq_ref, cbk_ref, gb_ref, nw_ref, tri_ref,
                  o_ref, qst, ks, vst, gst, yst, cst, *, n_lat, n_ctx):
    ch = MLSTM_CHUNK
    n_chunks = (n_lat + n_ctx) // ch
    n_lat_chunks = n_lat // ch
    dqk = q_ref.shape[1]
    dv = v_ref.shape[1]
    k_scale = dqk ** -0.5

    def conv_piece(r0, rows, period):
        sl = pl.ds(r0, rows)
        c0 = r0 // ch
        q = _silu(_conv3(q_ref[sl, :].astype(F32), cwq_ref[...], cbq_ref[...], period))
        ks[sl, :] = (_silu(_conv3(k_ref[sl, :].astype(F32), cwk_ref[...], cbk_ref[...], period)) * k_scale).astype(BF16)
        raw = gt_ref[sl, :] + gb_ref[...]
        lane = lax.broadcasted_iota(jnp.int32, raw.shape, 1)
        g = jnp.where(lane < 2, raw, _log_sigmoid(raw))
        v = v_ref[sl, :].astype(F32)
        for k in range(rows // ch):
            rk = slice(k * ch, (k + 1) * ch)
            for j in range(dqk // ch):
                qst[c0 + k, j * ch:(j + 1) * ch, :] = q[rk, j * ch:(j + 1) * ch].T.astype(BF16)
            for j in range(dv // ch):
                vst[c0 + k, j * ch:(j + 1) * ch, :] = v[rk, j * ch:(j + 1) * ch].T.astype(BF16)
            vst[c0 + k, dv:, :] = jnp.ones((MLSTM_ONES_ROWS, ch), BF16)
            gst[c0 + k] = g[rk].T

    _conv_rows(n_lat, n_ctx, conv_piece)

    si = lax.broadcasted_iota(jnp.int32, (ch, ch), 0)
    ti = lax.broadcasted_iota(jnp.int32, (ch, ch), 1)

    for d in range(2):
        mask = (si <= ti) if d == 0 else (si >= ti)
        cst[...] = jnp.zeros_like(cst)

        def body(j, m_prev, d=d, mask=mask):
            c = _chunk_index(j, d, n_lat_chunks, n_chunks)
            sl = pl.ds(pl.multiple_of(c * ch, ch), ch)
            g_t = gst[c]
            cum_t = _dot01_rows(g_t, tri_ref[d])
            brow = cum_t[2 + d:3 + d, :]
            irow = g_t[d:d + 1, :]
            tot = brow[:, ch - 1:ch] if d == 0 else brow[:, 0:1]
            logd = jnp.where(mask, brow + jnp.broadcast_to(irow - brow, (ch, ch)).T, -jnp.inf)
            gstate = brow + m_prev
            mt = jnp.maximum(jnp.max(logd, axis=0, keepdims=True), gstate)
            q_t = qst[c]
            kc = ks[sl, :]
            w = _dot(kc, q_t) * jnp.exp(logd - mt)
            sw = jnp.exp(gstate - mt)
            v_t = vst[c]
            state = cst[...]
            qstate = _dot(state.astype(BF16), q_t)
            num = _dot(v_t[:dv], w.astype(BF16)) + sw * qstate[:dv]
            den = jnp.sum(w, axis=0, keepdims=True) + sw * qstate[dv:dv + 1]
            hout = num * (1.0 / jnp.maximum(jnp.abs(den), jnp.exp(-mt)))
            logw = tot - brow + irow
            m_new = jnp.maximum(tot + m_prev, jnp.max(logw, axis=1, keepdims=True))
            ws = jnp.exp(logw - m_new).astype(BF16)
            cst[...] = jnp.exp(tot + m_prev - m_new) * state + _dot(v_t * ws, kc)
            if d == 0:
                yst[c] = hout
            else:
                hh_t = yst[c] + hout
                hh = jnp.concatenate([hh_t[i * ch:(i + 1) * ch].T for i in range(dv // ch)], axis=1)
                o_ref[sl, :] = (_rms(hh) * nw_ref[...] * jax.nn.sigmoid(og_ref[sl, :].astype(F32))).astype(BF16)
            return m_new

        lax.fori_loop(0, n_chunks, body, jnp.zeros((1, 1), F32), unroll=2)


def _mlstm(u, us, conv_w, conv_b, gate_b, nw, n_lat, n_ctx):
    b, lt, _ = u.shape
    dqk, dv = 256, 512
    nh = MLSTM_HEADS
    ch = MLSTM_CHUNK
    nc = lt // ch
    upper = np.triu(np.ones((ch, ch), np.float32))
    tri = np.stack([upper, upper.T])
    tri = jnp.asarray(np.concatenate([tri, tri, tri], axis=1), BF16)
    seq = lambda w, off: pl.BlockSpec((None, lt, w), lambda b, h, off=off: (b, 0, off + h))
    par = lambda r, off: pl.BlockSpec((r, dqk), lambda b, h, off=off: (0, off + h))
    return pl.pallas_call(
        functools.partial(_mlstm_kernel, n_lat=n_lat, n_ctx=n_ctx),
        grid=(b, nh),
        in_specs=[
            seq(dqk, 0), seq(dqk, nh), seq(dv, 2048 // dv), seq(dv, 4096 // dv), seq(128, 0),
            par(3, 0), par(3, nh), par(1, 0), par(1, nh),
            pl.BlockSpec((None, 1, 128), lambda b, h: (h, 0, 0)),
            pl.BlockSpec((None, 1, dv), lambda b, h: (h, 0, 0)),
            pl.BlockSpec(tri.shape, lambda b, h: (0, 0, 0)),
        ],
        out_specs=pl.BlockSpec((None, lt, dv), lambda b, h: (b, 0, h)),
        out_shape=jax.ShapeDtypeStruct((b, lt, nh * dv), BF16),
        scratch_shapes=[
            pltpu.VMEM((nc, dqk, ch), BF16), pltpu.VMEM((lt, dqk), BF16),
            pltpu.VMEM((nc, dv + MLSTM_ONES_ROWS, ch), BF16), pltpu.VMEM((nc, 128, ch), F32),
            pltpu.VMEM((nc, dv, ch), F32), pltpu.VMEM((dv + MLSTM_ONES_ROWS, dqk), F32),
        ],
        compiler_params=_cparams("parallel", "parallel"),
        name="mlstm",
    )(u, u, u, u, us, conv_w, conv_w, conv_b, conv_b, gate_b, nw, tri)


def _pad_lanes(a, width=128):
    return jnp.pad(a, [(0, 0)] * (a.ndim - 1) + [(0, width - a.shape[-1])])


def _even_params(w_in, dt_bias, a_log, d_skip):
    heads = a_log.shape[1]
    hg = heads // SSD_GROUPS
    w_main = jnp.concatenate([w_in[:, :2560], w_in[:, 2560 + 2 * heads:]], axis=1).astype(BF16)
    dt_w = w_in[:, 2560:2560 + 2 * heads]
    per_group = lambda a: [_pad_lanes(jnp.concatenate([a[..., g * hg:(g + 1) * hg], a[..., heads + g * hg:heads + (g + 1) * hg]], axis=-1))
                           for g in range(SSD_GROUPS)]
    w_small = jnp.concatenate(per_group(dt_w), axis=1).astype(BF16)
    flat = lambda a: a.reshape(1, 2 * heads)
    dtb = jnp.stack(per_group(flat(dt_bias)))
    alog = jnp.stack(per_group(flat(a_log)))
    dsk = jnp.broadcast_to(jnp.repeat(d_skip, SSD_HEAD_DIM).reshape(SSD_GROUPS, hg * SSD_HEAD_DIM, 1),
                           (SSD_GROUPS, hg * SSD_HEAD_DIM, 128))
    return w_main, w_small, dtb, alog, dsk


def _odd_params(w_in, gate_b):
    nh = MLSTM_HEADS
    w_main = w_in[:, :6144].astype(BF16)
    gw = w_in[:, 6144:]
    w_small = jnp.concatenate([_pad_lanes(gw[:, h::nh]) for h in range(nh)], axis=1).astype(BF16)
    gb = jnp.stack([_pad_lanes(gate_b[:, h].reshape(1, 4)) for h in range(nh)])
    return w_main, w_small, gb


def kernel(x, c, ctx, c_ctx, mod_w, mod_b, norm_w, final_norm_w, mlp_w1, mlp_w2, even_w_in, even_w_out, ssd_conv_w, ssd_conv_b, ssd_a_log, ssd_dt_bias, ssd_d, ssd_norm_w, hgrn_lb, hgrn_norm_w, odd_w_in, odd_w_out, mlstm_conv_w, mlstm_conv_b, mlstm_gate_b, mlstm_norm_w):
    bsz, n_lat, d = x.shape
    n_ctx = ctx.shape[1]
    depth = mod_w.shape[0]
    lt = n_lat + n_ctx
    assert bsz < C_ROWS and n_lat % CONV_PIECE == 0 and n_ctx % SSD_CHUNK == 0 and n_ctx & (n_ctx - 1) == 0
    ctx_row = bsz

    c_all = jnp.zeros((C_ROWS, d), F32).at[:bsz].set(c).at[ctx_row].set(c_ctx)
    mods = _modulation(c_all, mod_w, mod_b).reshape(depth, C_ROWS, N_MOD, 1, d)

    lb_all = jnp.cumsum(jax.nn.softmax(hgrn_lb.astype(F32), axis=0), axis=0)
    lb_all = lb_all - lb_all[0]

    tm_in = lt // 2
    tm_full = lt // 4
    tm_lat = min(512, n_lat)

    xx = jnp.concatenate([x, ctx], axis=1)
    for layer in range(depth):
        last = layer == depth - 1
        nw1 = norm_w[layer, 0].reshape(1, d)
        nw2 = norm_w[layer, 1].reshape(1, d)
        if layer % 2 == 0:
            e = layer // 2
            w_main, w_small, dtb, alog, dsk = _even_params(even_w_in[e], ssd_dt_bias[e], ssd_a_log[e], ssd_d[e])
            u, us = _inproj(xx, nw1, mods, layer, ctx_row, w_main, w_small, n_lat, tm_in, 1536)
            ya = _ssd(u, us, ssd_conv_w[e], ssd_conv_b[e].reshape(1, -1), dtb, alog, dsk,
                      ssd_norm_w[e].reshape(SSD_GROUPS, 1, -1), n_lat, n_ctx)
            yb = _hgrn(u, lb_all[e].reshape(HGRN_HEADS, 1, -1), hgrn_norm_w[e].reshape(HGRN_HEADS, 1, -1), n_lat, n_ctx)
            ys, w_out = [ya, yb], even_w_out[e].astype(BF16)
        else:
            o = layer // 2
            w_main, w_small, gb = _odd_params(odd_w_in[o], mlstm_gate_b[o])
            u, us = _inproj(xx, nw1, mods, layer, ctx_row, w_main, w_small, n_lat, tm_in, 1024)
            yc = _mlstm(u, us, mlstm_conv_w[o], mlstm_conv_b[o].reshape(1, -1), gb,
                        mlstm_norm_w[o].reshape(MLSTM_HEADS, 1, -1), n_lat, n_ctx)
            ys, w_out = [yc], odd_w_out[o].astype(BF16)
        rows, tm = (n_lat, tm_lat) if last else (lt, tm_full)
        xx = _outproj(ys, w_out, xx, mods, layer, ctx_row, n_lat, rows, tm // 2 if not last else tm)
        xx = _mlp(xx, nw2, mods, layer, ctx_row, mlp_w1[layer].astype(BF16), mlp_w2[layer].astype(BF16),
                  final_norm_w.reshape(1, d), n_lat, rows, tm, 1024, last)
    return xx
```

```python
import functools

import numpy as np
import jax
import jax.numpy as jnp
from jax import lax
from jax.experimental import pallas as pl
from jax.experimental.pallas import tpu as pltpu

F32 = jnp.float32
BF16 = jnp.bfloat16

EPS = 1e-6
GRID_W = 64
N_MOD = 6
HGRN_LOG2_FLOOR = -1e5
C_ROWS = 32

V7X_VMEM_BYTES = 64 * 1024 * 1024
VMEM_LIMIT = V7X_VMEM_BYTES - 8 * 1024 * 1024

SSD_CHUNK = 128
SSD_HEAD_DIM = 64
SSD_STATE = 128
SSD_GROUPS = 2
HGRN_CHUNK = 64
HGRN_HEADS = 8
MLSTM_CHUNK = 128
MLSTM_HEADS = 4
MLSTM_ONES_ROWS = 16
CONV_PIECE = 256
NORM_ROWS = 16


def _cparams(*sem):
    return pltpu.CompilerParams(dimension_semantics=sem, vmem_limit_bytes=VMEM_LIMIT)


def _silu(x):
    return x * jax.nn.sigmoid(x)


def _softplus(x):
    return jnp.maximum(x, 0.0) + jnp.log(1.0 + jnp.exp(-jnp.abs(x)))


def _log_sigmoid(x):
    return -_softplus(-x)


def _dot(a, b):
    return jnp.dot(a, b, preferred_element_type=F32)


def _dot_nt(a, b):
    return lax.dot_general(a, b, (((1,), (1,)), ((), ())), preferred_element_type=F32)


def _dot_tn(a, b):
    return lax.dot_general(a, b, (((0,), (0,)), ((), ())), preferred_element_type=F32)


def _dot01(m01x3, x):
    hi = x.astype(BF16)
    r = x - hi.astype(F32)
    mid = r.astype(BF16)
    lo = (r - mid.astype(F32)).astype(BF16)
    return _dot(m01x3, jnp.concatenate([hi, mid, lo], axis=0))


def _dot01_rows(x, m01x3):
    hi = x.astype(BF16)
    r = x - hi.astype(F32)
    mid = r.astype(BF16)
    lo = (r - mid.astype(F32)).astype(BF16)
    return _dot(jnp.concatenate([hi, mid, lo], axis=1), m01x3)


def _rms(x):
    return x * lax.rsqrt(jnp.mean(x * x, axis=-1, keepdims=True) + EPS)


def _conv3(u, w, b, period):
    rows = u.shape[0]
    t = lax.broadcasted_iota(jnp.int32, u.shape, 0) & (period - 1)
    left = jnp.where(t == 0, 0.0, pltpu.roll(u, 1, axis=0))
    right = jnp.where(t == period - 1, 0.0, pltpu.roll(u, rows - 1, axis=0))
    return left * w[0:1] + u * w[1:2] + right * w[2:3] + b


def _modulated_norm(x_ref, h_ref, nw, sh_l, sc_l, sh_c, sc_c, row0, n_lat):
    a_l, a_c = nw * (1.0 + sc_l), nw * (1.0 + sc_c)

    def block(r, carry):
        start = pl.multiple_of(r * NORM_ROWS, NORM_ROWS)
        rs = pl.ds(start, NORM_ROWS)
        is_ctx = row0 + start >= n_lat
        h_ref[rs, :] = (_rms(x_ref[rs, :]) * jnp.where(is_ctx, a_c, a_l) + jnp.where(is_ctx, sh_c, sh_l)).astype(BF16)
        return carry

    lax.fori_loop(0, x_ref.shape[0] // NORM_ROWS, block, 0, unroll=4)


def _mod_kernel(c_ref, w_ref, b_ref, o_ref):
    a = _silu(c_ref[...]).astype(BF16)
    o_ref[...] = _dot(a, w_ref[...].astype(BF16)) + b_ref[...]


def _modulation(c_all, mod_w, mod_b):
    depth, d, n = mod_w.shape
    tn = 1024
    return pl.pallas_call(
        _mod_kernel,
        grid=(depth, n // tn),
        in_specs=[
            pl.BlockSpec((C_ROWS, d), lambda l, j: (0, 0)),
            pl.BlockSpec((None, d, tn), lambda l, j: (l, 0, j)),
            pl.BlockSpec((None, 1, tn), lambda l, j: (l, 0, j)),
        ],
        out_specs=pl.BlockSpec((None, C_ROWS, tn), lambda l, j: (l, 0, j)),
        out_shape=jax.ShapeDtypeStruct((depth, C_ROWS, n), F32),
        compiler_params=_cparams("parallel", "parallel"),
        name="modulation",
    )(c_all, mod_w, mod_b.reshape(depth, 1, n))


def _mod_specs(layer, ctx_row, ks, d, nargs):
    specs = []
    for k in ks:
        if nargs == 3:
            specs.append(pl.BlockSpec((None, None, None, 1, d), lambda b, i, j, k=k: (layer, b, k, 0, 0)))
            specs.append(pl.BlockSpec((None, None, None, 1, d), lambda b, i, j, k=k: (layer, ctx_row, k, 0, 0)))
        else:
            specs.append(pl.BlockSpec((None, None, None, 1, d), lambda b, i, k=k: (layer, b, k, 0, 0)))
            specs.append(pl.BlockSpec((None, None, None, 1, d), lambda b, i, k=k: (layer, ctx_row, k, 0, 0)))
    return specs


def _inproj_kernel(x_ref, nw_ref, shl_ref, shc_ref, scl_ref, scc_ref, w_ref, ws_ref, o_ref, os_ref, h_ref, *, n_lat):
    @pl.when(pl.program_id(2) == 0)
    def _():
        _modulated_norm(x_ref, h_ref, nw_ref[...], shl_ref[...], scl_ref[...], shc_ref[...], scc_ref[...],
                        pl.program_id(1) * x_ref.shape[0], n_lat)
        os_ref[...] = _dot(h_ref[...], ws_ref[...])

    o_ref[...] = _dot(h_ref[...], w_ref[...]).astype(o_ref.dtype)


def _inproj(x, nw, mods, layer, ctx_row, w_main, w_small, n_lat, tm, tn):
    b, lt, d = x.shape
    n = w_main.shape[1]
    ns = w_small.shape[1]
    return pl.pallas_call(
        functools.partial(_inproj_kernel, n_lat=n_lat),
        grid=(b, lt // tm, n // tn),
        in_specs=[
            pl.BlockSpec((None, tm, d), lambda b, i, j: (b, i, 0)),
            pl.BlockSpec((1, d), lambda b, i, j: (0, 0)),
            *_mod_specs(layer, ctx_row, (0, 1), d, 3),
            pl.BlockSpec((d, tn), lambda b, i, j: (0, j)),
            pl.BlockSpec((d, ns), lambda b, i, j: (0, 0)),
        ],
        out_specs=[
            pl.BlockSpec((None, tm, tn), lambda b, i, j: (b, i, j)),
            pl.BlockSpec((None, tm, ns), lambda b, i, j: (b, i, 0)),
        ],
        out_shape=[jax.ShapeDtypeStruct((b, lt, n), BF16), jax.ShapeDtypeStruct((b, lt, ns), F32)],
        scratch_shapes=[pltpu.VMEM((tm, d), BF16)],
        compiler_params=_cparams("parallel", "parallel", "arbitrary"),
        name="inproj",
    )(x, nw, mods, mods, mods, mods, w_main, w_small)


def _outproj_kernel(*refs, n_y, n_lat):
    y_refs, w_refs = refs[:n_y], refs[n_y:2 * n_y]
    x_ref, gl_ref, gc_ref, o_ref = refs[2 * n_y:]
    acc = _dot(y_refs[0][...], w_refs[0][...])
    for y_ref, w_ref in zip(y_refs[1:], w_refs[1:]):
        acc = acc + _dot(y_ref[...], w_ref[...])
    tm = x_ref.shape[0]
    rows = pl.program_id(1) * tm + lax.broadcasted_iota(jnp.int32, (tm, 1), 0)
    g = jnp.where(rows >= n_lat, gc_ref[...], gl_ref[...])
    o_ref[...] = x_ref[...] + g * acc


def _outproj(ys, w, w_idx, x, mods, layer, ctx_row, n_lat, rows, tm):
    b, lt, d = x.shape
    n_y = len(ys)
    y_specs = [pl.BlockSpec((None, tm, y.shape[2]), lambda b, i: (b, i, 0)) for y in ys]
    w_specs = [pl.BlockSpec((None, y.shape[2], d), lambda b, i, k=k: (w_idx, k, 0)) for k, y in enumerate(ys)]
    return pl.pallas_call(
        functools.partial(_outproj_kernel, n_y=n_y, n_lat=n_lat),
        grid=(b, rows // tm),
        in_specs=[*y_specs, *w_specs,
                  pl.BlockSpec((None, tm, d), lambda b, i: (b, i, 0)),
                  *_mod_specs(layer, ctx_row, (2,), d, 2)],
        out_specs=pl.BlockSpec((None, tm, d), lambda b, i: (b, i, 0)),
        out_shape=jax.ShapeDtypeStruct((b, rows, d), F32),
        compiler_params=_cparams("parallel", "parallel"),
        name="outproj",
    )(*ys, *([w] * n_y), x, mods, mods)


def _mlp_kernel(x_ref, nw_ref, shl_ref, shc_ref, scl_ref, scc_ref, gl_ref, gc_ref, w1_ref, w2_ref, fw_ref,
                o_ref, h_ref, acc_ref, *, n_lat, final_norm):
    f = pl.program_id(2)
    tm = x_ref.shape[0]

    @pl.when(f == 0)
    def _():
        _modulated_norm(x_ref, h_ref, nw_ref[...], shl_ref[...], scl_ref[...], shc_ref[...], scc_ref[...],
                        pl.program_id(1) * tm, n_lat)
        acc_ref[...] = jnp.zeros_like(acc_ref)

    a = jnp.maximum(_dot(h_ref[...], w1_ref[...]), 0.0)
    acc_ref[...] += _dot((a * a).astype(BF16), w2_ref[...])

    @pl.when(f == pl.num_programs(2) - 1)
    def _():
        rows = pl.program_id(1) * tm + lax.broadcasted_iota(jnp.int32, (tm, 1), 0)
        g = jnp.where(rows >= n_lat, gc_ref[...], gl_ref[...])
        y = x_ref[...] + g * acc_ref[...]
        if final_norm:
            y = _rms(y) * fw_ref[...]
        o_ref[...] = y


def _mlp(x, nw, mods, layer, ctx_row, w1, w2, fw, n_lat, rows, tm, tf, final_norm):
    b, _, d = x.shape
    dff = w1.shape[2]
    return pl.pallas_call(
        functools.partial(_mlp_kernel, n_lat=n_lat, final_norm=final_norm),
        grid=(b, rows // tm, dff // tf),
        in_specs=[
            pl.BlockSpec((None, tm, d), lambda b, i, j: (b, i, 0)),
            pl.BlockSpec((1, d), lambda b, i, j: (0, 0)),
            *_mod_specs(layer, ctx_row, (3, 4, 5), d, 3),
            pl.BlockSpec((None, d, tf), lambda b, i, j: (layer, 0, j)),
            pl.BlockSpec((None, tf, d), lambda b, i, j: (layer, j, 0)),
            pl.BlockSpec((1, d), lambda b, i, j: (0, 0)),
        ],
        out_specs=pl.BlockSpec((None, tm, d), lambda b, i, j: (b, i, 0)),
        out_shape=jax.ShapeDtypeStruct((b, rows, d), F32),
        scratch_shapes=[pltpu.VMEM((tm, d), BF16), pltpu.VMEM((tm, d), F32)],
        compiler_params=_cparams("parallel", "parallel", "arbitrary"),
        name="mlp",
    )(x, nw, mods, mods, mods, mods, mods, mods, w1, w2, fw)


def _tri_consts(c):
    lower = np.tril(np.ones((c, c), np.float32))
    tri = np.stack([lower, lower.T])
    return jnp.asarray(np.concatenate([tri, tri, tri], axis=2), BF16)


def _chunk_index(j, d, n_lat_chunks, n_chunks):
    if d == 0:
        c = j + n_lat_chunks
        return jnp.where(c >= n_chunks, c - n_chunks, c)
    return n_chunks - 1 - j


def _conv_rows(n_lat, n_ctx, fn):
    def body(p, carry):
        fn(pl.multiple_of(p * CONV_PIECE, CONV_PIECE), CONV_PIECE, GRID_W)
        return carry
    lax.fori_loop(0, n_lat // CONV_PIECE, body, 0)
    fn(n_lat, n_ctx, n_ctx)


def _ssd_kernel(x_ref, b_ref, c_ref, z_ref, dt_ref, cwx_ref, cwb_ref, cwc_ref, cbx_ref, cbb_ref, cbc_ref,
                dtb_ref, alog_ref, dsk_ref, nw_ref, tri_ref, o_ref,
                xst, bs, cst, dtt, acs, yst, hst, *, n_lat, n_ctx):
    ch = SSD_CHUNK
    p = SSD_HEAD_DIM
    gw = x_ref.shape[1]
    heads = gw // p
    n_chunks = (n_lat + n_ctx) // ch
    n_lat_chunks = n_lat // ch

    def conv_piece(r0, rows, period):
        sl = pl.ds(r0, rows)
        c0 = r0 // ch
        x = _silu(_conv3(x_ref[sl, :].astype(F32), cwx_ref[...], cbx_ref[...], period))
        c = _silu(_conv3(c_ref[sl, :].astype(F32), cwc_ref[...], cbc_ref[...], period))
        bs[sl, :] = _silu(_conv3(b_ref[sl, :].astype(F32), cwb_ref[...], cbb_ref[...], period)).astype(BF16)
        dt = _softplus(dt_ref[sl, :] + dtb_ref[...])
        acs[sl, :] = -jnp.exp(alog_ref[...]) * dt
        for k in range(rows // ch):
            rk = slice(k * ch, (k + 1) * ch)
            for j in range(gw // ch):
                xst[c0 + k, j * ch:(j + 1) * ch, :] = x[rk, j * ch:(j + 1) * ch].T
            cst[c0 + k] = c[rk].T.astype(BF16)
            dtt[c0 + k] = dt[rk].T

    _conv_rows(n_lat, n_ctx, conv_piece)

    si = lax.broadcasted_iota(jnp.int32, (ch, ch), 0)
    ti = lax.broadcasted_iota(jnp.int32, (ch, ch), 1)

    for d in range(2):
        mask = (si <= ti) if d == 0 else (si >= ti)
        hst[...] = jnp.zeros_like(hst)

        def body(j, carry, d=d, mask=mask):
            c = _chunk_index(j, d, n_lat_chunks, n_chunks)
            sl = pl.ds(pl.multiple_of(c * ch, ch), ch)
            cum = _dot01(tri_ref[d], acs[sl, :])
            cum_t = cum.T
            tot_c = cum_t[:, ch - 1:ch] if d == 0 else cum_t[:, 0:1]
            e_tot = jnp.exp(cum[ch - 1:ch, :] if d == 0 else cum[0:1, :])
            dt_t = dtt[c]
            e_cum_t = jnp.exp(cum_t)
            e_end_t = jnp.exp(tot_c - cum_t) * dt_t
            bc = bs[sl, :]
            cc_t = cst[c]
            cb_t = _dot(bc, cc_t)
            for h in range(heads):
                col = d * heads + h
                hs = slice(h * p, (h + 1) * p)
                diff = cum_t[col:col + 1, :] - cum[:, col:col + 1]
                decay = jnp.exp(jnp.where(mask, diff, -jnp.inf))
                xh = xst[c, hs, :]
                state = hst[h]
                y = _dot((xh * dt_t[col:col + 1, :]).astype(BF16), (cb_t * decay).astype(BF16))
                y = y + _dot(state.astype(BF16), cc_t) * e_cum_t[col:col + 1, :]
                xw = (xh * e_end_t[col:col + 1, :]).astype(BF16)
                hst[h] = state * e_tot[:, col:col + 1] + _dot(xw, bc)
                if d == 0:
                    yst[c, hs, :] = y
                else:
                    yst[c, hs, :] = yst[c, hs, :] + y
            if d == 1:
                g_t = yst[c] + xst[c] * dsk_ref[...]
                g = jnp.concatenate([g_t[j * ch:(j + 1) * ch].T for j in range(gw // ch)], axis=1)
                g = g * _silu(z_ref[sl, :].astype(F32))
                o_ref[sl, :] = (_rms(g) * nw_ref[...]).astype(BF16)
            return carry

        lax.fori_loop(0, n_chunks, body, 0, unroll=2)


def _ssd(u, us, conv_w, conv_b, dtb, alog, dsk, nw, n_lat, n_ctx):
    b, lt, _ = u.shape
    gw = SSD_HEAD_DIM * 8
    ns = SSD_STATE
    nc = lt // SSD_CHUNK
    x_blk, b_blk, c_blk = 1024 // gw, 2048 // ns, 2304 // ns
    cx_blk, cb_blk, cc_blk = 0, 1024 // ns, 1280 // ns
    seq = lambda w, off: pl.BlockSpec((None, lt, w), lambda b, g, off=off: (b, 0, off + g))
    par = lambda r, w, off: pl.BlockSpec((r, w), lambda b, g, off=off: (0, off + g))
    grp = lambda w: pl.BlockSpec((None, 1, w), lambda b, g: (g, 0, 0))
    return pl.pallas_call(
        functools.partial(_ssd_kernel, n_lat=n_lat, n_ctx=n_ctx),
        grid=(b, SSD_GROUPS),
        in_specs=[
            seq(gw, x_blk), seq(ns, b_blk), seq(ns, c_blk), seq(gw, 0),
            pl.BlockSpec((None, lt, 128), lambda b, g: (b, 0, g)),
            par(3, gw, cx_blk), par(3, ns, cb_blk), par(3, ns, cc_blk),
            par(1, gw, cx_blk), par(1, ns, cb_blk), par(1, ns, cc_blk),
            grp(128), grp(128), pl.BlockSpec((None, gw, 128), lambda b, g: (g, 0, 0)), grp(gw),
            pl.BlockSpec((2, SSD_CHUNK, 3 * SSD_CHUNK), lambda b, g: (0, 0, 0)),
        ],
        out_specs=pl.BlockSpec((None, lt, gw), lambda b, g: (b, 0, g)),
        out_shape=jax.ShapeDtypeStruct((b, lt, SSD_GROUPS * gw), BF16),
        scratch_shapes=[
            pltpu.VMEM((nc, gw, SSD_CHUNK), F32), pltpu.VMEM((lt, ns), BF16), pltpu.VMEM((nc, ns, SSD_CHUNK), BF16),
            pltpu.VMEM((nc, 128, SSD_CHUNK), F32), pltpu.VMEM((lt, 128), F32), pltpu.VMEM((nc, gw, SSD_CHUNK), F32),
            pltpu.VMEM((gw // SSD_HEAD_DIM, SSD_HEAD_DIM, ns), F32),
        ],
        compiler_params=_cparams("parallel", "parallel"),
        name="ssd",
    )(u, u, u, u, us, conv_w, conv_w, conv_w, conv_b, conv_b, conv_b, dtb, alog, dsk, nw, _tri_consts(SSD_CHUNK))


_HGRN_LEVELS = (32, 16, 8, 4, 2, 1)


def _hgrn_consts():
    c = HGRN_CHUNK
    sums = np.zeros((7, c, c), np.float32)
    pairs = np.zeros((7, c, c), np.float32)
    for li, m in enumerate(_HGRN_LEVELS):
        for t in range(c):
            beta = (t // (2 * m)) * 2 * m
            mid = beta + m
            if t >= mid:
                sums[li, t, mid:t + 1] = 1.0
                pairs[li, t, beta:mid] = 1.0
            else:
                sums[li, t, t + 1:mid] = 1.0
    sums[6] = np.tril(np.ones((c, c), np.float32))
    pairs[6] = np.eye(c, dtype=np.float32)
    sums = np.stack([sums, sums[:, ::-1, ::-1]]).reshape(2, 7 * c, c)
    sums = np.concatenate([sums, sums, sums, np.zeros_like(sums)], axis=2)
    pairs = np.stack([pairs, pairs[:, ::-1, ::-1]])
    zero = np.zeros_like(pairs)
    pairs = np.concatenate([np.concatenate([pairs, zero], axis=3), np.concatenate([zero, pairs], axis=3)], axis=2)
    return jnp.asarray(sums, BF16), jnp.asarray(pairs, F32)


def _hgrn_kernel(q_ref, ff_ref, fb_ref, i_ref, g_ref, lb_ref, nw_ref, sums_ref, pairs_ref, o_ref,
                 ys, qb_s, *dir_scratch, n_lat, n_ctx):
    w_s, kb_s, qd_s, kd_s, et_s, att_s, p_s = zip(dir_scratch[:7], dir_scratch[7:])
    ch = HGRN_CHUNK
    n_chunks = (n_lat + n_ctx) // ch
    n_lat_chunks = n_lat // ch
    n_pairs = n_chunks // 2
    n_lv = len(_HGRN_LEVELS)
    lb = lb_ref[...]
    one_m_lb = 1.0 - lb
    dk = q_ref.shape[1]
    dv = i_ref.shape[1]
    t_idx = lax.broadcasted_iota(jnp.int32, (ch, dk), 0)

    laters = [[((t_idx & m) != 0) == (d == 0) for m in _HGRN_LEVELS] for d in range(2)]

    def operands(d, pi):
        f_ref = ff_ref if d == 0 else fb_ref
        r0 = pl.multiple_of(pi * 2 * ch, 2 * ch)
        sl = pl.ds(r0, 2 * ch)
        gate = jax.nn.sigmoid(f_ref[sl, :].astype(F32))
        kin = one_m_lb * (1.0 - gate)
        logf2 = jnp.maximum(jnp.log2(lb + one_m_lb * gate), HGRN_LOG2_FLOOR)
        q = _silu(q_ref[sl, :].astype(F32))
        if d == 0:
            qb_s[sl, :] = q.astype(BF16)
        kb_s[d][sl, :] = kin.astype(BF16)
        lf = jnp.concatenate([logf2[:ch], logf2[ch:]], axis=1)
        hi = lf.astype(BF16)
        r1 = lf - hi.astype(F32)
        mid = r1.astype(BF16)
        lo = (r1 - mid.astype(F32)).astype(BF16)
        rel2 = _dot(sums_ref[d], jnp.concatenate([hi, mid, lo, jnp.zeros_like(hi)], axis=0))
        for half in range(2):
            rows = slice(half * ch, (half + 1) * ch)
            rel = rel2[:, half * dk:(half + 1) * dk]
            qh, kh = q[rows], kin[rows]
            c = pi * 2 + half
            for li in range(n_lv):
                e = jnp.exp2(rel[li * ch:(li + 1) * ch])
                w_s[d][pi, li, rows, :] = (jnp.where(laters[d][li], qh, kh) * e).astype(BF16)
            bcum = rel[n_lv * ch:(n_lv + 1) * ch]
            tot = bcum[ch - 1:ch] if d == 0 else bcum[0:1]
            hs = pl.ds(r0 + half * ch, ch)
            qd_s[d][hs, :] = (qh * jnp.exp2(bcum)).astype(BF16)
            kd_s[d][hs, :] = (kh * jnp.exp2(tot - bcum)).astype(BF16)
            et_s[d][c] = jnp.broadcast_to(jnp.exp2(tot), (8, dk))

    def intra(d, pi):
        sl = pl.ds(pl.multiple_of(pi * 2 * ch, 2 * ch), 2 * ch)
        att = pairs_ref[d, n_lv] * _dot_nt(qb_s[sl, :], kb_s[d][sl, :])
        for li in range(n_lv):
            w = w_s[d][pi, li]
            att = att + pairs_ref[d, li] * _dot_nt(w, w)
        att_s[d][sl, :] = att.astype(BF16)
        for half in range(2):
            hs = pl.ds(pl.multiple_of(pi * 2 * ch, 2 * ch) + half * ch, ch)
            p_s[d][pi * 2 + half] = _dot_tn(i_ref[hs, :], kd_s[d][hs, :])

    def scan(d, j, state_t):
        pi = _chunk_index(j, d, n_lat_chunks // 2, n_pairs)
        r0 = pl.multiple_of(pi * 2 * ch, 2 * ch)
        sl = pl.ds(r0, 2 * ch)
        o_intra = _dot(att_s[d][sl, :], i_ref[sl, :])
        o_halves = [None, None]
        for half in ((0, 1) if d == 0 else (1, 0)):
            hs = pl.ds(r0 + half * ch, ch)
            o_halves[half] = (o_intra[half * ch:(half + 1) * ch]
                              + _dot_nt(qd_s[d][hs, :], state_t.astype(BF16)))
            c = pi * 2 + half
            state_t = state_t * et_s[d][c][0:1] + p_s[d][c]
        o = jnp.concatenate(o_halves, axis=0)
        if d == 0:
            ys[sl, :] = o
        else:
            o = ys[sl, :] + o
            o_ref[sl, :] = (_rms(o) * nw_ref[...] * _silu(g_ref[sl, :].astype(F32))).astype(BF16)
        return state_t

    zero_state = jnp.zeros((dv, dk), F32)

    def stage_a(pi, carry):
        operands(0, pi)
        return carry

    def stage_b(pi, carry):
        intra(0, pi)
        operands(1, pi)
        return carry

    def stage_c(j, state_t):
        intra(1, j)
        return scan(0, j, state_t)

    lax.fori_loop(0, n_pairs, stage_a, 0, unroll=2)
    lax.fori_loop(0, n_pairs, stage_b, 0, unroll=2)
    lax.fori_loop(0, n_pairs, stage_c, zero_state, unroll=3)
    lax.fori_loop(0, n_pairs, functools.partial(scan, 1), zero_state, unroll=3)


def _hgrn(u, lb, nw, n_lat, n_ctx):
    b, lt, _ = u.shape
    w = 128
    ch = HGRN_CHUNK
    n_chunks = lt // ch
    sums, pairs = _hgrn_consts()
    seq = lambda off: pl.BlockSpec((None, lt, w), lambda b, h, off=off: (b, 0, off + h))
    head = pl.BlockSpec((None, 1, w), lambda b, h: (h, 0, 0))
    return pl.pallas_call(
        functools.partial(_hgrn_kernel, n_lat=n_lat, n_ctx=n_ctx),
        grid=(b, HGRN_HEADS),
        in_specs=[seq(2560 // w), seq(3584 // w), seq(4608 // w), seq(5632 // w), seq(6656 // w), head, head,
                  pl.BlockSpec(sums.shape, lambda b, h: (0, 0, 0)),
                  pl.BlockSpec(pairs.shape, lambda b, h: (0, 0, 0, 0))],
        out_specs=pl.BlockSpec((None, lt, w), lambda b, h: (b, 0, h)),
        out_shape=jax.ShapeDtypeStruct((b, lt, HGRN_HEADS * w), BF16),
        scratch_shapes=[
            pltpu.VMEM((lt, w), F32), pltpu.VMEM((lt, w), BF16),
            *([pltpu.VMEM((n_chunks // 2, len(_HGRN_LEVELS), 2 * ch, w), BF16),
               pltpu.VMEM((lt, w), BF16), pltpu.VMEM((lt, w), BF16), pltpu.VMEM((lt, w), BF16),
               pltpu.VMEM((n_chunks, 8, w), F32), pltpu.VMEM((lt, 2 * ch), BF16),
               pltpu.VMEM((n_chunks, w, w), F32)] * 2),
        ],
        compiler_params=_cparams("parallel", "parallel"),
        name="hgrn2",
    )(u, u, u, u, u, lb, nw, sums, pairs)


def _mlstm_kernel(q_ref, k_ref, v_ref, og_ref, gt_ref, cwq_ref, cwk_ref, cbq_ref, cbk_ref, gb_ref, nw_ref, tri_ref,
                  o_ref, qst, ks, vst, gst, yst, cst, *, n_lat, n_ctx):
    ch = MLSTM_CHUNK
    n_chunks = (n_lat + n_ctx) // ch
    n_lat_chunks = n_lat // ch
    dqk = q_ref.shape[1]
    dv = v_ref.shape[1]
    k_scale = dqk ** -0.5

    def conv_piece(r0, rows, period):
        sl = pl.ds(r0, rows)
        c0 = r0 // ch
        q = _silu(_conv3(q_ref[sl, :].astype(F32), cwq_ref[...], cbq_ref[...], period))
        ks[sl, :] = (_silu(_conv3(k_ref[sl, :].astype(F32), cwk_ref[...], cbk_ref[...], period)) * k_scale).astype(BF16)
        raw = gt_ref[sl, :] + gb_ref[...]
        lane = lax.broadcasted_iota(jnp.int32, raw.shape, 1)
        g = jnp.where(lane < 2, raw, _log_sigmoid(raw))
        v = v_ref[sl, :].astype(F32)
        for k in range(rows // ch):
            rk = slice(k * ch, (k + 1) * ch)
            for j in range(dqk // ch):
                qst[c0 + k, j * ch:(j + 1) * ch, :] = q[rk, j * ch:(j + 1) * ch].T.astype(BF16)
            for j in range(dv // ch):
                vst[c0 + k, j * ch:(j + 1) * ch, :] = v[rk, j * ch:(j + 1) * ch].T.astype(BF16)
            vst[c0 + k, dv:, :] = jnp.ones((MLSTM_ONES_ROWS, ch), BF16)
            gst[c0 + k] = g[rk].T

    _conv_rows(n_lat, n_ctx, conv_piece)

    si = lax.broadcasted_iota(jnp.int32, (ch, ch), 0)
    ti = lax.broadcasted_iota(jnp.int32, (ch, ch), 1)

    for d in range(2):
        mask = (si <= ti) if d == 0 else (si >= ti)
        cst[...] = jnp.zeros_like(cst)

        def body(j, m_prev, d=d, mask=mask):
            c = _chunk_index(j, d, n_lat_chunks, n_chunks)
            sl = pl.ds(pl.multiple_of(c * ch, ch), ch)
            g_t = gst[c]
            cum_t = _dot01_rows(g_t, tri_ref[d])
            brow = cum_t[2 + d:3 + d, :]
            irow = g_t[d:d + 1, :]
            tot = brow[:, ch - 1:ch] if d == 0 else brow[:, 0:1]
            logd = jnp.where(mask, brow + jnp.broadcast_to(irow - brow, (ch, ch)).T, -jnp.inf)
            gstate = brow + m_prev
            mt = jnp.maximum(jnp.max(logd, axis=0, keepdims=True), gstate)
            q_t = qst[c]
            kc = ks[sl, :]
            w = _dot(kc, q_t) * jnp.exp(logd - mt)
            sw = jnp.exp(gstate - mt)
            v_t = vst[c]
            state = cst[...]
            qstate = _dot(state.astype(BF16), q_t)
            num = _dot(v_t[:dv], w.astype(BF16)) + sw * qstate[:dv]
            den = jnp.sum(w, axis=0, keepdims=True) + sw * qstate[dv:dv + 1]
            hout = num * (1.0 / jnp.maximum(jnp.abs(den), jnp.exp(-mt)))
            logw = tot - brow + irow
            m_new = jnp.maximum(tot + m_prev, jnp.max(logw, axis=1, keepdims=True))
            ws = jnp.exp(logw - m_new).astype(BF16)
            cst[...] = jnp.exp(tot + m_prev - m_new) * state + _dot(v_t * ws, kc)
            if d == 0:
                yst[c] = hout
            else:
                hh_t = yst[c] + hout
                hh = jnp.concatenate([hh_t[i * ch:(i + 1) * ch].T for i in range(dv // ch)], axis=1)
                o_ref[sl, :] = (_rms(hh) * nw_ref[...] * jax.nn.sigmoid(og_ref[sl, :].astype(F32))).astype(BF16)
            return m_new

        lax.fori_loop(0, n_chunks, body, jnp.zeros((1, 1), F32), unroll=2)


def _mlstm(u, us, conv_w, conv_b, gate_b, nw, n_lat, n_ctx):
    b, lt, _ = u.shape
    dqk, dv = 256, 512
    nh = MLSTM_HEADS
    ch = MLSTM_CHUNK
    nc = lt // ch
    upper = np.triu(np.ones((ch, ch), np.float32))
    tri = np.stack([upper, upper.T])
    tri = jnp.asarray(np.concatenate([tri, tri, tri], axis=1), BF16)
    seq = lambda w, off: pl.BlockSpec((None, lt, w), lambda b, h, off=off: (b, 0, off + h))
    par = lambda r, off: pl.BlockSpec((r, dqk), lambda b, h, off=off: (0, off + h))
    return pl.pallas_call(
        functools.partial(_mlstm_kernel, n_lat=n_lat, n_ctx=n_ctx),
        grid=(b, nh),
        in_specs=[
            seq(dqk, 0), seq(dqk, nh), seq(dv, 2048 // dv), seq(dv, 4096 // dv), seq(128, 0),
            par(3, 0), par(3, nh), par(1, 0), par(1, nh),
            pl.BlockSpec((None, 1, 128), lambda b, h: (h, 0, 0)),
            pl.BlockSpec((None, 1, dv), lambda b, h: (h, 0, 0)),
            pl.BlockSpec(tri.shape, lambda b, h: (0, 0, 0)),
        ],
        out_specs=pl.BlockSpec((None, lt, dv), lambda b, h: (b, 0, h)),
        out_shape=jax.ShapeDtypeStruct((b, lt, nh * dv), BF16),
        scratch_shapes=[
            pltpu.VMEM((nc, dqk, ch), BF16), pltpu.VMEM((lt, dqk), BF16),
            pltpu.VMEM((nc, dv + MLSTM_ONES_ROWS, ch), BF16), pltpu.VMEM((nc, 128, ch), F32),
            pltpu.VMEM((nc, dv, ch), F32), pltpu.VMEM((dv + MLSTM_ONES_ROWS, dqk), F32),
        ],
        compiler_params=_cparams("parallel", "parallel"),
        name="mlstm",
    )(u, u, u, u, us, conv_w, conv_w, conv_b, conv_b, gate_b, nw, tri)


def _pad_lanes(a, width=128):
    return jnp.pad(a, [(0, 0)] * (a.ndim - 1) + [(0, width - a.shape[-1])])


def _even_params(w_in, dt_bias, a_log, d_skip):
    heads = a_log.shape[1]
    hg = heads // SSD_GROUPS
    w_main = jnp.concatenate([w_in[:, :2560], w_in[:, 2560 + 2 * heads:]], axis=1).astype(BF16)
    dt_w = w_in[:, 2560:2560 + 2 * heads]
    per_group = lambda a: [_pad_lanes(jnp.concatenate([a[..., g * hg:(g + 1) * hg], a[..., heads + g * hg:heads + (g + 1) * hg]], axis=-1))
                           for g in range(SSD_GROUPS)]
    w_small = jnp.concatenate(per_group(dt_w), axis=1).astype(BF16)
    flat = lambda a: a.reshape(1, 2 * heads)
    dtb = jnp.stack(per_group(flat(dt_bias)))
    alog = jnp.stack(per_group(flat(a_log)))
    dsk = jnp.broadcast_to(jnp.repeat(d_skip, SSD_HEAD_DIM).reshape(SSD_GROUPS, hg * SSD_HEAD_DIM, 1),
                           (SSD_GROUPS, hg * SSD_HEAD_DIM, 128))
    return w_main, w_small, dtb, alog, dsk


def _odd_params(w_in, gate_b):
    nh = MLSTM_HEADS
    w_main = w_in[:, :6144].astype(BF16)
    gw = w_in[:, 6144:]
    w_small = jnp.concatenate([_pad_lanes(gw[:, h::nh]) for h in range(nh)], axis=1).astype(BF16)
    gb = jnp.stack([_pad_lanes(gate_b[:, h].reshape(1, 4)) for h in range(nh)])
    return w_main, w_small, gb


def kernel(x, c, ctx, c_ctx, mod_w, mod_b, norm_w, final_norm_w, mlp_w1, mlp_w2, even_w_in, even_w_out, ssd_conv_w, ssd_conv_b, ssd_a_log, ssd_dt_bias, ssd_d, ssd_norm_w, hgrn_lb, hgrn_norm_w, odd_w_in, odd_w_out, mlstm_conv_w, mlstm_conv_b, mlstm_gate_b, mlstm_norm_w):
    bsz, n_lat, d = x.shape
    n_ctx = ctx.shape[1]
    depth = mod_w.shape[0]
    lt = n_lat + n_ctx
    assert bsz < C_ROWS and n_lat % CONV_PIECE == 0 and n_ctx % SSD_CHUNK == 0 and n_ctx & (n_ctx - 1) == 0
    ctx_row = bsz

    c_all = jnp.zeros((C_ROWS, d), F32).at[:bsz].set(c).at[ctx_row].set(c_ctx)
    mods = _modulation(c_all, mod_w, mod_b).reshape(depth, C_ROWS, N_MOD, 1, d)

    lb_all = jnp.cumsum(jax.nn.softmax(hgrn_lb.astype(F32), axis=0), axis=0)
    lb_all = lb_all - lb_all[0]

    tm_in = lt // 2
    tm_full = lt // 4
    tm_lat = min(512, n_lat)

    mlp_w1_b, mlp_w2_b = mlp_w1.astype(BF16), mlp_w2.astype(BF16)
    even_w_out_b, odd_w_out_b = even_w_out.astype(BF16), odd_w_out.astype(BF16)

    xx = jnp.concatenate([x, ctx], axis=1)
    for layer in range(depth):
        last = layer == depth - 1
        nw1 = norm_w[layer, 0].reshape(1, d)
        nw2 = norm_w[layer, 1].reshape(1, d)
        if layer % 2 == 0:
            e = layer // 2
            w_main, w_small, dtb, alog, dsk = _even_params(even_w_in[e], ssd_dt_bias[e], ssd_a_log[e], ssd_d[e])
            u, us = _inproj(xx, nw1, mods, layer, ctx_row, w_main, w_small, n_lat, tm_in, 1536)
            ya = _ssd(u, us, ssd_conv_w[e], ssd_conv_b[e].reshape(1, -1), dtb, alog, dsk,
                      ssd_norm_w[e].reshape(SSD_GROUPS, 1, -1), n_lat, n_ctx)
            yb = _hgrn(u, lb_all[e].reshape(HGRN_HEADS, 1, -1), hgrn_norm_w[e].reshape(HGRN_HEADS, 1, -1), n_lat, n_ctx)
            ys, w_out, w_idx = [ya, yb], even_w_out_b, e
        else:
            o = layer // 2
            w_main, w_small, gb = _odd_params(odd_w_in[o], mlstm_gate_b[o])
            u, us = _inproj(xx, nw1, mods, layer, ctx_row, w_main, w_small, n_lat, tm_in, 1024)
            yc = _mlstm(u, us, mlstm_conv_w[o], mlstm_conv_b[o].reshape(1, -1), gb,
                        mlstm_norm_w[o].reshape(MLSTM_HEADS, 1, -1), n_lat, n_ctx)
            ys, w_out, w_idx = [yc], odd_w_out_b, o
        rows, tm = (n_lat, tm_lat) if last else (lt, tm_full)
        xx = _outproj(ys, w_out, w_idx, xx, mods, layer, ctx_row, n_lat, rows, tm // 2 if not last else tm)
        xx = _mlp(xx, nw2, mods, layer, ctx_row, mlp_w1_b, mlp_w2_b,
                  final_norm_w.reshape(1, d), n_lat, rows, tm, 1024, last)
    return xx
```

```python
import functools

import numpy as np
import jax
import jax.numpy as jnp
from jax import lax
from jax.experimental import pallas as pl
from jax.experimental.pallas import tpu as pltpu

F32 = jnp.float32
BF16 = jnp.bfloat16

EPS = 1e-6
GRID_W = 64
N_MOD = 6
HGRN_LOG2_FLOOR = -1e5
C_ROWS = 32

V7X_VMEM_BYTES = 64 * 1024 * 1024
VMEM_LIMIT = V7X_VMEM_BYTES - 8 * 1024 * 1024

SSD_CHUNK = 128
SSD_HEAD_DIM = 64
SSD_STATE = 128
SSD_GROUPS = 2
HGRN_CHUNK = 64
HGRN_HEADS = 8
MLSTM_CHUNK = 128
MLSTM_HEADS = 4
MLSTM_ONES_ROWS = 16
CONV_PIECE = 256
NORM_ROWS = 16


def _cparams(*sem):
    return pltpu.CompilerParams(dimension_semantics=sem, vmem_limit_bytes=VMEM_LIMIT)


def _silu(x):
    return x * jax.nn.sigmoid(x)


def _softplus(x):
    return jnp.maximum(x, 0.0) + jnp.log(1.0 + jnp.exp(-jnp.abs(x)))


def _log_sigmoid(x):
    return -_softplus(-x)


def _dot(a, b):
    return jnp.dot(a, b, preferred_element_type=F32)


def _dot_nt(a, b):
    return lax.dot_general(a, b, (((1,), (1,)), ((), ())), preferred_element_type=F32)


def _dot_tn(a, b):
    return lax.dot_general(a, b, (((0,), (0,)), ((), ())), preferred_element_type=F32)


def _dot01(m01x3, x):
    hi = x.astype(BF16)
    r = x - hi.astype(F32)
    mid = r.astype(BF16)
    lo = (r - mid.astype(F32)).astype(BF16)
    return _dot(m01x3, jnp.concatenate([hi, mid, lo], axis=0))


def _dot01_rows(x, m01x3):
    hi = x.astype(BF16)
    r = x - hi.astype(F32)
    mid = r.astype(BF16)
    lo = (r - mid.astype(F32)).astype(BF16)
    return _dot(jnp.concatenate([hi, mid, lo], axis=1), m01x3)


def _rms(x):
    return x * lax.rsqrt(jnp.mean(x * x, axis=-1, keepdims=True) + EPS)


def _conv3(u, w, b, period):
    rows = u.shape[0]
    t = lax.broadcasted_iota(jnp.int32, u.shape, 0) & (period - 1)
    left = jnp.where(t == 0, 0.0, pltpu.roll(u, 1, axis=0))
    right = jnp.where(t == period - 1, 0.0, pltpu.roll(u, rows - 1, axis=0))
    return left * w[0:1] + u * w[1:2] + right * w[2:3] + b


def _modulated_norm(x_ref, h_ref, nw, sh_l, sc_l, sh_c, sc_c, row0, n_lat):
    a_l, a_c = nw * (1.0 + sc_l), nw * (1.0 + sc_c)

    def block(r, carry):
        start = pl.multiple_of(r * NORM_ROWS, NORM_ROWS)
        rs = pl.ds(start, NORM_ROWS)
        is_ctx = row0 + start >= n_lat
        h_ref[rs, :] = (_rms(x_ref[rs, :]) * jnp.where(is_ctx, a_c, a_l) + jnp.where(is_ctx, sh_c, sh_l)).astype(BF16)
        return carry

    lax.fori_loop(0, x_ref.shape[0] // NORM_ROWS, block, 0, unroll=4)


def _mod_kernel(c_ref, w_ref, b_ref, o_ref):
    a = _silu(c_ref[...]).astype(BF16)
    o_ref[...] = _dot(a, w_ref[...].astype(BF16)) + b_ref[...]


def _modulation(c_all, mod_w, mod_b):
    depth, d, n = mod_w.shape
    tn = 1024
    return pl.pallas_call(
        _mod_kernel,
        grid=(depth, n // tn),
        in_specs=[
            pl.BlockSpec((C_ROWS, d), lambda l, j: (0, 0)),
            pl.BlockSpec((None, d, tn), lambda l, j: (l, 0, j)),
            pl.BlockSpec((None, 1, tn), lambda l, j: (l, 0, j)),
        ],
        out_specs=pl.BlockSpec((None, C_ROWS, tn), lambda l, j: (l, 0, j)),
        out_shape=jax.ShapeDtypeStruct((depth, C_ROWS, n), F32),
        compiler_params=_cparams("parallel", "parallel"),
        name="modulation",
    )(c_all, mod_w, mod_b.reshape(depth, 1, n))


def _mod_specs(layer, ctx_row, ks, d, nargs):
    specs = []
    for k in ks:
        if nargs == 3:
            specs.append(pl.BlockSpec((None, None, None, 1, d), lambda b, i, j, k=k: (layer, b, k, 0, 0)))
            specs.append(pl.BlockSpec((None, None, None, 1, d), lambda b, i, j, k=k: (layer, ctx_row, k, 0, 0)))
        else:
            specs.append(pl.BlockSpec((None, None, None, 1, d), lambda b, i, k=k: (layer, b, k, 0, 0)))
            specs.append(pl.BlockSpec((None, None, None, 1, d), lambda b, i, k=k: (layer, ctx_row, k, 0, 0)))
    return specs


def _inproj_kernel(x_ref, nw_ref, shl_ref, shc_ref, scl_ref, scc_ref, w_ref, ws_ref, o_ref, os_ref, h_ref, *, n_lat):
    @pl.when(pl.program_id(2) == 0)
    def _():
        _modulated_norm(x_ref, h_ref, nw_ref[...], shl_ref[...], scl_ref[...], shc_ref[...], scc_ref[...],
                        pl.program_id(1) * x_ref.shape[0], n_lat)
        os_ref[...] = _dot(h_ref[...], ws_ref[...])

    o_ref[...] = _dot(h_ref[...], w_ref[...]).astype(o_ref.dtype)


def _inproj(x, nw, mods, layer, ctx_row, w_main, w_small, n_lat, tm, tn):
    b, lt, d = x.shape
    n = w_main.shape[1]
    ns = w_small.shape[1]
    return pl.pallas_call(
        functools.partial(_inproj_kernel, n_lat=n_lat),
        grid=(b, lt // tm, n // tn),
        in_specs=[
            pl.BlockSpec((None, tm, d), lambda b, i, j: (b, i, 0)),
            pl.BlockSpec((1, d), lambda b, i, j: (0, 0)),
            *_mod_specs(layer, ctx_row, (0, 1), d, 3),
            pl.BlockSpec((d, tn), lambda b, i, j: (0, j)),
            pl.BlockSpec((d, ns), lambda b, i, j: (0, 0)),
        ],
        out_specs=[
            pl.BlockSpec((None, tm, tn), lambda b, i, j: (b, i, j)),
            pl.BlockSpec((None, tm, ns), lambda b, i, j: (b, i, 0)),
        ],
        out_shape=[jax.ShapeDtypeStruct((b, lt, n), BF16), jax.ShapeDtypeStruct((b, lt, ns), F32)],
        scratch_shapes=[pltpu.VMEM((tm, d), BF16)],
        compiler_params=_cparams("parallel", "parallel", "arbitrary"),
        name="inproj",
    )(x, nw, mods, mods, mods, mods, w_main, w_small)


def _outproj_kernel(*refs, n_y, n_lat):
    y_refs, w_refs = refs[:n_y], refs[n_y:2 * n_y]
    x_ref, gl_ref, gc_ref, o_ref = refs[2 * n_y:]
    acc = _dot(y_refs[0][...], w_refs[0][...])
    for y_ref, w_ref in zip(y_refs[1:], w_refs[1:]):
        acc = acc + _dot(y_ref[...], w_ref[...])
    tm = x_ref.shape[0]
    rows = pl.program_id(1) * tm + lax.broadcasted_iota(jnp.int32, (tm, 1), 0)
    g = jnp.where(rows >= n_lat, gc_ref[...], gl_ref[...])
    o_ref[...] = x_ref[...] + g * acc


def _outproj(ys, w, w_idx, x, mods, layer, ctx_row, n_lat, rows, tm):
    b, lt, d = x.shape
    n_y = len(ys)
    y_specs = [pl.BlockSpec((None, tm, y.shape[2]), lambda b, i: (b, i, 0)) for y in ys]
    w_specs = [pl.BlockSpec((None, y.shape[2], d), lambda b, i, k=k: (w_idx, k, 0)) for k, y in enumerate(ys)]
    return pl.pallas_call(
        functools.partial(_outproj_kernel, n_y=n_y, n_lat=n_lat),
        grid=(b, rows // tm),
        in_specs=[*y_specs, *w_specs,
                  pl.BlockSpec((None, tm, d), lambda b, i: (b, i, 0)),
                  *_mod_specs(layer, ctx_row, (2,), d, 2)],
        out_specs=pl.BlockSpec((None, tm, d), lambda b, i: (b, i, 0)),
        out_shape=jax.ShapeDtypeStruct((b, rows, d), F32),
        compiler_params=_cparams("parallel", "parallel"),
        name="outproj",
    )(*ys, *([w] * n_y), x, mods, mods)


def _mlp_kernel(x_ref, nw_ref, shl_ref, shc_ref, scl_ref, scc_ref, gl_ref, gc_ref, w1_ref, w2_ref, fw_ref,
                o_ref, h_ref, acc_ref, *, n_lat, final_norm):
    f = pl.program_id(2)
    tm = x_ref.shape[0]

    @pl.when(f == 0)
    def _():
        _modulated_norm(x_ref, h_ref, nw_ref[...], shl_ref[...], scl_ref[...], shc_ref[...], scc_ref[...],
                        pl.program_id(1) * tm, n_lat)
        acc_ref[...] = jnp.zeros_like(acc_ref)

    a = jnp.maximum(_dot(h_ref[...], w1_ref[...]), 0.0)
    acc_ref[...] += _dot((a * a).astype(BF16), w2_ref[...])

    @pl.when(f == pl.num_programs(2) - 1)
    def _():
        rows = pl.program_id(1) * tm + lax.broadcasted_iota(jnp.int32, (tm, 1), 0)
        g = jnp.where(rows >= n_lat, gc_ref[...], gl_ref[...])
        y = x_ref[...] + g * acc_ref[...]
        if final_norm:
            y = _rms(y) * fw_ref[...]
        o_ref[...] = y


def _mlp(x, nw, mods, layer, ctx_row, w1, w2, fw, n_lat, rows, tm, tf, final_norm):
    b, _, d = x.shape
    dff = w1.shape[2]
    return pl.pallas_call(
        functools.partial(_mlp_kernel, n_lat=n_lat, final_norm=final_norm),
        grid=(b, rows // tm, dff // tf),
        in_specs=[
            pl.BlockSpec((None, tm, d), lambda b, i, j: (b, i, 0)),
            pl.BlockSpec((1, d), lambda b, i, j: (0, 0)),
            *_mod_specs(layer, ctx_row, (3, 4, 5), d, 3),
            pl.BlockSpec((None, d, tf), lambda b, i, j: (layer, 0, j)),
            pl.BlockSpec((None, tf, d), lambda b, i, j: (layer, j, 0)),
            pl.BlockSpec((1, d), lambda b, i, j: (0, 0)),
        ],
        out_specs=pl.BlockSpec((None, tm, d), lambda b, i, j: (b, i, 0)),
        out_shape=jax.ShapeDtypeStruct((b, rows, d), F32),
        scratch_shapes=[pltpu.VMEM((tm, d), BF16), pltpu.VMEM((tm, d), F32)],
        compiler_params=_cparams("parallel", "parallel", "arbitrary"),
        name="mlp",
    )(x, nw, mods, mods, mods, mods, mods, mods, w1, w2, fw)


def _tri_consts(c):
    lower = np.tril(np.ones((c, c), np.float32))
    tri = np.stack([lower, lower.T])
    return jnp.asarray(np.concatenate([tri, tri, tri], axis=2), BF16)


def _chunk_index(j, d, n_lat_chunks, n_chunks):
    if d == 0:
        c = j + n_lat_chunks
        return jnp.where(c >= n_chunks, c - n_chunks, c)
    return n_chunks - 1 - j


def _conv_rows(n_lat, n_ctx, fn):
    def body(p, carry):
        fn(pl.multiple_of(p * CONV_PIECE, CONV_PIECE), CONV_PIECE, GRID_W)
        return carry
    lax.fori_loop(0, n_lat // CONV_PIECE, body, 0)
    fn(n_lat, n_ctx, n_ctx)


def _ssd_kernel(u_ref, dt_ref, cwx_ref, cwb_ref, cwc_ref, cbx_ref, cbb_ref, cbc_ref,
                dtb_ref, alog_ref, dsk_ref, nw_ref, tri_ref, o_ref,
                xst, bs, cst, dtt, acs, yst, hst, *, n_lat, n_ctx):
    ch = SSD_CHUNK
    p = SSD_HEAD_DIM
    gw = cwx_ref.shape[1]
    ns = cwb_ref.shape[1]
    heads = gw // p
    x_ref, b_ref = u_ref.at[:, 0:gw], u_ref.at[:, gw:gw + ns]
    c_ref, z_ref = u_ref.at[:, gw + ns:gw + 2 * ns], u_ref.at[:, gw + 2 * ns:]
    n_chunks = (n_lat + n_ctx) // ch
    n_lat_chunks = n_lat // ch

    def conv_piece(r0, rows, period):
        sl = pl.ds(r0, rows)
        c0 = r0 // ch
        x = _silu(_conv3(x_ref[sl, :].astype(F32), cwx_ref[...], cbx_ref[...], period))
        c = _silu(_conv3(c_ref[sl, :].astype(F32), cwc_ref[...], cbc_ref[...], period))
        bs[sl, :] = _silu(_conv3(b_ref[sl, :].astype(F32), cwb_ref[...], cbb_ref[...], period)).astype(BF16)
        dt = _softplus(dt_ref[sl, :] + dtb_ref[...])
        acs[sl, :] = -jnp.exp(alog_ref[...]) * dt
        for k in range(rows // ch):
            rk = slice(k * ch, (k + 1) * ch)
            for j in range(gw // ch):
                xst[c0 + k, j * ch:(j + 1) * ch, :] = x[rk, j * ch:(j + 1) * ch].T
            cst[c0 + k] = c[rk].T.astype(BF16)
            dtt[c0 + k] = dt[rk].T

    _conv_rows(n_lat, n_ctx, conv_piece)

    si = lax.broadcasted_iota(jnp.int32, (ch, ch), 0)
    ti = lax.broadcasted_iota(jnp.int32, (ch, ch), 1)

    for d in range(2):
        mask = (si <= ti) if d == 0 else (si >= ti)
        hst[...] = jnp.zeros_like(hst)

        def body(j, carry, d=d, mask=mask):
            c = _chunk_index(j, d, n_lat_chunks, n_chunks)
            sl = pl.ds(pl.multiple_of(c * ch, ch), ch)
            cum = _dot01(tri_ref[d], acs[sl, :])
            cum_t = cum.T
            tot_c = cum_t[:, ch - 1:ch] if d == 0 else cum_t[:, 0:1]
            e_tot = jnp.exp(cum[ch - 1:ch, :] if d == 0 else cum[0:1, :])
            dt_t = dtt[c]
            e_cum_t = jnp.exp(cum_t)
            e_end_t = jnp.exp(tot_c - cum_t) * dt_t
            bc = bs[sl, :]
            cc_t = cst[c]
            cb_t = _dot(bc, cc_t)
            for h in range(heads):
                col = d * heads + h
                hs = slice(h * p, (h + 1) * p)
                diff = cum_t[col:col + 1, :] - cum[:, col:col + 1]
                decay = jnp.exp(jnp.where(mask, diff, -jnp.inf))
                xh = xst[c, hs, :]
                state = hst[h]
                y = _dot((xh * dt_t[col:col + 1, :]).astype(BF16), (cb_t * decay).astype(BF16))
                y = y + _dot(state.astype(BF16), cc_t) * e_cum_t[col:col + 1, :]
                xw = (xh * e_end_t[col:col + 1, :]).astype(BF16)
                hst[h] = state * e_tot[:, col:col + 1] + _dot(xw, bc)
                if d == 0:
                    yst[c, hs, :] = y
                else:
                    yst[c, hs, :] = yst[c, hs, :] + y
            if d == 1:
                g_t = yst[c] + xst[c] * dsk_ref[...]
                g = jnp.concatenate([g_t[j * ch:(j + 1) * ch].T for j in range(gw // ch)], axis=1)
                g = g * _silu(z_ref[sl, :].astype(F32))
                o_ref[sl, :] = (_rms(g) * nw_ref[...]).astype(BF16)
            return carry

        lax.fori_loop(0, n_chunks, body, 0, unroll=2)


def _ssd(u, us, conv_w, conv_b, dtb, alog, dsk, nw, n_lat, n_ctx):
    b, lt, _ = u.shape
    gw = SSD_HEAD_DIM * 8
    ns = SSD_STATE
    nc = lt // SSD_CHUNK
    cx_blk, cb_blk, cc_blk = 0, 1024 // ns, 1280 // ns
    par = lambda r, w, off: pl.BlockSpec((r, w), lambda b, g, off=off: (0, off + g))
    grp = lambda w: pl.BlockSpec((None, 1, w), lambda b, g: (g, 0, 0))
    return pl.pallas_call(
        functools.partial(_ssd_kernel, n_lat=n_lat, n_ctx=n_ctx),
        grid=(b, SSD_GROUPS),
        in_specs=[
            pl.BlockSpec((None, lt, 2 * gw + 2 * ns), lambda b, g: (b, 0, g)),
            pl.BlockSpec((None, lt, 128), lambda b, g: (b, 0, g)),
            par(3, gw, cx_blk), par(3, ns, cb_blk), par(3, ns, cc_blk),
            par(1, gw, cx_blk), par(1, ns, cb_blk), par(1, ns, cc_blk),
            grp(128), grp(128), pl.BlockSpec((None, gw, 128), lambda b, g: (g, 0, 0)), grp(gw),
            pl.BlockSpec((2, SSD_CHUNK, 3 * SSD_CHUNK), lambda b, g: (0, 0, 0)),
        ],
        out_specs=pl.BlockSpec((None, lt, gw), lambda b, g: (b, 0, g)),
        out_shape=jax.ShapeDtypeStruct((b, lt, SSD_GROUPS * gw), BF16),
        scratch_shapes=[
            pltpu.VMEM((nc, gw, SSD_CHUNK), F32), pltpu.VMEM((lt, ns), BF16), pltpu.VMEM((nc, ns, SSD_CHUNK), BF16),
            pltpu.VMEM((nc, 128, SSD_CHUNK), F32), pltpu.VMEM((lt, 128), F32), pltpu.VMEM((nc, gw, SSD_CHUNK), F32),
            pltpu.VMEM((gw // SSD_HEAD_DIM, SSD_HEAD_DIM, ns), F32),
        ],
        compiler_params=_cparams("parallel", "parallel"),
        name="ssd",
    )(u, us, conv_w, conv_w, conv_w, conv_b, conv_b, conv_b, dtb, alog, dsk, nw, _tri_consts(SSD_CHUNK))


_HGRN_LEVELS = (32, 16, 8, 4, 2, 1)


def _hgrn_consts():
    c = HGRN_CHUNK
    sums = np.zeros((7, c, c), np.float32)
    pairs = np.zeros((7, c, c), np.float32)
    for li, m in enumerate(_HGRN_LEVELS):
        for t in range(c):
            beta = (t // (2 * m)) * 2 * m
            mid = beta + m
            if t >= mid:
                sums[li, t, mid:t + 1] = 1.0
                pairs[li, t, beta:mid] = 1.0
            else:
                sums[li, t, t + 1:mid] = 1.0
    sums[6] = np.tril(np.ones((c, c), np.float32))
    pairs[6] = np.eye(c, dtype=np.float32)
    sums = np.stack([sums, sums[:, ::-1, ::-1]]).reshape(2, 7 * c, c)
    sums = np.concatenate([sums, sums, sums, np.zeros_like(sums)], axis=2)
    pairs = np.stack([pairs, pairs[:, ::-1, ::-1]])
    zero = np.zeros_like(pairs)
    pairs = np.concatenate([np.concatenate([pairs, zero], axis=3), np.concatenate([zero, pairs], axis=3)], axis=2)
    return jnp.asarray(sums, BF16), jnp.asarray(pairs, F32)


def _hgrn_kernel(u_ref, lb_ref, nw_ref, sums_ref, pairs_ref, o_ref,
                 ys, qb_s, *dir_scratch, n_lat, n_ctx):
    dk = dv = u_ref.shape[1] // 5
    q_ref, ff_ref, fb_ref, i_ref, g_ref = (u_ref.at[:, k * dk:(k + 1) * dk] for k in range(5))
    w_s, kb_s, qd_s, kd_s, et_s, att_s, p_s = zip(dir_scratch[:7], dir_scratch[7:])
    ch = HGRN_CHUNK
    n_chunks = (n_lat + n_ctx) // ch
    n_lat_chunks = n_lat // ch
    n_pairs = n_chunks // 2
    n_lv = len(_HGRN_LEVELS)
    lb = lb_ref[...]
    one_m_lb = 1.0 - lb
    t_idx = lax.broadcasted_iota(jnp.int32, (ch, dk), 0)

    laters = [[((t_idx & m) != 0) == (d == 0) for m in _HGRN_LEVELS] for d in range(2)]

    def operands(d, pi):
        f_ref = ff_ref if d == 0 else fb_ref
        r0 = pl.multiple_of(pi * 2 * ch, 2 * ch)
        sl = pl.ds(r0, 2 * ch)
        gate = jax.nn.sigmoid(f_ref[sl, :].astype(F32))
        kin = one_m_lb * (1.0 - gate)
        logf2 = jnp.maximum(jnp.log2(lb + one_m_lb * gate), HGRN_LOG2_FLOOR)
        q = _silu(q_ref[sl, :].astype(F32))
        if d == 0:
            qb_s[sl, :] = q.astype(BF16)
        kb_s[d][sl, :] = kin.astype(BF16)
        lf = jnp.concatenate([logf2[:ch], logf2[ch:]], axis=1)
        hi = lf.astype(BF16)
        r1 = lf - hi.astype(F32)
        mid = r1.astype(BF16)
        lo = (r1 - mid.astype(F32)).astype(BF16)
        rel2 = _dot(sums_ref[d], jnp.concatenate([hi, mid, lo, jnp.zeros_like(hi)], axis=0))
        for half in range(2):
            rows = slice(half * ch, (half + 1) * ch)
            rel = rel2[:, half * dk:(half + 1) * dk]
            qh, kh = q[rows], kin[rows]
            c = pi * 2 + half
            for li in range(n_lv):
                e = jnp.exp2(rel[li * ch:(li + 1) * ch])
                w_s[d][pi, li, rows, :] = (jnp.where(laters[d][li], qh, kh) * e).astype(BF16)
            bcum = rel[n_lv * ch:(n_lv + 1) * ch]
            tot = bcum[ch - 1:ch] if d == 0 else bcum[0:1]
            hs = pl.ds(r0 + half * ch, ch)
            qd_s[d][hs, :] = (qh * jnp.exp2(bcum)).astype(BF16)
            kd_s[d][hs, :] = (kh * jnp.exp2(tot - bcum)).astype(BF16)
            et_s[d][c] = jnp.broadcast_to(jnp.exp2(tot), (8, dk))

    def intra(d, pi):
        sl = pl.ds(pl.multiple_of(pi * 2 * ch, 2 * ch), 2 * ch)
        att = pairs_ref[d, n_lv] * _dot_nt(qb_s[sl, :], kb_s[d][sl, :])
        for li in range(n_lv):
            w = w_s[d][pi, li]
            att = att + pairs_ref[d, li] * _dot_nt(w, w)
        att_s[d][sl, :] = att.astype(BF16)
        for half in range(2):
            hs = pl.ds(pl.multiple_of(pi * 2 * ch, 2 * ch) + half * ch, ch)
            p_s[d][pi * 2 + half] = _dot_tn(i_ref[hs, :], kd_s[d][hs, :])

    def scan(d, j, state_t):
        pi = _chunk_index(j, d, n_lat_chunks // 2, n_pairs)
        r0 = pl.multiple_of(pi * 2 * ch, 2 * ch)
        sl = pl.ds(r0, 2 * ch)
        o_intra = _dot(att_s[d][sl, :], i_ref[sl, :])
        o_halves = [None, None]
        for half in ((0, 1) if d == 0 else (1, 0)):
            hs = pl.ds(r0 + half * ch, ch)
            o_halves[half] = (o_intra[half * ch:(half + 1) * ch]
                              + _dot_nt(qd_s[d][hs, :], state_t.astype(BF16)))
            c = pi * 2 + half
            state_t = state_t * et_s[d][c][0:1] + p_s[d][c]
        o = jnp.concatenate(o_halves, axis=0)
        if d == 0:
            ys[sl, :] = o
        else:
            o = ys[sl, :] + o
            o_ref[sl, :] = (_rms(o) * nw_ref[...] * _silu(g_ref[sl, :].astype(F32))).astype(BF16)
        return state_t

    zero_state = jnp.zeros((dv, dk), F32)

    def stage_a(pi, carry):
        operands(0, pi)
        return carry

    def stage_b(pi, carry):
        intra(0, pi)
        operands(1, pi)
        return carry

    def stage_c(j, state_t):
        intra(1, j)
        return scan(0, j, state_t)

    lax.fori_loop(0, n_pairs, stage_a, 0, unroll=2)
    lax.fori_loop(0, n_pairs, stage_b, 0, unroll=2)
    lax.fori_loop(0, n_pairs, stage_c, zero_state, unroll=3)
    lax.fori_loop(0, n_pairs, functools.partial(scan, 1), zero_state, unroll=3)


def _hgrn(u, lb, nw, n_lat, n_ctx):
    b, lt, _ = u.shape
    w = 128
    ch = HGRN_CHUNK
    n_chunks = lt // ch
    sums, pairs = _hgrn_consts()
    head = pl.BlockSpec((None, 1, w), lambda b, h: (h, 0, 0))
    return pl.pallas_call(
        functools.partial(_hgrn_kernel, n_lat=n_lat, n_ctx=n_ctx),
        grid=(b, HGRN_HEADS),
        in_specs=[pl.BlockSpec((None, lt, 5 * w), lambda b, h: (b, 0, 2560 // (5 * w) + h)), head, head,
                  pl.BlockSpec(sums.shape, lambda b, h: (0, 0, 0)),
                  pl.BlockSpec(pairs.shape, lambda b, h: (0, 0, 0, 0))],
        out_specs=pl.BlockSpec((None, lt, w), lambda b, h: (b, 0, h)),
        out_shape=jax.ShapeDtypeStruct((b, lt, HGRN_HEADS * w), BF16),
        scratch_shapes=[
            pltpu.VMEM((lt, w), F32), pltpu.VMEM((lt, w), BF16),
            *([pltpu.VMEM((n_chunks // 2, len(_HGRN_LEVELS), 2 * ch, w), BF16),
               pltpu.VMEM((lt, w), BF16), pltpu.VMEM((lt, w), BF16), pltpu.VMEM((lt, w), BF16),
               pltpu.VMEM((n_chunks, 8, w), F32), pltpu.VMEM((lt, 2 * ch), BF16),
               pltpu.VMEM((n_chunks, w, w), F32)] * 2),
        ],
        compiler_params=_cparams("parallel", "parallel"),
        name="hgrn2",
    )(u, lb, nw, sums, pairs)


def _mlstm_kernel(u_ref, gt_ref, cwq_ref, cwk_ref, cbq_ref, cbk_ref, gb_ref, nw_ref, tri_ref,
                  o_ref, qst, ks, vst, gst, yst, cst, *, n_lat, n_ctx):
    ch = MLSTM_CHUNK
    n_chunks = (n_lat + n_ctx) // ch
    n_lat_chunks = n_lat // ch
    dqk = cwq_ref.shape[1]
    dv = nw_ref.shape[1]
    q_ref, k_ref = u_ref.at[:, 0:dqk], u_ref.at[:, dqk:2 * dqk]
    v_ref, og_ref = u_ref.at[:, 2 * dqk:2 * dqk + dv], u_ref.at[:, 2 * dqk + dv:]
    k_scale = dqk ** -0.5

    def conv_piece(r0, rows, period):
        sl = pl.ds(r0, rows)
        c0 = r0 // ch
        q = _silu(_conv3(q_ref[sl, :].astype(F32), cwq_ref[...], cbq_ref[...], period))
        ks[sl, :] = (_silu(_conv3(k_ref[sl, :].astype(F32), cwk_ref[...], cbk_ref[...], period)) * k_scale).astype(BF16)
        raw = gt_ref[sl, :] + gb_ref[...]
        lane = lax.broadcasted_iota(jnp.int32, raw.shape, 1)
        g = jnp.where(lane < 2, raw, _log_sigmoid(raw))
        v = v_ref[sl, :].astype(F32)
        for k in range(rows // ch):
            rk = slice(k * ch, (k + 1) * ch)
            for j in range(dqk // ch):
                qst[c0 + k, j * ch:(j + 1) * ch, :] = q[rk, j * ch:(j + 1) * ch].T.astype(BF16)
            for j in range(dv // ch):
                vst[c0 + k, j * ch:(j + 1) * ch, :] = v[rk, j * ch:(j + 1) * ch].T.astype(BF16)
            vst[c0 + k, dv:, :] = jnp.ones((MLSTM_ONES_ROWS, ch), BF16)
            gst[c0 + k] = g[rk].T

    _conv_rows(n_lat, n_ctx, conv_piece)

    si = lax.broadcasted_iota(jnp.int32, (ch, ch), 0)
    ti = lax.broadcasted_iota(jnp.int32, (ch, ch), 1)

    for d in range(2):
        mask = (si <= ti) if d == 0 else (si >= ti)
        cst[...] = jnp.zeros_like(cst)

        def body(j, m_prev, d=d, mask=mask):
            c = _chunk_index(j, d, n_lat_chunks, n_chunks)
            sl = pl.ds(pl.multiple_of(c * ch, ch), ch)
            g_t = gst[c]
            cum_t = _dot01_rows(g_t, tri_ref[d])
            brow = cum_t[2 + d:3 + d, :]
            irow = g_t[d:d + 1, :]
            tot = brow[:, ch - 1:ch] if d == 0 else brow[:, 0:1]
            logd = jnp.where(mask, brow + jnp.broadcast_to(irow - brow, (ch, ch)).T, -jnp.inf)
            gstate = brow + m_prev
            mt = jnp.maximum(jnp.max(logd, axis=0, keepdims=True), gstate)
            q_t = qst[c]
            kc = ks[sl, :]
            w = _dot(kc, q_t) * jnp.exp(logd - mt)
            sw = jnp.exp(gstate - mt)
            v_t = vst[c]
            state = cst[...]
            qstate = _dot(state.astype(BF16), q_t)
            num = _dot(v_t[:dv], w.astype(BF16)) + sw * qstate[:dv]
            den = jnp.sum(w, axis=0, keepdims=True) + sw * qstate[dv:dv + 1]
            hout = num * (1.0 / jnp.maximum(jnp.abs(den), jnp.exp(-mt)))
            logw = tot - brow + irow
            m_new = jnp.maximum(tot + m_prev, jnp.max(logw, axis=1, keepdims=True))
            ws = jnp.exp(logw - m_new).astype(BF16)
            cst[...] = jnp.exp(tot + m_prev - m_new) * state + _dot(v_t * ws, kc)
            if d == 0:
                yst[c] = hout
            else:
                hh_t = yst[c] + hout
                hh = jnp.concatenate([hh_t[i * ch:(i + 1) * ch].T for i in range(dv // ch)], axis=1)
                o_ref[sl, :] = (_rms(hh) * nw_ref[...] * jax.nn.sigmoid(og_ref[sl, :].astype(F32))).astype(BF16)
            return m_new

        lax.fori_loop(0, n_chunks, body, jnp.zeros((1, 1), F32), unroll=2)


def _mlstm(u, us, conv_w, conv_b, gate_b, nw, n_lat, n_ctx):
    b, lt, _ = u.shape
    dqk, dv = 256, 512
    nh = MLSTM_HEADS
    ch = MLSTM_CHUNK
    nc = lt // ch
    upper = np.triu(np.ones((ch, ch), np.float32))
    tri = np.stack([upper, upper.T])
    tri = jnp.asarray(np.concatenate([tri, tri, tri], axis=1), BF16)
    par = lambda r, off: pl.BlockSpec((r, dqk), lambda b, h, off=off: (0, off + h))
    return pl.pallas_call(
        functools.partial(_mlstm_kernel, n_lat=n_lat, n_ctx=n_ctx),
        grid=(b, nh),
        in_specs=[
            pl.BlockSpec((None, lt, 2 * dqk + 2 * dv), lambda b, h: (b, 0, h)),
            pl.BlockSpec((None, lt, 128), lambda b, h: (b, 0, h)),
            par(3, 0), par(3, nh), par(1, 0), par(1, nh),
            pl.BlockSpec((None, 1, 128), lambda b, h: (h, 0, 0)),
            pl.BlockSpec((None, 1, dv), lambda b, h: (h, 0, 0)),
            pl.BlockSpec(tri.shape, lambda b, h: (0, 0, 0)),
        ],
        out_specs=pl.BlockSpec((None, lt, dv), lambda b, h: (b, 0, h)),
        out_shape=jax.ShapeDtypeStruct((b, lt, nh * dv), BF16),
        scratch_shapes=[
            pltpu.VMEM((nc, dqk, ch), BF16), pltpu.VMEM((lt, dqk), BF16),
            pltpu.VMEM((nc, dv + MLSTM_ONES_ROWS, ch), BF16), pltpu.VMEM((nc, 128, ch), F32),
            pltpu.VMEM((nc, dv, ch), F32), pltpu.VMEM((dv + MLSTM_ONES_ROWS, dqk), F32),
        ],
        compiler_params=_cparams("parallel", "parallel"),
        name="mlstm",
    )(u, us, conv_w, conv_w, conv_b, conv_b, gate_b, nw, tri)


def _pad_lanes(a, width=128):
    return jnp.pad(a, [(0, 0)] * (a.ndim - 1) + [(0, width - a.shape[-1])])


def _even_params(w_in, dt_bias, a_log, d_skip):
    heads = a_log.shape[1]
    hg = heads // SSD_GROUPS
    w_hgrn = w_in[:, 2560 + 2 * heads:].reshape(w_in.shape[0], 5, HGRN_HEADS, -1).swapaxes(1, 2)
    grp = lambda a, b: w_in[:, a:b].reshape(w_in.shape[0], SSD_GROUPS, -1)
    w_ssd = jnp.concatenate([grp(1024, 2048), grp(2048, 2304), grp(2304, 2560), grp(0, 1024)], axis=2)
    w_main = jnp.concatenate([w_ssd.reshape(w_in.shape[0], -1), w_hgrn.reshape(w_in.shape[0], -1)], axis=1).astype(BF16)
    dt_w = w_in[:, 2560:2560 + 2 * heads]
    per_group = lambda a: [_pad_lanes(jnp.concatenate([a[..., g * hg:(g + 1) * hg], a[..., heads + g * hg:heads + (g + 1) * hg]], axis=-1))
                           for g in range(SSD_GROUPS)]
    w_small = jnp.concatenate(per_group(dt_w), axis=1).astype(BF16)
    flat = lambda a: a.reshape(1, 2 * heads)
    dtb = jnp.stack(per_group(flat(dt_bias)))
    alog = jnp.stack(per_group(flat(a_log)))
    dsk = jnp.broadcast_to(jnp.repeat(d_skip, SSD_HEAD_DIM).reshape(SSD_GROUPS, hg * SSD_HEAD_DIM, 1),
                           (SSD_GROUPS, hg * SSD_HEAD_DIM, 128))
    return w_main, w_small, dtb, alog, dsk


def _odd_params(w_in, gate_b):
    nh = MLSTM_HEADS
    parts = [w_in[:, a:b].reshape(w_in.shape[0], nh, -1) for a, b in ((0, 1024), (1024, 2048), (2048, 4096), (4096, 6144))]
    w_main = jnp.concatenate(parts, axis=2).reshape(w_in.shape[0], -1).astype(BF16)
    gw = w_in[:, 6144:]
    w_small = jnp.concatenate([_pad_lanes(gw[:, h::nh]) for h in range(nh)], axis=1).astype(BF16)
    gb = jnp.stack([_pad_lanes(gate_b[:, h].reshape(1, 4)) for h in range(nh)])
    return w_main, w_small, gb


def kernel(x, c, ctx, c_ctx, mod_w, mod_b, norm_w, final_norm_w, mlp_w1, mlp_w2, even_w_in, even_w_out, ssd_conv_w, ssd_conv_b, ssd_a_log, ssd_dt_bias, ssd_d, ssd_norm_w, hgrn_lb, hgrn_norm_w, odd_w_in, odd_w_out, mlstm_conv_w, mlstm_conv_b, mlstm_gate_b, mlstm_norm_w):
    bsz, n_lat, d = x.shape
    n_ctx = ctx.shape[1]
    depth = mod_w.shape[0]
    lt = n_lat + n_ctx
    assert bsz < C_ROWS and n_lat % CONV_PIECE == 0 and n_ctx % SSD_CHUNK == 0 and n_ctx & (n_ctx - 1) == 0
    ctx_row = bsz

    c_all = jnp.zeros((C_ROWS, d), F32).at[:bsz].set(c).at[ctx_row].set(c_ctx)
    mods = _modulation(c_all, mod_w, mod_b).reshape(depth, C_ROWS, N_MOD, 1, d)

    lb_all = jnp.cumsum(jax.nn.softmax(hgrn_lb.astype(F32), axis=0), axis=0)
    lb_all = lb_all - lb_all[0]

    tm_in = lt // 2
    tm_full = lt // 4
    tm_lat = min(512, n_lat)

    mlp_w1_b, mlp_w2_b = mlp_w1.astype(BF16), mlp_w2.astype(BF16)
    even_w_out_b, odd_w_out_b = even_w_out.astype(BF16), odd_w_out.astype(BF16)

    xx = jnp.concatenate([x, ctx], axis=1)
    for layer in range(depth):
        last = layer == depth - 1
        nw1 = norm_w[layer, 0].reshape(1, d)
        nw2 = norm_w[layer, 1].reshape(1, d)
        if layer % 2 == 0:
            e = layer // 2
            w_main, w_small, dtb, alog, dsk = _even_params(even_w_in[e], ssd_dt_bias[e], ssd_a_log[e], ssd_d[e])
            u, us = _inproj(xx, nw1, mods, layer, ctx_row, w_main, w_small, n_lat, tm_in, 1536)
            ya = _ssd(u, us, ssd_conv_w[e], ssd_conv_b[e].reshape(1, -1), dtb, alog, dsk,
                      ssd_norm_w[e].reshape(SSD_GROUPS, 1, -1), n_lat, n_ctx)
            yb = _hgrn(u, lb_all[e].reshape(HGRN_HEADS, 1, -1), hgrn_norm_w[e].reshape(HGRN_HEADS, 1, -1), n_lat, n_ctx)
            ys, w_out, w_idx = [ya, yb], even_w_out_b, e
        else:
            o = layer // 2
            w_main, w_small, gb = _odd_params(odd_w_in[o], mlstm_gate_b[o])
            u, us = _inproj(xx, nw1, mods, layer, ctx_row, w_main, w_small, n_lat, tm_in, 1024)
            yc = _mlstm(u, us, mlstm_conv_w[o], mlstm_conv_b[o].reshape(1, -1), gb,
                        mlstm_norm_w[o].reshape(MLSTM_HEADS, 1, -1), n_lat, n_ctx)
            ys, w_out, w_idx = [yc], odd_w_out_b, o
        rows, tm = (n_lat, tm_lat) if last else (lt, tm_full)
        xx = _outproj(ys, w_out, w_idx, xx, mods, layer, ctx_row, n_lat, rows, tm // 2 if not last else tm)
        xx = _mlp(xx, nw2, mods, layer, ctx_row, mlp_w1_b, mlp_w2_b,
                  final_norm_w.reshape(1, d), n_lat, rows, tm, 1024, last)
    return xx
```

```python
import functools

import numpy as np
import jax
import jax.numpy as jnp
from jax import lax
from jax.experimental import pallas as pl
from jax.experimental.pallas import tpu as pltpu

F32 = jnp.float32
BF16 = jnp.bfloat16

EPS = 1e-6
GRID_W = 64
N_MOD = 6
HGRN_LOG2_FLOOR = -1e5
C_ROWS = 32

V7X_VMEM_BYTES = 64 * 1024 * 1024
VMEM_LIMIT = V7X_VMEM_BYTES - 8 * 1024 * 1024

SSD_CHUNK = 128
SSD_HEAD_DIM = 64
SSD_STATE = 128
SSD_GROUPS = 2
HGRN_CHUNK = 64
HGRN_HEADS = 8
MLSTM_CHUNK = 128
MLSTM_HEADS = 4
MLSTM_ONES_ROWS = 16
CONV_PIECE = 256
NORM_ROWS = 16


def _cparams(*sem):
    return pltpu.CompilerParams(dimension_semantics=sem, vmem_limit_bytes=VMEM_LIMIT)


def _silu(x):
    return x * jax.nn.sigmoid(x)


def _softplus(x):
    return jnp.maximum(x, 0.0) + jnp.log(1.0 + jnp.exp(-jnp.abs(x)))


def _log_sigmoid(x):
    return -_softplus(-x)


def _dot(a, b):
    return jnp.dot(a, b, preferred_element_type=F32)


def _dot_nt(a, b):
    return lax.dot_general(a, b, (((1,), (1,)), ((), ())), preferred_element_type=F32)


def _dot_tn(a, b):
    return lax.dot_general(a, b, (((0,), (0,)), ((), ())), preferred_element_type=F32)


def _dot01(m01x3, x):
    hi = x.astype(BF16)
    r = x - hi.astype(F32)
    mid = r.astype(BF16)
    lo = (r - mid.astype(F32)).astype(BF16)
    return _dot(m01x3, jnp.concatenate([hi, mid, lo], axis=0))


def _dot01_rows(x, m01x3):
    hi = x.astype(BF16)
    r = x - hi.astype(F32)
    mid = r.astype(BF16)
    lo = (r - mid.astype(F32)).astype(BF16)
    return _dot(jnp.concatenate([hi, mid, lo], axis=1), m01x3)


def _rms(x):
    return x * lax.rsqrt(jnp.mean(x * x, axis=-1, keepdims=True) + EPS)


def _conv3(u, w, b, period):
    rows = u.shape[0]
    t = lax.broadcasted_iota(jnp.int32, u.shape, 0) & (period - 1)
    left = jnp.where(t == 0, 0.0, pltpu.roll(u, 1, axis=0))
    right = jnp.where(t == period - 1, 0.0, pltpu.roll(u, rows - 1, axis=0))
    return left * w[0:1] + u * w[1:2] + right * w[2:3] + b


def _modulated_norm(x_ref, h_ref, nw, sh_l, sc_l, sh_c, sc_c, row0, n_lat):
    a_l, a_c = nw * (1.0 + sc_l), nw * (1.0 + sc_c)

    def block(r, carry):
        start = pl.multiple_of(r * NORM_ROWS, NORM_ROWS)
        rs = pl.ds(start, NORM_ROWS)
        is_ctx = row0 + start >= n_lat
        h_ref[rs, :] = (_rms(x_ref[rs, :]) * jnp.where(is_ctx, a_c, a_l) + jnp.where(is_ctx, sh_c, sh_l)).astype(BF16)
        return carry

    lax.fori_loop(0, x_ref.shape[0] // NORM_ROWS, block, 0, unroll=4)


def _mod_kernel(c_ref, w_ref, b_ref, o_ref):
    a = _silu(c_ref[...]).astype(BF16)
    o_ref[...] = _dot(a, w_ref[...].astype(BF16)) + b_ref[...]


def _modulation(c_all, mod_w, mod_b):
    depth, d, n = mod_w.shape
    tn = 1024
    return pl.pallas_call(
        _mod_kernel,
        grid=(depth, n // tn),
        in_specs=[
            pl.BlockSpec((C_ROWS, d), lambda l, j: (0, 0)),
            pl.BlockSpec((None, d, tn), lambda l, j: (l, 0, j)),
            pl.BlockSpec((None, 1, tn), lambda l, j: (l, 0, j)),
        ],
        out_specs=pl.BlockSpec((None, C_ROWS, tn), lambda l, j: (l, 0, j)),
        out_shape=jax.ShapeDtypeStruct((depth, C_ROWS, n), F32),
        compiler_params=_cparams("parallel", "parallel"),
        name="modulation",
    )(c_all, mod_w, mod_b.reshape(depth, 1, n))


def _mod_specs(layer, ctx_row, ks, d, nargs):
    specs = []
    for k in ks:
        if nargs == 3:
            specs.append(pl.BlockSpec((None, None, None, 1, d), lambda b, i, j, k=k: (layer, b, k, 0, 0)))
            specs.append(pl.BlockSpec((None, None, None, 1, d), lambda b, i, j, k=k: (layer, ctx_row, k, 0, 0)))
        else:
            specs.append(pl.BlockSpec((None, None, None, 1, d), lambda b, i, k=k: (layer, b, k, 0, 0)))
            specs.append(pl.BlockSpec((None, None, None, 1, d), lambda b, i, k=k: (layer, ctx_row, k, 0, 0)))
    return specs


def _inproj_kernel(x_ref, nw_ref, shl_ref, shc_ref, scl_ref, scc_ref, w_ref, ws_ref, o_ref, os_ref, h_ref, *, n_lat):
    @pl.when(pl.program_id(2) == 0)
    def _():
        _modulated_norm(x_ref, h_ref, nw_ref[...], shl_ref[...], scl_ref[...], shc_ref[...], scc_ref[...],
                        pl.program_id(1) * x_ref.shape[0], n_lat)
        os_ref[...] = _dot(h_ref[...], ws_ref[...])

    o_ref[...] = _dot(h_ref[...], w_ref[...]).astype(o_ref.dtype)


def _inproj(x, nw, mods, layer, ctx_row, w_main, w_small, n_lat, tm, tn):
    b, lt, d = x.shape
    n = w_main.shape[1]
    ns = w_small.shape[1]
    return pl.pallas_call(
        functools.partial(_inproj_kernel, n_lat=n_lat),
        grid=(b, lt // tm, n // tn),
        in_specs=[
            pl.BlockSpec((None, tm, d), lambda b, i, j: (b, i, 0)),
            pl.BlockSpec((1, d), lambda b, i, j: (0, 0)),
            *_mod_specs(layer, ctx_row, (0, 1), d, 3),
            pl.BlockSpec((d, tn), lambda b, i, j: (0, j)),
            pl.BlockSpec((d, ns), lambda b, i, j: (0, 0)),
        ],
        out_specs=[
            pl.BlockSpec((None, tm, tn), lambda b, i, j: (b, i, j)),
            pl.BlockSpec((None, tm, ns), lambda b, i, j: (b, i, 0)),
        ],
        out_shape=[jax.ShapeDtypeStruct((b, lt, n), BF16), jax.ShapeDtypeStruct((b, lt, ns), F32)],
        scratch_shapes=[pltpu.VMEM((tm, d), BF16)],
        compiler_params=_cparams("parallel", "parallel", "arbitrary"),
        name="inproj",
    )(x, nw, mods, mods, mods, mods, w_main, w_small)


def _outproj_kernel(*refs, n_y, n_lat):
    y_refs, w_refs = refs[:n_y], refs[n_y:2 * n_y]
    x_ref, gl_ref, gc_ref, o_ref = refs[2 * n_y:]
    acc = _dot(y_refs[0][...], w_refs[0][...])
    for y_ref, w_ref in zip(y_refs[1:], w_refs[1:]):
        acc = acc + _dot(y_ref[...], w_ref[...])
    tm = x_ref.shape[0]
    rows = pl.program_id(1) * tm + lax.broadcasted_iota(jnp.int32, (tm, 1), 0)
    g = jnp.where(rows >= n_lat, gc_ref[...], gl_ref[...])
    o_ref[...] = x_ref[...] + g * acc


def _outproj(ys, w, w_idx, x, mods, layer, ctx_row, n_lat, rows, tm):
    b, lt, d = x.shape
    n_y = len(ys)
    y_specs = [pl.BlockSpec((None, tm, y.shape[2]), lambda b, i: (b, i, 0)) for y in ys]
    w_specs = [pl.BlockSpec((None, y.shape[2], d), lambda b, i, k=k: (w_idx, k, 0)) for k, y in enumerate(ys)]
    return pl.pallas_call(
        functools.partial(_outproj_kernel, n_y=n_y, n_lat=n_lat),
        grid=(b, rows // tm),
        in_specs=[*y_specs, *w_specs,
                  pl.BlockSpec((None, tm, d), lambda b, i: (b, i, 0)),
                  *_mod_specs(layer, ctx_row, (2,), d, 2)],
        out_specs=pl.BlockSpec((None, tm, d), lambda b, i: (b, i, 0)),
        out_shape=jax.ShapeDtypeStruct((b, rows, d), F32),
        compiler_params=_cparams("parallel", "parallel"),
        name="outproj",
    )(*ys, *([w] * n_y), x, mods, mods)


def _mlp_kernel(x_ref, nw_ref, shl_ref, shc_ref, scl_ref, scc_ref, gl_ref, gc_ref, w1_ref, w2_ref, fw_ref,
                o_ref, h_ref, acc_ref, *, n_lat, final_norm):
    f = pl.program_id(2)
    tm = x_ref.shape[0]

    @pl.when(f == 0)
    def _():
        _modulated_norm(x_ref, h_ref, nw_ref[...], shl_ref[...], scl_ref[...], shc_ref[...], scc_ref[...],
                        pl.program_id(1) * tm, n_lat)
        acc_ref[...] = jnp.zeros_like(acc_ref)

    a = jnp.maximum(_dot(h_ref[...], w1_ref[...]), 0.0)
    acc_ref[...] += _dot((a * a).astype(BF16), w2_ref[...])

    @pl.when(f == pl.num_programs(2) - 1)
    def _():
        rows = pl.program_id(1) * tm + lax.broadcasted_iota(jnp.int32, (tm, 1), 0)
        g = jnp.where(rows >= n_lat, gc_ref[...], gl_ref[...])
        y = x_ref[...] + g * acc_ref[...]
        if final_norm:
            y = _rms(y) * fw_ref[...]
        o_ref[...] = y


def _mlp(x, nw, mods, layer, ctx_row, w1, w2, fw, n_lat, rows, tm, tf, final_norm):
    b, _, d = x.shape
    dff = w1.shape[2]
    return pl.pallas_call(
        functools.partial(_mlp_kernel, n_lat=n_lat, final_norm=final_norm),
        grid=(b, rows // tm, dff // tf),
        in_specs=[
            pl.BlockSpec((None, tm, d), lambda b, i, j: (b, i, 0)),
            pl.BlockSpec((1, d), lambda b, i, j: (0, 0)),
            *_mod_specs(layer, ctx_row, (3, 4, 5), d, 3),
            pl.BlockSpec((None, d, tf), lambda b, i, j: (layer, 0, j)),
            pl.BlockSpec((None, tf, d), lambda b, i, j: (layer, j, 0)),
            pl.BlockSpec((1, d), lambda b, i, j: (0, 0)),
        ],
        out_specs=pl.BlockSpec((None, tm, d), lambda b, i, j: (b, i, 0)),
        out_shape=jax.ShapeDtypeStruct((b, rows, d), F32),
        scratch_shapes=[pltpu.VMEM((tm, d), BF16), pltpu.VMEM((tm, d), F32)],
        compiler_params=_cparams("parallel", "parallel", "arbitrary"),
        name="mlp",
    )(x, nw, mods, mods, mods, mods, mods, mods, w1, w2, fw)


def _tri_consts(c):
    lower = np.tril(np.ones((c, c), np.float32))
    tri = np.stack([lower, lower.T])
    return jnp.asarray(np.concatenate([tri, tri, tri], axis=2), BF16)


def _chunk_index(j, d, n_lat_chunks, n_chunks):
    if d == 0:
        c = j + n_lat_chunks
        return jnp.where(c >= n_chunks, c - n_chunks, c)
    return n_chunks - 1 - j


def _conv_rows(n_lat, n_ctx, fn):
    def body(p, carry):
        fn(pl.multiple_of(p * CONV_PIECE, CONV_PIECE), CONV_PIECE, GRID_W)
        return carry
    lax.fori_loop(0, n_lat // CONV_PIECE, body, 0)
    fn(n_lat, n_ctx, n_ctx)


def _ssd_kernel(u_ref, dt_ref, cwx_ref, cwb_ref, cwc_ref, cbx_ref, cbb_ref, cbc_ref,
                dtb_ref, alog_ref, dsk_ref, nw_ref, tri_ref, o_ref,
                xst, bs, cst, dtt, acs, yst, hst, *, n_lat, n_ctx):
    ch = SSD_CHUNK
    p = SSD_HEAD_DIM
    gw = cwx_ref.shape[1]
    ns = cwb_ref.shape[1]
    heads = gw // p
    x_ref, b_ref = u_ref.at[:, 0:gw], u_ref.at[:, gw:gw + ns]
    c_ref, z_ref = u_ref.at[:, gw + ns:gw + 2 * ns], u_ref.at[:, gw + 2 * ns:]
    n_chunks = (n_lat + n_ctx) // ch
    n_lat_chunks = n_lat // ch

    def conv_piece(r0, rows, period):
        sl = pl.ds(r0, rows)
        c0 = r0 // ch
        x = _silu(_conv3(x_ref[sl, :].astype(F32), cwx_ref[...], cbx_ref[...], period))
        c = _silu(_conv3(c_ref[sl, :].astype(F32), cwc_ref[...], cbc_ref[...], period))
        bs[sl, :] = _silu(_conv3(b_ref[sl, :].astype(F32), cwb_ref[...], cbb_ref[...], period)).astype(BF16)
        dt = _softplus(dt_ref[sl, :] + dtb_ref[...])
        acs[sl, :] = -jnp.exp(alog_ref[...]) * dt
        for k in range(rows // ch):
            rk = slice(k * ch, (k + 1) * ch)
            for j in range(gw // ch):
                xst[c0 + k, j * ch:(j + 1) * ch, :] = x[rk, j * ch:(j + 1) * ch].T
            cst[c0 + k] = c[rk].T.astype(BF16)
            dtt[c0 + k] = dt[rk].T

    _conv_rows(n_lat, n_ctx, conv_piece)

    si = lax.broadcasted_iota(jnp.int32, (ch, ch), 0)
    ti = lax.broadcasted_iota(jnp.int32, (ch, ch), 1)

    for d in range(2):
        mask = (si <= ti) if d == 0 else (si >= ti)
        hst[...] = jnp.zeros_like(hst)

        def body(j, carry, d=d, mask=mask):
            c = _chunk_index(j, d, n_lat_chunks, n_chunks)
            sl = pl.ds(pl.multiple_of(c * ch, ch), ch)
            cum = _dot01(tri_ref[d], acs[sl, :])
            cum_t = cum.T
            tot_c = cum_t[:, ch - 1:ch] if d == 0 else cum_t[:, 0:1]
            e_tot = jnp.exp(cum[ch - 1:ch, :] if d == 0 else cum[0:1, :])
            dt_t = dtt[c]
            e_cum_t = jnp.exp(cum_t)
            e_end_t = jnp.exp(tot_c - cum_t) * dt_t
            bc = bs[sl, :]
            cc_t = cst[c]
            cb_t = _dot(bc, cc_t)
            for h in range(heads):
                col = d * heads + h
                hs = slice(h * p, (h + 1) * p)
                diff = cum_t[col:col + 1, :] - cum[:, col:col + 1]
                decay = jnp.exp(jnp.where(mask, diff, -jnp.inf))
                xh = xst[c, hs, :]
                state = hst[h]
                y = _dot((xh * dt_t[col:col + 1, :]).astype(BF16), (cb_t * decay).astype(BF16))
                y = y + _dot(state.astype(BF16), cc_t) * e_cum_t[col:col + 1, :]
                xw = (xh * e_end_t[col:col + 1, :]).astype(BF16)
                hst[h] = state * e_tot[:, col:col + 1] + _dot(xw, bc)
                if d == 0:
                    yst[c, hs, :] = y
                else:
                    yst[c, hs, :] = yst[c, hs, :] + y
            if d == 1:
                g_t = yst[c] + xst[c] * dsk_ref[...]
                g = jnp.concatenate([g_t[j * ch:(j + 1) * ch].T for j in range(gw // ch)], axis=1)
                g = g * _silu(z_ref[sl, :].astype(F32))
                o_ref[sl, :] = (_rms(g) * nw_ref[...]).astype(BF16)
            return carry

        lax.fori_loop(0, n_chunks, body, 0, unroll=3)


def _ssd(u, us, conv_w, conv_b, dtb, alog, dsk, nw, n_lat, n_ctx):
    b, lt, _ = u.shape
    gw = SSD_HEAD_DIM * 8
    ns = SSD_STATE
    nc = lt // SSD_CHUNK
    cx_blk, cb_blk, cc_blk = 0, 1024 // ns, 1280 // ns
    par = lambda r, w, off: pl.BlockSpec((r, w), lambda b, g, off=off: (0, off + g))
    grp = lambda w: pl.BlockSpec((None, 1, w), lambda b, g: (g, 0, 0))
    return pl.pallas_call(
        functools.partial(_ssd_kernel, n_lat=n_lat, n_ctx=n_ctx),
        grid=(b, SSD_GROUPS),
        in_specs=[
            pl.BlockSpec((None, lt, 2 * gw + 2 * ns), lambda b, g: (b, 0, g)),
            pl.BlockSpec((None, lt, 128), lambda b, g: (b, 0, g)),
            par(3, gw, cx_blk), par(3, ns, cb_blk), par(3, ns, cc_blk),
            par(1, gw, cx_blk), par(1, ns, cb_blk), par(1, ns, cc_blk),
            grp(128), grp(128), pl.BlockSpec((None, gw, 128), lambda b, g: (g, 0, 0)), grp(gw),
            pl.BlockSpec((2, SSD_CHUNK, 3 * SSD_CHUNK), lambda b, g: (0, 0, 0)),
        ],
        out_specs=pl.BlockSpec((None, lt, gw), lambda b, g: (b, 0, g)),
        out_shape=jax.ShapeDtypeStruct((b, lt, SSD_GROUPS * gw), BF16),
        scratch_shapes=[
            pltpu.VMEM((nc, gw, SSD_CHUNK), F32), pltpu.VMEM((lt, ns), BF16), pltpu.VMEM((nc, ns, SSD_CHUNK), BF16),
            pltpu.VMEM((nc, 128, SSD_CHUNK), F32), pltpu.VMEM((lt, 128), F32), pltpu.VMEM((nc, gw, SSD_CHUNK), F32),
            pltpu.VMEM((gw // SSD_HEAD_DIM, SSD_HEAD_DIM, ns), F32),
        ],
        compiler_params=_cparams("parallel", "parallel"),
        name="ssd",
    )(u, us, conv_w, conv_w, conv_w, conv_b, conv_b, conv_b, dtb, alog, dsk, nw, _tri_consts(SSD_CHUNK))


_HGRN_LEVELS = (32, 16, 8, 4, 2, 1)


def _hgrn_consts():
    c = HGRN_CHUNK
    sums = np.zeros((7, c, c), np.float32)
    pairs = np.zeros((7, c, c), np.float32)
    for li, m in enumerate(_HGRN_LEVELS):
        for t in range(c):
            beta = (t // (2 * m)) * 2 * m
            mid = beta + m
            if t >= mid:
                sums[li, t, mid:t + 1] = 1.0
                pairs[li, t, beta:mid] = 1.0
            else:
                sums[li, t, t + 1:mid] = 1.0
    sums[6] = np.tril(np.ones((c, c), np.float32))
    pairs[6] = np.eye(c, dtype=np.float32)
    sums = np.stack([sums, sums[:, ::-1, ::-1]]).reshape(2, 7 * c, c)
    sums = np.concatenate([sums, sums, sums, np.zeros_like(sums)], axis=2)
    pairs = np.stack([pairs, pairs[:, ::-1, ::-1]])
    zero = np.zeros_like(pairs)
    pairs = np.concatenate([np.concatenate([pairs, zero], axis=3), np.concatenate([zero, pairs], axis=3)], axis=2)
    return jnp.asarray(sums, BF16), jnp.asarray(pairs, F32)


def _hgrn_kernel(u_ref, lb_ref, nw_ref, sums_ref, pairs_ref, o_ref,
                 ys, qb_s, *dir_scratch, n_lat, n_ctx):
    dk = dv = u_ref.shape[1] // 5
    q_ref, ff_ref, fb_ref, i_ref, g_ref = (u_ref.at[:, k * dk:(k + 1) * dk] for k in range(5))
    w_s, kb_s, qd_s, kd_s, et_s, att_s, p_s = zip(dir_scratch[:7], dir_scratch[7:])
    ch = HGRN_CHUNK
    n_chunks = (n_lat + n_ctx) // ch
    n_lat_chunks = n_lat // ch
    n_pairs = n_chunks // 2
    n_lv = len(_HGRN_LEVELS)
    lb = lb_ref[...]
    one_m_lb = 1.0 - lb
    t_idx = lax.broadcasted_iota(jnp.int32, (ch, dk), 0)

    laters = [[((t_idx & m) != 0) == (d == 0) for m in _HGRN_LEVELS] for d in range(2)]

    def operands(d, pi):
        f_ref = ff_ref if d == 0 else fb_ref
        r0 = pl.multiple_of(pi * 2 * ch, 2 * ch)
        sl = pl.ds(r0, 2 * ch)
        gate = jax.nn.sigmoid(f_ref[sl, :].astype(F32))
        kin = one_m_lb * (1.0 - gate)
        logf2 = jnp.maximum(jnp.log2(lb + one_m_lb * gate), HGRN_LOG2_FLOOR)
        q = _silu(q_ref[sl, :].astype(F32))
        if d == 0:
            qb_s[sl, :] = q.astype(BF16)
        kb_s[d][sl, :] = kin.astype(BF16)
        lf = jnp.concatenate([logf2[:ch], logf2[ch:]], axis=1)
        hi = lf.astype(BF16)
        r1 = lf - hi.astype(F32)
        mid = r1.astype(BF16)
        lo = (r1 - mid.astype(F32)).astype(BF16)
        rel2 = _dot(sums_ref[d], jnp.concatenate([hi, mid, lo, jnp.zeros_like(hi)], axis=0))
        for half in range(2):
            rows = slice(half * ch, (half + 1) * ch)
            rel = rel2[:, half * dk:(half + 1) * dk]
            qh, kh = q[rows], kin[rows]
            c = pi * 2 + half
            for li in range(n_lv):
                e = jnp.exp2(rel[li * ch:(li + 1) * ch])
                w_s[d][pi, li, rows, :] = (jnp.where(laters[d][li], qh, kh) * e).astype(BF16)
            bcum = rel[n_lv * ch:(n_lv + 1) * ch]
            tot = bcum[ch - 1:ch] if d == 0 else bcum[0:1]
            hs = pl.ds(r0 + half * ch, ch)
            qd_s[d][hs, :] = (qh * jnp.exp2(bcum)).astype(BF16)
            kd_s[d][hs, :] = (kh * jnp.exp2(tot - bcum)).astype(BF16)
            et_s[d][c] = jnp.broadcast_to(jnp.exp2(tot), (8, dk))

    def intra(d, pi):
        sl = pl.ds(pl.multiple_of(pi * 2 * ch, 2 * ch), 2 * ch)
        att = pairs_ref[d, n_lv] * _dot_nt(qb_s[sl, :], kb_s[d][sl, :])
        for li in range(n_lv):
            w = w_s[d][pi, li]
            att = att + pairs_ref[d, li] * _dot_nt(w, w)
        att_s[d][sl, :] = att.astype(BF16)
        for half in range(2):
            hs = pl.ds(pl.multiple_of(pi * 2 * ch, 2 * ch) + half * ch, ch)
            p_s[d][pi * 2 + half] = _dot_tn(i_ref[hs, :], kd_s[d][hs, :])

    def scan(d, j, state_t):
        pi = _chunk_index(j, d, n_lat_chunks // 2, n_pairs)
        r0 = pl.multiple_of(pi * 2 * ch, 2 * ch)
        sl = pl.ds(r0, 2 * ch)
        o_intra = _dot(att_s[d][sl, :], i_ref[sl, :])
        o_halves = [None, None]
        for half in ((0, 1) if d == 0 else (1, 0)):
            hs = pl.ds(r0 + half * ch, ch)
            o_halves[half] = (o_intra[half * ch:(half + 1) * ch]
                              + _dot_nt(qd_s[d][hs, :], state_t.astype(BF16)))
            c = pi * 2 + half
            state_t = state_t * et_s[d][c][0:1] + p_s[d][c]
        o = jnp.concatenate(o_halves, axis=0)
        if d == 0:
            ys[sl, :] = o
        else:
            o = ys[sl, :] + o
            o_ref[sl, :] = (_rms(o) * nw_ref[...] * _silu(g_ref[sl, :].astype(F32))).astype(BF16)
        return state_t

    zero_state = jnp.zeros((dv, dk), F32)

    def stage_a(pi, carry):
        operands(0, pi)
        return carry

    def stage_b(pi, carry):
        intra(0, pi)
        operands(1, pi)
        return carry

    def stage_c(j, state_t):
        intra(1, j)
        return scan(0, j, state_t)

    lax.fori_loop(0, n_pairs, stage_a, 0, unroll=6)
    lax.fori_loop(0, n_pairs, stage_b, 0, unroll=6)
    lax.fori_loop(0, n_pairs, stage_c, zero_state, unroll=9)
    lax.fori_loop(0, n_pairs, functools.partial(scan, 1), zero_state, unroll=9)


def _hgrn(u, lb, nw, n_lat, n_ctx):
    b, lt, _ = u.shape
    w = 128
    ch = HGRN_CHUNK
    n_chunks = lt // ch
    sums, pairs = _hgrn_consts()
    head = pl.BlockSpec((None, 1, w), lambda b, h: (h, 0, 0))
    return pl.pallas_call(
        functools.partial(_hgrn_kernel, n_lat=n_lat, n_ctx=n_ctx),
        grid=(b, HGRN_HEADS),
        in_specs=[pl.BlockSpec((None, lt, 5 * w), lambda b, h: (b, 0, 2560 // (5 * w) + h)), head, head,
                  pl.BlockSpec(sums.shape, lambda b, h: (0, 0, 0)),
                  pl.BlockSpec(pairs.shape, lambda b, h: (0, 0, 0, 0))],
        out_specs=pl.BlockSpec((None, lt, w), lambda b, h: (b, 0, h)),
        out_shape=jax.ShapeDtypeStruct((b, lt, HGRN_HEADS * w), BF16),
        scratch_shapes=[
            pltpu.VMEM((lt, w), F32), pltpu.VMEM((lt, w), BF16),
            *([pltpu.VMEM((n_chunks // 2, len(_HGRN_LEVELS), 2 * ch, w), BF16),
               pltpu.VMEM((lt, w), BF16), pltpu.VMEM((lt, w), BF16), pltpu.VMEM((lt, w), BF16),
               pltpu.VMEM((n_chunks, 8, w), F32), pltpu.VMEM((lt, 2 * ch), BF16),
               pltpu.VMEM((n_chunks, w, w), F32)] * 2),
        ],
        compiler_params=_cparams("parallel", "parallel"),
        name="hgrn2",
    )(u, lb, nw, sums, pairs)


def _mlstm_kernel(u_ref, gt_ref, cwq_ref, cwk_ref, cbq_ref, cbk_ref, gb_ref, nw_ref, tri_ref,
                  o_ref, qst, ks, vst, gst, yst, cst, *, n_lat, n_ctx):
    ch = MLSTM_CHUNK
    n_chunks = (n_lat + n_ctx) // ch
    n_lat_chunks = n_lat // ch
    dqk = cwq_ref.shape[1]
    dv = nw_ref.shape[1]
    q_ref, k_ref = u_ref.at[:, 0:dqk], u_ref.at[:, dqk:2 * dqk]
    v_ref, og_ref = u_ref.at[:, 2 * dqk:2 * dqk + dv], u_ref.at[:, 2 * dqk + dv:]
    k_scale = dqk ** -0.5

    def conv_piece(r0, rows, period):
        sl = pl.ds(r0, rows)
        c0 = r0 // ch
        q = _silu(_conv3(q_ref[sl, :].astype(F32), cwq_ref[...], cbq_ref[...], period))
        ks[sl, :] = (_silu(_conv3(k_ref[sl, :].astype(F32), cwk_ref[...], cbk_ref[...], period)) * k_scale).astype(BF16)
        raw = gt_ref[sl, :] + gb_ref[...]
        lane = lax.broadcasted_iota(jnp.int32, raw.shape, 1)
        g = jnp.where(lane < 2, raw, _log_sigmoid(raw))
        v = v_ref[sl, :].astype(F32)
        for k in range(rows // ch):
            rk = slice(k * ch, (k + 1) * ch)
            for j in range(dqk // ch):
                qst[c0 + k, j * ch:(j + 1) * ch, :] = q[rk, j * ch:(j + 1) * ch].T.astype(BF16)
            for j in range(dv // ch):
                vst[c0 + k, j * ch:(j + 1) * ch, :] = v[rk, j * ch:(j + 1) * ch].T.astype(BF16)
            vst[c0 + k, dv:, :] = jnp.ones((MLSTM_ONES_ROWS, ch), BF16)
            gst[c0 + k] = g[rk].T

    _conv_rows(n_lat, n_ctx, conv_piece)

    si = lax.broadcasted_iota(jnp.int32, (ch, ch), 0)
    ti = lax.broadcasted_iota(jnp.int32, (ch, ch), 1)

    for d in range(2):
        mask = (si <= ti) if d == 0 else (si >= ti)
        cst[...] = jnp.zeros_like(cst)

        def body(j, m_prev, d=d, mask=mask):
            c = _chunk_index(j, d, n_lat_chunks, n_chunks)
            sl = pl.ds(pl.multiple_of(c * ch, ch), ch)
            g_t = gst[c]
            cum_t = _dot01_rows(g_t, tri_ref[d])
            brow = cum_t[2 + d:3 + d, :]
            irow = g_t[d:d + 1, :]
            tot = brow[:, ch - 1:ch] if d == 0 else brow[:, 0:1]
            logd = jnp.where(mask, brow + jnp.broadcast_to(irow - brow, (ch, ch)).T, -jnp.inf)
            gstate = brow + m_prev
            mt = jnp.maximum(jnp.max(logd, axis=0, keepdims=True), gstate)
            q_t = qst[c]
            kc = ks[sl, :]
            w = _dot(kc, q_t) * jnp.exp(logd - mt)
            sw = jnp.exp(gstate - mt)
            v_t = vst[c]
            state = cst[...]
            qstate = _dot(state.astype(BF16), q_t)
            num = _dot(v_t[:dv], w.astype(BF16)) + sw * qstate[:dv]
            den = jnp.sum(w, axis=0, keepdims=True) + sw * qstate[dv:dv + 1]
            hout = num * (1.0 / jnp.maximum(jnp.abs(den), jnp.exp(-mt)))
            logw = tot - brow + irow
            m_new = jnp.maximum(tot + m_prev, jnp.max(logw, axis=1, keepdims=True))
            ws = jnp.exp(logw - m_new).astype(BF16)
            cst[...] = jnp.exp(tot + m_prev - m_new) * state + _dot(v_t * ws, kc)
            if d == 0:
                yst[c] = hout
            else:
                hh_t = yst[c] + hout
                hh = jnp.concatenate([hh_t[i * ch:(i + 1) * ch].T for i in range(dv // ch)], axis=1)
                o_ref[sl, :] = (_rms(hh) * nw_ref[...] * jax.nn.sigmoid(og_ref[sl, :].astype(F32))).astype(BF16)
            return m_new

        lax.fori_loop(0, n_chunks, body, jnp.zeros((1, 1), F32), unroll=3)


def _mlstm(u, us, conv_w, conv_b, gate_b, nw, n_lat, n_ctx):
    b, lt, _ = u.shape
    dqk, dv = 256, 512
    nh = MLSTM_HEADS
    ch = MLSTM_CHUNK
    nc = lt // ch
    upper = np.triu(np.ones((ch, ch), np.float32))
    tri = np.stack([upper, upper.T])
    tri = jnp.asarray(np.concatenate([tri, tri, tri], axis=1), BF16)
    par = lambda r, off: pl.BlockSpec((r, dqk), lambda b, h, off=off: (0, off + h))
    return pl.pallas_call(
        functools.partial(_mlstm_kernel, n_lat=n_lat, n_ctx=n_ctx),
        grid=(b, nh),
        in_specs=[
            pl.BlockSpec((None, lt, 2 * dqk + 2 * dv), lambda b, h: (b, 0, h)),
            pl.BlockSpec((None, lt, 128), lambda b, h: (b, 0, h)),
            par(3, 0), par(3, nh), par(1, 0), par(1, nh),
            pl.BlockSpec((None, 1, 128), lambda b, h: (h, 0, 0)),
            pl.BlockSpec((None, 1, dv), lambda b, h: (h, 0, 0)),
            pl.BlockSpec(tri.shape, lambda b, h: (0, 0, 0)),
        ],
        out_specs=pl.BlockSpec((None, lt, dv), lambda b, h: (b, 0, h)),
        out_shape=jax.ShapeDtypeStruct((b, lt, nh * dv), BF16),
        scratch_shapes=[
            pltpu.VMEM((nc, dqk, ch), BF16), pltpu.VMEM((lt, dqk), BF16),
            pltpu.VMEM((nc, dv + MLSTM_ONES_ROWS, ch), BF16), pltpu.VMEM((nc, 128, ch), F32),
            pltpu.VMEM((nc, dv, ch), F32), pltpu.VMEM((dv + MLSTM_ONES_ROWS, dqk), F32),
        ],
        compiler_params=_cparams("parallel", "parallel"),
        name="mlstm",
    )(u, us, conv_w, conv_w, conv_b, conv_b, gate_b, nw, tri)


def _pad_lanes(a, width=128):
    return jnp.pad(a, [(0, 0)] * (a.ndim - 1) + [(0, width - a.shape[-1])])


def _even_params(w_in, dt_bias, a_log, d_skip):
    heads = a_log.shape[1]
    hg = heads // SSD_GROUPS
    w_hgrn = w_in[:, 2560 + 2 * heads:].reshape(w_in.shape[0], 5, HGRN_HEADS, -1).swapaxes(1, 2)
    grp = lambda a, b: w_in[:, a:b].reshape(w_in.shape[0], SSD_GROUPS, -1)
    w_ssd = jnp.concatenate([grp(1024, 2048), grp(2048, 2304), grp(2304, 2560), grp(0, 1024)], axis=2)
    w_main = jnp.concatenate([w_ssd.reshape(w_in.shape[0], -1), w_hgrn.reshape(w_in.shape[0], -1)], axis=1).astype(BF16)
    dt_w = w_in[:, 2560:2560 + 2 * heads]
    per_group = lambda a: [_pad_lanes(jnp.concatenate([a[..., g * hg:(g + 1) * hg], a[..., heads + g * hg:heads + (g + 1) * hg]], axis=-1))
                           for g in range(SSD_GROUPS)]
    w_small = jnp.concatenate(per_group(dt_w), axis=1).astype(BF16)
    flat = lambda a: a.reshape(1, 2 * heads)
    dtb = jnp.stack(per_group(flat(dt_bias)))
    alog = jnp.stack(per_group(flat(a_log)))
    dsk = jnp.broadcast_to(jnp.repeat(d_skip, SSD_HEAD_DIM).reshape(SSD_GROUPS, hg * SSD_HEAD_DIM, 1),
                           (SSD_GROUPS, hg * SSD_HEAD_DIM, 128))
    return w_main, w_small, dtb, alog, dsk


def _odd_params(w_in, gate_b):
    nh = MLSTM_HEADS
    parts = [w_in[:, a:b].reshape(w_in.shape[0], nh, -1) for a, b in ((0, 1024), (1024, 2048), (2048, 4096), (4096, 6144))]
    w_main = jnp.concatenate(parts, axis=2).reshape(w_in.shape[0], -1).astype(BF16)
    gw = w_in[:, 6144:]
    w_small = jnp.concatenate([_pad_lanes(gw[:, h::nh]) for h in range(nh)], axis=1).astype(BF16)
    gb = jnp.stack([_pad_lanes(gate_b[:, h].reshape(1, 4)) for h in range(nh)])
    return w_main, w_small, gb


def kernel(x, c, ctx, c_ctx, mod_w, mod_b, norm_w, final_norm_w, mlp_w1, mlp_w2, even_w_in, even_w_out, ssd_conv_w, ssd_conv_b, ssd_a_log, ssd_dt_bias, ssd_d, ssd_norm_w, hgrn_lb, hgrn_norm_w, odd_w_in, odd_w_out, mlstm_conv_w, mlstm_conv_b, mlstm_gate_b, mlstm_norm_w):
    bsz, n_lat, d = x.shape
    n_ctx = ctx.shape[1]
    depth = mod_w.shape[0]
    lt = n_lat + n_ctx
    assert bsz < C_ROWS and n_lat % CONV_PIECE == 0 and n_ctx % SSD_CHUNK == 0 and n_ctx & (n_ctx - 1) == 0
    ctx_row = bsz

    c_all = jnp.zeros((C_ROWS, d), F32).at[:bsz].set(c).at[ctx_row].set(c_ctx)
    mods = _modulation(c_all, mod_w, mod_b).reshape(depth, C_ROWS, N_MOD, 1, d)

    lb_all = jnp.cumsum(jax.nn.softmax(hgrn_lb.astype(F32), axis=0), axis=0)
    lb_all = lb_all - lb_all[0]

    tm_in = lt // 2
    tm_full = lt // 4
    tm_lat = min(512, n_lat)

    mlp_w1_b, mlp_w2_b = mlp_w1.astype(BF16), mlp_w2.astype(BF16)
    even_w_out_b, odd_w_out_b = even_w_out.astype(BF16), odd_w_out.astype(BF16)

    xx = jnp.concatenate([x, ctx], axis=1)
    for layer in range(depth):
        last = layer == depth - 1
        nw1 = norm_w[layer, 0].reshape(1, d)
        nw2 = norm_w[layer, 1].reshape(1, d)
        if layer % 2 == 0:
            e = layer // 2
            w_main, w_small, dtb, alog, dsk = _even_params(even_w_in[e], ssd_dt_bias[e], ssd_a_log[e], ssd_d[e])
            u, us = _inproj(xx, nw1, mods, layer, ctx_row, w_main, w_small, n_lat, tm_in, 1536)
            ya = _ssd(u, us, ssd_conv_w[e], ssd_conv_b[e].reshape(1, -1), dtb, alog, dsk,
                      ssd_norm_w[e].reshape(SSD_GROUPS, 1, -1), n_lat, n_ctx)
            yb = _hgrn(u, lb_all[e].reshape(HGRN_HEADS, 1, -1), hgrn_norm_w[e].reshape(HGRN_HEADS, 1, -1), n_lat, n_ctx)
            ys, w_out, w_idx = [ya, yb], even_w_out_b, e
        else:
            o = layer // 2
            w_main, w_small, gb = _odd_params(odd_w_in[o], mlstm_gate_b[o])
            u, us = _inproj(xx, nw1, mods, layer, ctx_row, w_main, w_small, n_lat, tm_in, 1024)
            yc = _mlstm(u, us, mlstm_conv_w[o], mlstm_conv_b[o].reshape(1, -1), gb,
                        mlstm_norm_w[o].reshape(MLSTM_HEADS, 1, -1), n_lat, n_ctx)
            ys, w_out, w_idx = [yc], odd_w_out_b, o
        rows, tm = (n_lat, tm_lat) if last else (lt, tm_full)
        xx = _outproj(ys, w_out, w_idx, xx, mods, layer, ctx_row, n_lat, rows, tm // 2 if not last else tm)
        xx = _mlp(xx, nw2, mods, layer, ctx_row, mlp_w1_b, mlp_w2_b,
                  final_norm_w.reshape(1, d), n_lat, rows, tm, 1024, last)
    return xx
```

```python
import functools

import numpy as np
import jax
import jax.numpy as jnp
from jax import lax
from jax.experimental import pallas as pl
from jax.experimental.pallas import tpu as pltpu

F32 = jnp.float32
BF16 = jnp.bfloat16

EPS = 1e-6
GRID_W = 64
N_MOD = 6
HGRN_LOG2_FLOOR = -1e5
C_ROWS = 32

V7X_VMEM_BYTES = 64 * 1024 * 1024
VMEM_LIMIT = V7X_VMEM_BYTES - 8 * 1024 * 1024

SSD_CHUNK = 128
SSD_HEAD_DIM = 64
SSD_STATE = 128
SSD_GROUPS = 2
HGRN_CHUNK = 64
HGRN_HEADS = 8
MLSTM_CHUNK = 128
MLSTM_HEADS = 4
MLSTM_DQK = 256
MLSTM_DV = 512
MLSTM_ONES_ROWS = 16
LANES = 128
SSD_GROUP_WIDTH = 8 * SSD_HEAD_DIM
SSD_GROUP_LANES = 2 * SSD_GROUP_WIDTH + 2 * SSD_STATE
HGRN_HEAD_LANES = 5 * LANES
CONV_PIECE = 256
NORM_ROWS = 16


def _cparams(*sem):
    return pltpu.CompilerParams(dimension_semantics=sem, vmem_limit_bytes=VMEM_LIMIT)


def _silu(x):
    return x * jax.nn.sigmoid(x)


def _softplus(x):
    return jnp.maximum(x, 0.0) + jnp.log(1.0 + jnp.exp(-jnp.abs(x)))


def _log_sigmoid(x):
    return -_softplus(-x)


def _dot(a, b):
    return jnp.dot(a, b, preferred_element_type=F32)


def _dot_nt(a, b):
    return lax.dot_general(a, b, (((1,), (1,)), ((), ())), preferred_element_type=F32)


def _dot_tn(a, b):
    return lax.dot_general(a, b, (((0,), (0,)), ((), ())), preferred_element_type=F32)


def _dot01(m01x3, x):
    hi = x.astype(BF16)
    r = x - hi.astype(F32)
    mid = r.astype(BF16)
    lo = (r - mid.astype(F32)).astype(BF16)
    return _dot(m01x3, jnp.concatenate([hi, mid, lo], axis=0))


def _dot01_rows(x, m01x3):
    hi = x.astype(BF16)
    r = x - hi.astype(F32)
    mid = r.astype(BF16)
    lo = (r - mid.astype(F32)).astype(BF16)
    return _dot(jnp.concatenate([hi, mid, lo], axis=1), m01x3)


def _rms(x):
    return x * lax.rsqrt(jnp.mean(x * x, axis=-1, keepdims=True) + EPS)


def _conv3(u, w, b, period):
    rows = u.shape[0]
    t = lax.broadcasted_iota(jnp.int32, u.shape, 0) & (period - 1)
    left = jnp.where(t == 0, 0.0, pltpu.roll(u, 1, axis=0))
    right = jnp.where(t == period - 1, 0.0, pltpu.roll(u, rows - 1, axis=0))
    return left * w[0:1] + u * w[1:2] + right * w[2:3] + b


def _modulated_norm(x_ref, h_ref, nw, sh_l, sc_l, sh_c, sc_c, row0, n_lat):
    a_l, a_c = nw * (1.0 + sc_l), nw * (1.0 + sc_c)

    def block(r, carry):
        start = pl.multiple_of(r * NORM_ROWS, NORM_ROWS)
        rs = pl.ds(start, NORM_ROWS)
        is_ctx = row0 + start >= n_lat
        h_ref[rs, :] = (_rms(x_ref[rs, :]) * jnp.where(is_ctx, a_c, a_l) + jnp.where(is_ctx, sh_c, sh_l)).astype(BF16)
        return carry

    lax.fori_loop(0, x_ref.shape[0] // NORM_ROWS, block, 0, unroll=4)


def _mod_kernel(c_ref, w_ref, b_ref, o_ref):
    a = _silu(c_ref[...]).astype(BF16)
    o_ref[...] = _dot(a, w_ref[...].astype(BF16)) + b_ref[...]


def _modulation(c_all, mod_w, mod_b):
    depth, d, n = mod_w.shape
    tn = 1024
    return pl.pallas_call(
        _mod_kernel,
        grid=(depth, n // tn),
        in_specs=[
            pl.BlockSpec((C_ROWS, d), lambda l, j: (0, 0)),
            pl.BlockSpec((None, d, tn), lambda l, j: (l, 0, j)),
            pl.BlockSpec((None, 1, tn), lambda l, j: (l, 0, j)),
        ],
        out_specs=pl.BlockSpec((None, C_ROWS, tn), lambda l, j: (l, 0, j)),
        out_shape=jax.ShapeDtypeStruct((depth, C_ROWS, n), F32),
        compiler_params=_cparams("parallel", "parallel"),
        name="modulation",
    )(c_all, mod_w, mod_b.reshape(depth, 1, n))


def _mod_specs(layer, ctx_row, ks, d, nargs):
    specs = []
    for k in ks:
        if nargs == 3:
            specs.append(pl.BlockSpec((None, None, None, 1, d), lambda b, i, j, k=k: (layer, b, k, 0, 0)))
            specs.append(pl.BlockSpec((None, None, None, 1, d), lambda b, i, j, k=k: (layer, ctx_row, k, 0, 0)))
        else:
            specs.append(pl.BlockSpec((None, None, None, 1, d), lambda b, i, k=k: (layer, b, k, 0, 0)))
            specs.append(pl.BlockSpec((None, None, None, 1, d), lambda b, i, k=k: (layer, ctx_row, k, 0, 0)))
    return specs


def _inproj_kernel(x_ref, nw_ref, shl_ref, shc_ref, scl_ref, scc_ref, w_ref, ws_ref, o_ref, os_ref, h_ref, *, n_lat):
    @pl.when(pl.program_id(2) == 0)
    def _():
        _modulated_norm(x_ref, h_ref, nw_ref[...], shl_ref[...], scl_ref[...], shc_ref[...], scc_ref[...],
                        pl.program_id(1) * x_ref.shape[0], n_lat)
        os_ref[...] = _dot(h_ref[...], ws_ref[...])

    o_ref[...] = _dot(h_ref[...], w_ref[...]).astype(o_ref.dtype)


def _inproj(x, nw, mods, layer, ctx_row, w_main, w_small, n_lat, tm, tn):
    b, lt, d = x.shape
    n = w_main.shape[1]
    ns = w_small.shape[1]
    return pl.pallas_call(
        functools.partial(_inproj_kernel, n_lat=n_lat),
        grid=(b, lt // tm, n // tn),
        in_specs=[
            pl.BlockSpec((None, tm, d), lambda b, i, j: (b, i, 0)),
            pl.BlockSpec((1, d), lambda b, i, j: (0, 0)),
            *_mod_specs(layer, ctx_row, (0, 1), d, 3),
            pl.BlockSpec((d, tn), lambda b, i, j: (0, j)),
            pl.BlockSpec((d, ns), lambda b, i, j: (0, 0)),
        ],
        out_specs=[
            pl.BlockSpec((None, tm, tn), lambda b, i, j: (b, i, j)),
            pl.BlockSpec((None, tm, ns), lambda b, i, j: (b, i, 0)),
        ],
        out_shape=[jax.ShapeDtypeStruct((b, lt, n), BF16), jax.ShapeDtypeStruct((b, lt, ns), F32)],
        scratch_shapes=[pltpu.VMEM((tm, d), BF16)],
        compiler_params=_cparams("parallel", "parallel", "arbitrary"),
        name="inproj",
    )(x, nw, mods, mods, mods, mods, w_main, w_small)


def _outproj_kernel(*refs, n_y, n_lat):
    y_refs, w_refs = refs[:n_y], refs[n_y:2 * n_y]
    x_ref, gl_ref, gc_ref, o_ref = refs[2 * n_y:]
    acc = _dot(y_refs[0][...], w_refs[0][...])
    for y_ref, w_ref in zip(y_refs[1:], w_refs[1:]):
        acc = acc + _dot(y_ref[...], w_ref[...])
    tm = x_ref.shape[0]
    rows = pl.program_id(1) * tm + lax.broadcasted_iota(jnp.int32, (tm, 1), 0)
    g = jnp.where(rows >= n_lat, gc_ref[...], gl_ref[...])
    o_ref[...] = x_ref[...] + g * acc


def _outproj(ys, w, w_idx, x, mods, layer, ctx_row, n_lat, rows, tm):
    b, lt, d = x.shape
    n_y = len(ys)
    y_specs = [pl.BlockSpec((None, tm, y.shape[2]), lambda b, i: (b, i, 0)) for y in ys]
    w_specs = [pl.BlockSpec((None, y.shape[2], d), lambda b, i, k=k: (w_idx, k, 0)) for k, y in enumerate(ys)]
    return pl.pallas_call(
        functools.partial(_outproj_kernel, n_y=n_y, n_lat=n_lat),
        grid=(b, rows // tm),
        in_specs=[*y_specs, *w_specs,
                  pl.BlockSpec((None, tm, d), lambda b, i: (b, i, 0)),
                  *_mod_specs(layer, ctx_row, (2,), d, 2)],
        out_specs=pl.BlockSpec((None, tm, d), lambda b, i: (b, i, 0)),
        out_shape=jax.ShapeDtypeStruct((b, rows, d), F32),
        compiler_params=_cparams("parallel", "parallel"),
        name="outproj",
    )(*ys, *([w] * n_y), x, mods, mods)


def _mlp_kernel(x_ref, nw_ref, shl_ref, shc_ref, scl_ref, scc_ref, gl_ref, gc_ref, w1_ref, w2_ref, fw_ref,
                o_ref, h_ref, acc_ref, *, n_lat, final_norm):
    f = pl.program_id(2)
    tm = x_ref.shape[0]

    @pl.when(f == 0)
    def _():
        _modulated_norm(x_ref, h_ref, nw_ref[...], shl_ref[...], scl_ref[...], shc_ref[...], scc_ref[...],
                        pl.program_id(1) * tm, n_lat)
        acc_ref[...] = jnp.zeros_like(acc_ref)

    a = jnp.maximum(_dot(h_ref[...], w1_ref[...]), 0.0)
    acc_ref[...] += _dot((a * a).astype(BF16), w2_ref[...])

    @pl.when(f == pl.num_programs(2) - 1)
    def _():
        rows = pl.program_id(1) * tm + lax.broadcasted_iota(jnp.int32, (tm, 1), 0)
        g = jnp.where(rows >= n_lat, gc_ref[...], gl_ref[...])
        y = x_ref[...] + g * acc_ref[...]
        if final_norm:
            y = _rms(y) * fw_ref[...]
        o_ref[...] = y


def _mlp(x, nw, mods, layer, ctx_row, w1, w2, fw, n_lat, rows, tm, tf, final_norm):
    b, _, d = x.shape
    dff = w1.shape[2]
    return pl.pallas_call(
        functools.partial(_mlp_kernel, n_lat=n_lat, final_norm=final_norm),
        grid=(b, rows // tm, dff // tf),
        in_specs=[
            pl.BlockSpec((None, tm, d), lambda b, i, j: (b, i, 0)),
            pl.BlockSpec((1, d), lambda b, i, j: (0, 0)),
            *_mod_specs(layer, ctx_row, (3, 4, 5), d, 3),
            pl.BlockSpec((None, d, tf), lambda b, i, j: (layer, 0, j)),
            pl.BlockSpec((None, tf, d), lambda b, i, j: (layer, j, 0)),
            pl.BlockSpec((1, d), lambda b, i, j: (0, 0)),
        ],
        out_specs=pl.BlockSpec((None, tm, d), lambda b, i, j: (b, i, 0)),
        out_shape=jax.ShapeDtypeStruct((b, rows, d), F32),
        scratch_shapes=[pltpu.VMEM((tm, d), BF16), pltpu.VMEM((tm, d), F32)],
        compiler_params=_cparams("parallel", "parallel", "arbitrary"),
        name="mlp",
    )(x, nw, mods, mods, mods, mods, mods, mods, w1, w2, fw)


def _tri_consts(c):
    lower = np.tril(np.ones((c, c), np.float32))
    tri = np.stack([lower, lower.T])
    return jnp.asarray(np.concatenate([tri, tri, tri], axis=2), BF16)


def _chunk_index(j, d, n_lat_chunks, n_chunks):
    if d == 0:
        c = j + n_lat_chunks
        return jnp.where(c >= n_chunks, c - n_chunks, c)
    return n_chunks - 1 - j


def _conv_rows(n_lat, n_ctx, fn):
    def body(p, carry):
        fn(pl.multiple_of(p * CONV_PIECE, CONV_PIECE), CONV_PIECE, GRID_W)
        return carry
    lax.fori_loop(0, n_lat // CONV_PIECE, body, 0)
    fn(n_lat, n_ctx, n_ctx)


def _ssd_kernel(u_ref, dt_ref, cwx_ref, cwb_ref, cwc_ref, cbx_ref, cbb_ref, cbc_ref,
                dtb_ref, alog_ref, dsk_ref, nw_ref, tri_ref, o_ref,
                xst, bs, cst, dtt, cums, cumt, ecum, eend, etot, yst, hst, *, n_lat, n_ctx):
    ch = SSD_CHUNK
    p = SSD_HEAD_DIM
    gw = cwx_ref.shape[1]
    ns = cwb_ref.shape[1]
    heads = gw // p
    x_ref, b_ref = u_ref.at[:, 0:gw], u_ref.at[:, gw:gw + ns]
    c_ref, z_ref = u_ref.at[:, gw + ns:gw + 2 * ns], u_ref.at[:, gw + 2 * ns:]
    n_chunks = (n_lat + n_ctx) // ch
    n_lat_chunks = n_lat // ch

    def conv_piece(r0, rows, period):
        sl = pl.ds(r0, rows)
        c0 = r0 // ch
        x = _silu(_conv3(x_ref[sl, :].astype(F32), cwx_ref[...], cbx_ref[...], period))
        c = _silu(_conv3(c_ref[sl, :].astype(F32), cwc_ref[...], cbc_ref[...], period))
        bs[sl, :] = _silu(_conv3(b_ref[sl, :].astype(F32), cwb_ref[...], cbb_ref[...], period)).astype(BF16)
        dt = _softplus(dt_ref[sl, :] + dtb_ref[...])
        a = -jnp.exp(alog_ref[...]) * dt
        nd = 2 * heads
        for k in range(rows // ch):
            rk = slice(k * ch, (k + 1) * ch)
            for j in range(gw // ch):
                xst[c0 + k, j * ch:(j + 1) * ch, :] = x[rk, j * ch:(j + 1) * ch].T
            cst[c0 + k] = c[rk].T.astype(BF16)
            dt_t = dt[rk].T
            dtt[c0 + k] = dt_t[0:nd]
            for d in range(2):
                cum = _dot01(tri_ref[d], a[rk])
                cum_t = cum.T
                tot_c = cum_t[:, ch - 1:ch] if d == 0 else cum_t[:, 0:1]
                cums[d, c0 + k] = cum
                cumt[d, c0 + k] = cum_t[0:nd]
                ecum[d, c0 + k] = jnp.exp(cum_t)[0:nd]
                eend[d, c0 + k] = (jnp.exp(tot_c - cum_t) * dt_t)[0:nd]
                etot[d, c0 + k] = jnp.broadcast_to(jnp.exp(cum[ch - 1:ch, :] if d == 0 else cum[0:1, :]), (8, ch))

    _conv_rows(n_lat, n_ctx, conv_piece)

    si = lax.broadcasted_iota(jnp.int32, (ch, ch), 0)
    ti = lax.broadcasted_iota(jnp.int32, (ch, ch), 1)

    for d in range(2):
        mask = (si <= ti) if d == 0 else (si >= ti)
        hst[...] = jnp.zeros_like(hst)

        def body(j, carry, d=d, mask=mask):
            c = _chunk_index(j, d, n_lat_chunks, n_chunks)
            sl = pl.ds(pl.multiple_of(c * ch, ch), ch)
            cum, cum_t = cums[d, c], cumt[d, c]
            e_cum_t, e_end_t, e_tot = ecum[d, c], eend[d, c], etot[d, c][0:1]
            dt_t = dtt[c]
            bc = bs[sl, :]
            cc_t = cst[c]
            cb_t = _dot(bc, cc_t)
            for h in range(heads):
                col = d * heads + h
                hs = slice(h * p, (h + 1) * p)
                diff = cum_t[col:col + 1, :] - cum[:, col:col + 1]
                decay = jnp.exp(jnp.where(mask, diff, -jnp.inf))
                xh = xst[c, hs, :]
                state = hst[h]
                y = _dot((xh * dt_t[col:col + 1, :]).astype(BF16), (cb_t * decay).astype(BF16))
                y = y + _dot(state.astype(BF16), cc_t) * e_cum_t[col:col + 1, :]
                xw = (xh * e_end_t[col:col + 1, :]).astype(BF16)
                hst[h] = state * e_tot[:, col:col + 1] + _dot(xw, bc)
                if d == 0:
                    yst[c, hs, :] = y
                else:
                    yst[c, hs, :] = yst[c, hs, :] + y
            if d == 1:
                g_t = yst[c] + xst[c] * dsk_ref[...]
                g = jnp.concatenate([g_t[j * ch:(j + 1) * ch].T for j in range(gw // ch)], axis=1)
                g = g * _silu(z_ref[sl, :].astype(F32))
                o_ref[sl, :] = (_rms(g) * nw_ref[...]).astype(BF16)
            return carry

        lax.fori_loop(0, n_chunks, body, 0, unroll=6)


def _ssd(u, us, conv_w, conv_b, dtb, alog, dsk, nw, n_lat, n_ctx):
    b, lt, _ = u.shape
    gw = SSD_GROUP_WIDTH
    ns = SSD_STATE
    nc = lt // SSD_CHUNK
    nd = 2 * gw // SSD_HEAD_DIM
    cx_blk, cb_blk, cc_blk = 0, SSD_GROUPS * gw // ns, SSD_GROUPS * (gw + ns) // ns
    par = lambda r, w, off: pl.BlockSpec((r, w), lambda b, g, off=off: (0, off + g))
    grp = lambda w: pl.BlockSpec((None, 1, w), lambda b, g: (g, 0, 0))
    return pl.pallas_call(
        functools.partial(_ssd_kernel, n_lat=n_lat, n_ctx=n_ctx),
        grid=(b, SSD_GROUPS),
        in_specs=[
            pl.BlockSpec((None, lt, SSD_GROUP_LANES), lambda b, g: (b, 0, g)),
            pl.BlockSpec((None, lt, LANES), lambda b, g: (b, 0, g)),
            par(3, gw, cx_blk), par(3, ns, cb_blk), par(3, ns, cc_blk),
            par(1, gw, cx_blk), par(1, ns, cb_blk), par(1, ns, cc_blk),
            grp(128), grp(128), pl.BlockSpec((None, gw, 128), lambda b, g: (g, 0, 0)), grp(gw),
            pl.BlockSpec((2, SSD_CHUNK, 3 * SSD_CHUNK), lambda b, g: (0, 0, 0)),
        ],
        out_specs=pl.BlockSpec((None, lt, gw), lambda b, g: (b, 0, g)),
        out_shape=jax.ShapeDtypeStruct((b, lt, SSD_GROUPS * gw), BF16),
        scratch_shapes=[
            pltpu.VMEM((nc, gw, SSD_CHUNK), F32), pltpu.VMEM((lt, ns), BF16), pltpu.VMEM((nc, ns, SSD_CHUNK), BF16),
            pltpu.VMEM((nc, nd, SSD_CHUNK), F32), pltpu.VMEM((2, nc, SSD_CHUNK, LANES), F32),
            pltpu.VMEM((2, nc, nd, SSD_CHUNK), F32), pltpu.VMEM((2, nc, nd, SSD_CHUNK), F32),
            pltpu.VMEM((2, nc, nd, SSD_CHUNK), F32), pltpu.VMEM((2, nc, 8, LANES), F32),
            pltpu.VMEM((nc, gw, SSD_CHUNK), F32), pltpu.VMEM((gw // SSD_HEAD_DIM, SSD_HEAD_DIM, ns), F32),
        ],
        compiler_params=_cparams("parallel", "parallel"),
        name="ssd",
    )(u, us, conv_w, conv_w, conv_w, conv_b, conv_b, conv_b, dtb, alog, dsk, nw, _tri_consts(SSD_CHUNK))


_HGRN_LEVELS = (32, 16, 8, 4, 2, 1)


def _hgrn_consts():
    c = HGRN_CHUNK
    sums = np.zeros((7, c, c), np.float32)
    pairs = np.zeros((7, c, c), np.float32)
    for li, m in enumerate(_HGRN_LEVELS):
        for t in range(c):
            beta = (t // (2 * m)) * 2 * m
            mid = beta + m
            if t >= mid:
                sums[li, t, mid:t + 1] = 1.0
                pairs[li, t, beta:mid] = 1.0
            else:
                sums[li, t, t + 1:mid] = 1.0
    sums[6] = np.tril(np.ones((c, c), np.float32))
    pairs[6] = np.eye(c, dtype=np.float32)
    sums = np.stack([sums, sums[:, ::-1, ::-1]]).reshape(2, 7 * c, c)
    sums = np.concatenate([sums, sums, sums, np.zeros_like(sums)], axis=2)
    pairs = np.stack([pairs, pairs[:, ::-1, ::-1]])
    zero = np.zeros_like(pairs)
    pairs = np.concatenate([np.concatenate([pairs, zero], axis=3), np.concatenate([zero, pairs], axis=3)], axis=2)
    return jnp.asarray(sums, BF16), jnp.asarray(pairs, F32)


def _hgrn_kernel(u_ref, lb_ref, nw_ref, sums_ref, pairs_ref, o_ref,
                 ys, qb_s, *dir_scratch, n_lat, n_ctx):
    dk = dv = u_ref.shape[1] // 5
    q_ref, ff_ref, fb_ref, i_ref, g_ref = (u_ref.at[:, k * dk:(k + 1) * dk] for k in range(5))
    w_s, kb_s, qd_s, kd_s, et_s, att_s, p_s = zip(dir_scratch[:7], dir_scratch[7:])
    ch = HGRN_CHUNK
    n_chunks = (n_lat + n_ctx) // ch
    n_lat_chunks = n_lat // ch
    n_pairs = n_chunks // 2
    n_lv = len(_HGRN_LEVELS)
    lb = lb_ref[...]
    one_m_lb = 1.0 - lb
    t_idx = lax.broadcasted_iota(jnp.int32, (ch, dk), 0)

    laters = [[((t_idx & m) != 0) == (d == 0) for m in _HGRN_LEVELS] for d in range(2)]

    def operands(d, pi):
        f_ref = ff_ref if d == 0 else fb_ref
        r0 = pl.multiple_of(pi * 2 * ch, 2 * ch)
        sl = pl.ds(r0, 2 * ch)
        gate = jax.nn.sigmoid(f_ref[sl, :].astype(F32))
        kin = one_m_lb * (1.0 - gate)
        logf2 = jnp.maximum(jnp.log2(lb + one_m_lb * gate), HGRN_LOG2_FLOOR)
        q = _silu(q_ref[sl, :].astype(F32))
        if d == 0:
            qb_s[sl, :] = q.astype(BF16)
        kb_s[d][sl, :] = kin.astype(BF16)
        lf = jnp.concatenate([logf2[:ch], logf2[ch:]], axis=1)
        hi = lf.astype(BF16)
        r1 = lf - hi.astype(F32)
        mid = r1.astype(BF16)
        lo = (r1 - mid.astype(F32)).astype(BF16)
        rel2 = _dot(sums_ref[d], jnp.concatenate([hi, mid, lo, jnp.zeros_like(hi)], axis=0))
        for half in range(2):
            rows = slice(half * ch, (half + 1) * ch)
            rel = rel2[:, half * dk:(half + 1) * dk]
            qh, kh = q[rows], kin[rows]
            c = pi * 2 + half
            for li in range(n_lv):
                e = jnp.exp2(rel[li * ch:(li + 1) * ch])
                w_s[d][pi, li, rows, :] = (jnp.where(laters[d][li], qh, kh) * e).astype(BF16)
            bcum = rel[n_lv * ch:(n_lv + 1) * ch]
            tot = bcum[ch - 1:ch] if d == 0 else bcum[0:1]
            hs = pl.ds(r0 + half * ch, ch)
            qd_s[d][hs, :] = (qh * jnp.exp2(bcum)).astype(BF16)
            kd_s[d][hs, :] = (kh * jnp.exp2(tot - bcum)).astype(BF16)
            et_s[d][c] = jnp.broadcast_to(jnp.exp2(tot), (8, dk))

    def intra(d, pi):
        sl = pl.ds(pl.multiple_of(pi * 2 * ch, 2 * ch), 2 * ch)
        att = pairs_ref[d, n_lv] * _dot_nt(qb_s[sl, :], kb_s[d][sl, :])
        for li in range(n_lv):
            w = w_s[d][pi, li]
            att = att + pairs_ref[d, li] * _dot_nt(w, w)
        att_s[d][sl, :] = att.astype(BF16)
        for half in range(2):
            hs = pl.ds(pl.multiple_of(pi * 2 * ch, 2 * ch) + half * ch, ch)
            p_s[d][pi * 2 + half] = _dot_tn(i_ref[hs, :], kd_s[d][hs, :])

    def scan(d, j, state_t):
        pi = _chunk_index(j, d, n_lat_chunks // 2, n_pairs)
        r0 = pl.multiple_of(pi * 2 * ch, 2 * ch)
        sl = pl.ds(r0, 2 * ch)
        o_intra = _dot(att_s[d][sl, :], i_ref[sl, :])
        o_halves = [None, None]
        for half in ((0, 1) if d == 0 else (1, 0)):
            hs = pl.ds(r0 + half * ch, ch)
            o_halves[half] = (o_intra[half * ch:(half + 1) * ch]
                              + _dot_nt(qd_s[d][hs, :], state_t.astype(BF16)))
            c = pi * 2 + half
            state_t = state_t * et_s[d][c][0:1] + p_s[d][c]
        o = jnp.concatenate(o_halves, axis=0)
        if d == 0:
            ys[sl, :] = o
        else:
            o = ys[sl, :] + o
            o_ref[sl, :] = (_rms(o) * nw_ref[...] * _silu(g_ref[sl, :].astype(F32))).astype(BF16)
        return state_t

    zero_state = jnp.zeros((dv, dk), F32)

    def stage_a(pi, carry):
        operands(0, pi)
        return carry

    def stage_b(pi, carry):
        intra(0, pi)
        operands(1, pi)
        return carry

    def stage_c(j, state_t):
        intra(1, j)
        return scan(0, j, state_t)

    lax.fori_loop(0, n_pairs, stage_a, 0, unroll=9)
    lax.fori_loop(0, n_pairs, stage_b, 0, unroll=9)
    lax.fori_loop(0, n_pairs, stage_c, zero_state, unroll=18)
    lax.fori_loop(0, n_pairs, functools.partial(scan, 1), zero_state, unroll=18)


def _hgrn(u, lb, nw, n_lat, n_ctx):
    b, lt, _ = u.shape
    w = LANES
    ch = HGRN_CHUNK
    n_chunks = lt // ch
    sums, pairs = _hgrn_consts()
    head = pl.BlockSpec((None, 1, w), lambda b, h: (h, 0, 0))
    return pl.pallas_call(
        functools.partial(_hgrn_kernel, n_lat=n_lat, n_ctx=n_ctx),
        grid=(b, HGRN_HEADS),
        in_specs=[pl.BlockSpec((None, lt, HGRN_HEAD_LANES),
                               lambda b, h: (b, 0, SSD_GROUPS * SSD_GROUP_LANES // HGRN_HEAD_LANES + h)), head, head,
                  pl.BlockSpec(sums.shape, lambda b, h: (0, 0, 0)),
                  pl.BlockSpec(pairs.shape, lambda b, h: (0, 0, 0, 0))],
        out_specs=pl.BlockSpec((None, lt, w), lambda b, h: (b, 0, h)),
        out_shape=jax.ShapeDtypeStruct((b, lt, HGRN_HEADS * w), BF16),
        scratch_shapes=[
            pltpu.VMEM((lt, w), F32), pltpu.VMEM((lt, w), BF16),
            *([pltpu.VMEM((n_chunks // 2, len(_HGRN_LEVELS), 2 * ch, w), BF16),
               pltpu.VMEM((lt, w), BF16), pltpu.VMEM((lt, w), BF16), pltpu.VMEM((lt, w), BF16),
               pltpu.VMEM((n_chunks, 8, w), F32), pltpu.VMEM((lt, 2 * ch), BF16),
               pltpu.VMEM((n_chunks, w, w), F32)] * 2),
        ],
        compiler_params=_cparams("parallel", "parallel"),
        name="hgrn2",
    )(u, lb, nw, sums, pairs)


def _mlstm_kernel(u_ref, gt_ref, cwq_ref, cwk_ref, cbq_ref, cbk_ref, gb_ref, nw_ref, tri_ref,
                  o_ref, qst, ks, vst, gst, bst, yst, cst, *, n_lat, n_ctx):
    ch = MLSTM_CHUNK
    n_chunks = (n_lat + n_ctx) // ch
    n_lat_chunks = n_lat // ch
    dqk = cwq_ref.shape[1]
    dv = nw_ref.shape[1]
    q_ref, k_ref = u_ref.at[:, 0:dqk], u_ref.at[:, dqk:2 * dqk]
    v_ref, og_ref = u_ref.at[:, 2 * dqk:2 * dqk + dv], u_ref.at[:, 2 * dqk + dv:]
    k_scale = dqk ** -0.5

    def conv_piece(r0, rows, period):
        sl = pl.ds(r0, rows)
        c0 = r0 // ch
        q = _silu(_conv3(q_ref[sl, :].astype(F32), cwq_ref[...], cbq_ref[...], period))
        ks[sl, :] = (_silu(_conv3(k_ref[sl, :].astype(F32), cwk_ref[...], cbk_ref[...], period)) * k_scale).astype(BF16)
        raw = gt_ref[sl, :] + gb_ref[...]
        lane = lax.broadcasted_iota(jnp.int32, raw.shape, 1)
        g = jnp.where(lane < 2, raw, _log_sigmoid(raw))
        v = v_ref[sl, :].astype(F32)
        for k in range(rows // ch):
            rk = slice(k * ch, (k + 1) * ch)
            for j in range(dqk // ch):
                qst[c0 + k, j * ch:(j + 1) * ch, :] = q[rk, j * ch:(j + 1) * ch].T.astype(BF16)
            for j in range(dv // ch):
                vst[c0 + k, j * ch:(j + 1) * ch, :] = v[rk, j * ch:(j + 1) * ch].T.astype(BF16)
            vst[c0 + k, dv:, :] = jnp.ones((MLSTM_ONES_ROWS, ch), BF16)
            g_t = g[rk].T
            gst[c0 + k] = g_t[0:8]
            for d in range(2):
                bst[d, c0 + k] = _dot01_rows(g_t, tri_ref[d])[0:8]

    _conv_rows(n_lat, n_ctx, conv_piece)

    si = lax.broadcasted_iota(jnp.int32, (ch, ch), 0)
    ti = lax.broadcasted_iota(jnp.int32, (ch, ch), 1)

    for d in range(2):
        mask = (si <= ti) if d == 0 else (si >= ti)
        cst[...] = jnp.zeros_like(cst)

        def body(j, m_prev, d=d, mask=mask):
            c = _chunk_index(j, d, n_lat_chunks, n_chunks)
            sl = pl.ds(pl.multiple_of(c * ch, ch), ch)
            brow = bst[d, c][2 + d:3 + d, :]
            irow = gst[c][d:d + 1, :]
            tot = brow[:, ch - 1:ch] if d == 0 else brow[:, 0:1]
            logd = jnp.where(mask, brow + jnp.broadcast_to(irow - brow, (ch, ch)).T, -jnp.inf)
            gstate = brow + m_prev
            mt = jnp.maximum(jnp.max(logd, axis=0, keepdims=True), gstate)
            q_t = qst[c]
            kc = ks[sl, :]
            w = _dot(kc, q_t) * jnp.exp(logd - mt)
            sw = jnp.exp(gstate - mt)
            v_t = vst[c]
            state = cst[...]
            qstate = _dot(state.astype(BF16), q_t)
            num = _dot(v_t[:dv], w.astype(BF16)) + sw * qstate[:dv]
            den = jnp.sum(w, axis=0, keepdims=True) + sw * qstate[dv:dv + 1]
            hout = num * (1.0 / jnp.maximum(jnp.abs(den), jnp.exp(-mt)))
            logw = tot - brow + irow
            m_new = jnp.maximum(tot + m_prev, jnp.max(logw, axis=1, keepdims=True))
            ws = jnp.exp(logw - m_new).astype(BF16)
            cst[...] = jnp.exp(tot + m_prev - m_new) * state + _dot(v_t * ws, kc)
            if d == 0:
                yst[c] = hout
            else:
                hh_t = yst[c] + hout
                hh = jnp.concatenate([hh_t[i * ch:(i + 1) * ch].T for i in range(dv // ch)], axis=1)
                o_ref[sl, :] = (_rms(hh) * nw_ref[...] * jax.nn.sigmoid(og_ref[sl, :].astype(F32))).astype(BF16)
            return m_new

        lax.fori_loop(0, n_chunks, body, jnp.zeros((1, 1), F32), unroll=6)


def _mlstm(u, us, conv_w, conv_b, gate_b, nw, n_lat, n_ctx):
    b, lt, _ = u.shape
    dqk, dv = MLSTM_DQK, MLSTM_DV
    nh = MLSTM_HEADS
    ch = MLSTM_CHUNK
    nc = lt // ch
    upper = np.triu(np.ones((ch, ch), np.float32))
    tri = np.stack([upper, upper.T])
    tri = jnp.asarray(np.concatenate([tri, tri, tri], axis=1), BF16)
    par = lambda r, off: pl.BlockSpec((r, dqk), lambda b, h, off=off: (0, off + h))
    return pl.pallas_call(
        functools.partial(_mlstm_kernel, n_lat=n_lat, n_ctx=n_ctx),
        grid=(b, nh),
        in_specs=[
            pl.BlockSpec((None, lt, 2 * dqk + 2 * dv), lambda b, h: (b, 0, h)),
            pl.BlockSpec((None, lt, 128), lambda b, h: (b, 0, h)),
            par(3, 0), par(3, nh), par(1, 0), par(1, nh),
            pl.BlockSpec((None, 1, 128), lambda b, h: (h, 0, 0)),
            pl.BlockSpec((None, 1, dv), lambda b, h: (h, 0, 0)),
            pl.BlockSpec(tri.shape, lambda b, h: (0, 0, 0)),
        ],
        out_specs=pl.BlockSpec((None, lt, dv), lambda b, h: (b, 0, h)),
        out_shape=jax.ShapeDtypeStruct((b, lt, nh * dv), BF16),
        scratch_shapes=[
            pltpu.VMEM((nc, dqk, ch), BF16), pltpu.VMEM((lt, dqk), BF16),
            pltpu.VMEM((nc, dv + MLSTM_ONES_ROWS, ch), BF16), pltpu.VMEM((nc, 8, ch), F32),
            pltpu.VMEM((2, nc, 8, ch), F32),
            pltpu.VMEM((nc, dv, ch), F32), pltpu.VMEM((dv + MLSTM_ONES_ROWS, dqk), F32),
        ],
        compiler_params=_cparams("parallel", "parallel"),
        name="mlstm",
    )(u, us, conv_w, conv_w, conv_b, conv_b, gate_b, nw, tri)


def _pad_lanes(a, width=128):
    return jnp.pad(a, [(0, 0)] * (a.ndim - 1) + [(0, width - a.shape[-1])])


def _even_params(w_in, dt_bias, a_log, d_skip):
    heads = a_log.shape[1]
    hg = heads // SSD_GROUPS
    width = heads * SSD_HEAD_DIM
    bc = SSD_GROUPS * SSD_STATE
    o_x, o_b, o_c, o_dt = width, 2 * width, 2 * width + bc, 2 * width + 2 * bc
    o_hgrn = o_dt + 2 * heads
    w_hgrn = w_in[:, o_hgrn:].reshape(w_in.shape[0], 5, HGRN_HEADS, -1).swapaxes(1, 2)
    grp = lambda a, b: w_in[:, a:b].reshape(w_in.shape[0], SSD_GROUPS, -1)
    w_ssd = jnp.concatenate([grp(o_x, o_b), grp(o_b, o_c), grp(o_c, o_dt), grp(0, o_x)], axis=2)
    w_main = jnp.concatenate([w_ssd.reshape(w_in.shape[0], -1), w_hgrn.reshape(w_in.shape[0], -1)], axis=1).astype(BF16)
    dt_w = w_in[:, o_dt:o_hgrn]
    per_group = lambda a: [_pad_lanes(jnp.concatenate([a[..., g * hg:(g + 1) * hg], a[..., heads + g * hg:heads + (g + 1) * hg]], axis=-1))
                           for g in range(SSD_GROUPS)]
    w_small = jnp.concatenate(per_group(dt_w), axis=1).astype(BF16)
    flat = lambda a: a.reshape(1, 2 * heads)
    dtb = jnp.stack(per_group(flat(dt_bias)))
    alog = jnp.stack(per_group(flat(a_log)))
    dsk = jnp.broadcast_to(jnp.repeat(d_skip, SSD_HEAD_DIM).reshape(SSD_GROUPS, hg * SSD_HEAD_DIM, 1),
                           (SSD_GROUPS, hg * SSD_HEAD_DIM, LANES))
    return w_main, w_small, dtb, alog, dsk


def _odd_params(w_in, gate_b):
    nh = MLSTM_HEADS
    edges = np.cumsum([0, nh * MLSTM_DQK, nh * MLSTM_DQK, nh * MLSTM_DV, nh * MLSTM_DV])
    parts = [w_in[:, a:b].reshape(w_in.shape[0], nh, -1) for a, b in zip(edges[:-1], edges[1:])]
    w_main = jnp.concatenate(parts, axis=2).reshape(w_in.shape[0], -1).astype(BF16)
    gw = w_in[:, edges[-1]:]
    w_small = jnp.concatenate([_pad_lanes(gw[:, h::nh]) for h in range(nh)], axis=1).astype(BF16)
    gb = jnp.stack([_pad_lanes(gate_b[:, h].reshape(1, 4)) for h in range(nh)])
    return w_main, w_small, gb


def kernel(x, c, ctx, c_ctx, mod_w, mod_b, norm_w, final_norm_w, mlp_w1, mlp_w2, even_w_in, even_w_out, ssd_conv_w, ssd_conv_b, ssd_a_log, ssd_dt_bias, ssd_d, ssd_norm_w, hgrn_lb, hgrn_norm_w, odd_w_in, odd_w_out, mlstm_conv_w, mlstm_conv_b, mlstm_gate_b, mlstm_norm_w):
    bsz, n_lat, d = x.shape
    n_ctx = ctx.shape[1]
    depth = mod_w.shape[0]
    lt = n_lat + n_ctx
    assert bsz < C_ROWS and n_lat % CONV_PIECE == 0 and n_ctx % SSD_CHUNK == 0 and n_ctx & (n_ctx - 1) == 0
    ctx_row = bsz

    c_all = jnp.zeros((C_ROWS, d), F32).at[:bsz].set(c).at[ctx_row].set(c_ctx)
    mods = _modulation(c_all, mod_w, mod_b).reshape(depth, C_ROWS, N_MOD, 1, d)

    lb_all = jnp.cumsum(jax.nn.softmax(hgrn_lb.astype(F32), axis=0), axis=0)
    lb_all = lb_all - lb_all[0]

    tm_in, tn_even, tn_odd = lt // 2, 1536, 1024
    tm_full = lt // 4
    tm_lat = min(512, n_lat)
    mlp_tf = 1024

    mlp_w1_b, mlp_w2_b = mlp_w1.astype(BF16), mlp_w2.astype(BF16)
    even_w_out_b, odd_w_out_b = even_w_out.astype(BF16), odd_w_out.astype(BF16)

    xx = jnp.concatenate([x, ctx], axis=1)
    for layer in range(depth):
        last = layer == depth - 1
        nw1 = norm_w[layer, 0].reshape(1, d)
        nw2 = norm_w[layer, 1].reshape(1, d)
        if layer % 2 == 0:
            e = layer // 2
            w_main, w_small, dtb, alog, dsk = _even_params(even_w_in[e], ssd_dt_bias[e], ssd_a_log[e], ssd_d[e])
            u, us = _inproj(xx, nw1, mods, layer, ctx_row, w_main, w_small, n_lat, tm_in, tn_even)
            ya = _ssd(u, us, ssd_conv_w[e], ssd_conv_b[e].reshape(1, -1), dtb, alog, dsk,
                      ssd_norm_w[e].reshape(SSD_GROUPS, 1, -1), n_lat, n_ctx)
            yb = _hgrn(u, lb_all[e].reshape(HGRN_HEADS, 1, -1), hgrn_norm_w[e].reshape(HGRN_HEADS, 1, -1), n_lat, n_ctx)
            ys, w_out, w_idx = [ya, yb], even_w_out_b, e
        else:
            o = layer // 2
            w_main, w_small, gb = _odd_params(odd_w_in[o], mlstm_gate_b[o])
            u, us = _inproj(xx, nw1, mods, layer, ctx_row, w_main, w_small, n_lat, tm_in, tn_odd)
            yc = _mlstm(u, us, mlstm_conv_w[o], mlstm_conv_b[o].reshape(1, -1), gb,
                        mlstm_norm_w[o].reshape(MLSTM_HEADS, 1, -1), n_lat, n_ctx)
            ys, w_out, w_idx = [yc], odd_w_out_b, o
        rows, tm = (n_lat, tm_lat) if last else (lt, tm_full)
        xx = _outproj(ys, w_out, w_idx, xx, mods, layer, ctx_row, n_lat, rows, tm // 2 if not last else tm)
        xx = _mlp(xx, nw2, mods, layer, ctx_row, mlp_w1_b, mlp_w2_b,
                  final_norm_w.reshape(1, d), n_lat, rows, tm, mlp_tf, last)
    return xx
```

```python
import functools

import numpy as np
import jax
import jax.numpy as jnp
from jax import lax
from jax.experimental import pallas as pl
from jax.experimental.pallas import tpu as pltpu

F32 = jnp.float32
BF16 = jnp.bfloat16

EPS = 1e-6
GRID_W = 64
N_MOD = 6
HGRN_LOG2_FLOOR = -1e5
C_ROWS = 32

V7X_VMEM_BYTES = 64 * 1024 * 1024
VMEM_LIMIT = V7X_VMEM_BYTES - 8 * 1024 * 1024

SSD_CHUNK = 128
SSD_HEAD_DIM = 64
SSD_STATE = 128
SSD_GROUPS = 2
HGRN_CHUNK = 64
HGRN_HEADS = 8
MLSTM_CHUNK = 128
MLSTM_HEADS = 4
MLSTM_DQK = 256
MLSTM_DV = 512
MLSTM_ONES_ROWS = 16
LANES = 128
SSD_GROUP_WIDTH = 8 * SSD_HEAD_DIM
SSD_GROUP_LANES = 2 * SSD_GROUP_WIDTH + 2 * SSD_STATE
HGRN_HEAD_LANES = 5 * LANES
CONV_PIECE = 256
NORM_ROWS = 16


def _cparams(*sem):
    return pltpu.CompilerParams(dimension_semantics=sem, vmem_limit_bytes=VMEM_LIMIT)


def _silu(x):
    return x * jax.nn.sigmoid(x)


def _softplus(x):
    return jnp.maximum(x, 0.0) + jnp.log(1.0 + jnp.exp(-jnp.abs(x)))


def _log_sigmoid(x):
    return -_softplus(-x)


def _dot(a, b):
    return jnp.dot(a, b, preferred_element_type=F32)


def _dot_nt(a, b):
    return lax.dot_general(a, b, (((1,), (1,)), ((), ())), preferred_element_type=F32)


def _dot_tn(a, b):
    return lax.dot_general(a, b, (((0,), (0,)), ((), ())), preferred_element_type=F32)


def _dot01(m01x3, x):
    hi = x.astype(BF16)
    r = x - hi.astype(F32)
    mid = r.astype(BF16)
    lo = (r - mid.astype(F32)).astype(BF16)
    return _dot(m01x3, jnp.concatenate([hi, mid, lo], axis=0))


def _dot01_rows(x, m01x3):
    hi = x.astype(BF16)
    r = x - hi.astype(F32)
    mid = r.astype(BF16)
    lo = (r - mid.astype(F32)).astype(BF16)
    return _dot(jnp.concatenate([hi, mid, lo], axis=1), m01x3)


def _rms(x):
    return x * lax.rsqrt(jnp.mean(x * x, axis=-1, keepdims=True) + EPS)


def _conv3(u, w, b, period):
    rows = u.shape[0]
    t = lax.broadcasted_iota(jnp.int32, u.shape, 0) & (period - 1)
    left = jnp.where(t == 0, 0.0, pltpu.roll(u, 1, axis=0))
    right = jnp.where(t == period - 1, 0.0, pltpu.roll(u, rows - 1, axis=0))
    return left * w[0:1] + u * w[1:2] + right * w[2:3] + b


def _modulated_norm(x_ref, h_ref, nw, sh_l, sc_l, sh_c, sc_c, row0, n_lat):
    a_l, a_c = nw * (1.0 + sc_l), nw * (1.0 + sc_c)

    def block(r, carry):
        start = pl.multiple_of(r * NORM_ROWS, NORM_ROWS)
        rs = pl.ds(start, NORM_ROWS)
        is_ctx = row0 + start >= n_lat
        h_ref[rs, :] = (_rms(x_ref[rs, :]) * jnp.where(is_ctx, a_c, a_l) + jnp.where(is_ctx, sh_c, sh_l)).astype(BF16)
        return carry

    lax.fori_loop(0, x_ref.shape[0] // NORM_ROWS, block, 0, unroll=4)


def _mod_kernel(c_ref, w_ref, b_ref, o_ref):
    a = _silu(c_ref[...]).astype(BF16)
    o_ref[...] = _dot(a, w_ref[...].astype(BF16)) + b_ref[...]


def _modulation(c_all, mod_w, mod_b):
    depth, d, n = mod_w.shape
    tn = 1024
    return pl.pallas_call(
        _mod_kernel,
        grid=(depth, n // tn),
        in_specs=[
            pl.BlockSpec((C_ROWS, d), lambda l, j: (0, 0)),
            pl.BlockSpec((None, d, tn), lambda l, j: (l, 0, j)),
            pl.BlockSpec((None, 1, tn), lambda l, j: (l, 0, j)),
        ],
        out_specs=pl.BlockSpec((None, C_ROWS, tn), lambda l, j: (l, 0, j)),
        out_shape=jax.ShapeDtypeStruct((depth, C_ROWS, n), F32),
        compiler_params=_cparams("parallel", "parallel"),
        name="modulation",
    )(c_all, mod_w, mod_b.reshape(depth, 1, n))


def _mod_specs(layer, ctx_row, ks, d, nargs):
    specs = []
    for k in ks:
        if nargs == 3:
            specs.append(pl.BlockSpec((None, None, None, 1, d), lambda b, i, j, k=k: (layer, b, k, 0, 0)))
            specs.append(pl.BlockSpec((None, None, None, 1, d), lambda b, i, j, k=k: (layer, ctx_row, k, 0, 0)))
        else:
            specs.append(pl.BlockSpec((None, None, None, 1, d), lambda b, i, k=k: (layer, b, k, 0, 0)))
            specs.append(pl.BlockSpec((None, None, None, 1, d), lambda b, i, k=k: (layer, ctx_row, k, 0, 0)))
    return specs


def _inproj_kernel(x_ref, nw_ref, shl_ref, shc_ref, scl_ref, scc_ref, w_ref, ws_ref, o_ref, os_ref, h_ref, *, n_lat):
    @pl.when(pl.program_id(2) == 0)
    def _():
        _modulated_norm(x_ref, h_ref, nw_ref[...], shl_ref[...], scl_ref[...], shc_ref[...], scc_ref[...],
                        pl.program_id(1) * x_ref.shape[0], n_lat)
        os_ref[...] = _dot(h_ref[...], ws_ref[...])

    o_ref[...] = _dot(h_ref[...], w_ref[...]).astype(o_ref.dtype)


def _inproj(x, nw, mods, layer, ctx_row, w_main, w_small, n_lat, tm, tn):
    b, lt, d = x.shape
    n = w_main.shape[1]
    ns = w_small.shape[1]
    return pl.pallas_call(
        functools.partial(_inproj_kernel, n_lat=n_lat),
        grid=(b, lt // tm, n // tn),
        in_specs=[
            pl.BlockSpec((None, tm, d), lambda b, i, j: (b, i, 0)),
            pl.BlockSpec((1, d), lambda b, i, j: (0, 0)),
            *_mod_specs(layer, ctx_row, (0, 1), d, 3),
            pl.BlockSpec((d, tn), lambda b, i, j: (0, j)),
            pl.BlockSpec((d, ns), lambda b, i, j: (0, 0)),
        ],
        out_specs=[
            pl.BlockSpec((None, tm, tn), lambda b, i, j: (b, i, j)),
            pl.BlockSpec((None, tm, ns), lambda b, i, j: (b, i, 0)),
        ],
        out_shape=[jax.ShapeDtypeStruct((b, lt, n), BF16), jax.ShapeDtypeStruct((b, lt, ns), F32)],
        scratch_shapes=[pltpu.VMEM((tm, d), BF16)],
        compiler_params=_cparams("parallel", "parallel", "arbitrary"),
        name="inproj",
    )(x, nw, mods, mods, mods, mods, w_main, w_small)


def _outproj_kernel(*refs, n_y, n_lat):
    y_refs, w_refs = refs[:n_y], refs[n_y:2 * n_y]
    x_ref, gl_ref, gc_ref, o_ref = refs[2 * n_y:]
    acc = _dot(y_refs[0][...], w_refs[0][...])
    for y_ref, w_ref in zip(y_refs[1:], w_refs[1:]):
        acc = acc + _dot(y_ref[...], w_ref[...])
    tm = x_ref.shape[0]
    rows = pl.program_id(1) * tm + lax.broadcasted_iota(jnp.int32, (tm, 1), 0)
    g = jnp.where(rows >= n_lat, gc_ref[...], gl_ref[...])
    o_ref[...] = x_ref[...] + g * acc


def _outproj(ys, w, w_idx, x, mods, layer, ctx_row, n_lat, rows, tm):
    b, lt, d = x.shape
    n_y = len(ys)
    y_specs = [pl.BlockSpec((None, tm, y.shape[2]), lambda b, i: (b, i, 0)) for y in ys]
    w_specs = [pl.BlockSpec((None, y.shape[2], d), lambda b, i, k=k: (w_idx, k, 0)) for k, y in enumerate(ys)]
    return pl.pallas_call(
        functools.partial(_outproj_kernel, n_y=n_y, n_lat=n_lat),
        grid=(b, rows // tm),
        in_specs=[*y_specs, *w_specs,
                  pl.BlockSpec((None, tm, d), lambda b, i: (b, i, 0)),
                  *_mod_specs(layer, ctx_row, (2,), d, 2)],
        out_specs=pl.BlockSpec((None, tm, d), lambda b, i: (b, i, 0)),
        out_shape=jax.ShapeDtypeStruct((b, rows, d), F32),
        compiler_params=_cparams("parallel", "parallel"),
        name="outproj",
    )(*ys, *([w] * n_y), x, mods, mods)


def _mlp_kernel(x_ref, nw_ref, shl_ref, shc_ref, scl_ref, scc_ref, gl_ref, gc_ref, w1_ref, w2_ref, fw_ref,
                o_ref, h_ref, acc_ref, *, n_lat, final_norm):
    f = pl.program_id(2)
    tm = x_ref.shape[0]

    @pl.when(f == 0)
    def _():
        _modulated_norm(x_ref, h_ref, nw_ref[...], shl_ref[...], scl_ref[...], shc_ref[...], scc_ref[...],
                        pl.program_id(1) * tm, n_lat)
        acc_ref[...] = jnp.zeros_like(acc_ref)

    a = jnp.maximum(_dot(h_ref[...], w1_ref[...]), 0.0)
    acc_ref[...] += _dot((a * a).astype(BF16), w2_ref[...])

    @pl.when(f == pl.num_programs(2) - 1)
    def _():
        rows = pl.program_id(1) * tm + lax.broadcasted_iota(jnp.int32, (tm, 1), 0)
        g = jnp.where(rows >= n_lat, gc_ref[...], gl_ref[...])
        y = x_ref[...] + g * acc_ref[...]
        if final_norm:
            y = _rms(y) * fw_ref[...]
        o_ref[...] = y


def _mlp(x, nw, mods, layer, ctx_row, w1, w2, fw, n_lat, rows, tm, tf, final_norm):
    b, _, d = x.shape
    dff = w1.shape[2]
    return pl.pallas_call(
        functools.partial(_mlp_kernel, n_lat=n_lat, final_norm=final_norm),
        grid=(b, rows // tm, dff // tf),
        in_specs=[
            pl.BlockSpec((None, tm, d), lambda b, i, j: (b, i, 0)),
            pl.BlockSpec((1, d), lambda b, i, j: (0, 0)),
            *_mod_specs(layer, ctx_row, (3, 4, 5), d, 3),
            pl.BlockSpec((None, d, tf), lambda b, i, j: (layer, 0, j)),
            pl.BlockSpec((None, tf, d), lambda b, i, j: (layer, j, 0)),
            pl.BlockSpec((1, d), lambda b, i, j: (0, 0)),
        ],
        out_specs=pl.BlockSpec((None, tm, d), lambda b, i, j: (b, i, 0)),
        out_shape=jax.ShapeDtypeStruct((b, rows, d), F32),
        scratch_shapes=[pltpu.VMEM((tm, d), BF16), pltpu.VMEM((tm, d), F32)],
        compiler_params=_cparams("parallel", "parallel", "arbitrary"),
        name="mlp",
    )(x, nw, mods, mods, mods, mods, mods, mods, w1, w2, fw)


def _tri_consts(c):
    lower = np.tril(np.ones((c, c), np.float32))
    tri = np.stack([lower, lower.T])
    return jnp.asarray(np.concatenate([tri, tri, tri], axis=2), BF16)


def _chunk_index(j, d, n_lat_chunks, n_chunks):
    if d == 0:
        c = j + n_lat_chunks
        return jnp.where(c >= n_chunks, c - n_chunks, c)
    return n_chunks - 1 - j


def _conv_rows(n_lat, n_ctx, fn):
    def body(p, carry):
        fn(pl.multiple_of(p * CONV_PIECE, CONV_PIECE), CONV_PIECE, GRID_W)
        return carry
    lax.fori_loop(0, n_lat // CONV_PIECE, body, 0)
    fn(n_lat, n_ctx, n_ctx)


def _ssd_kernel(u_ref, dt_ref, cwx_ref, cwb_ref, cwc_ref, cbx_ref, cbb_ref, cbc_ref,
                dtb_ref, alog_ref, dsk_ref, nw_ref, tri_ref, o_ref,
                xst, bs, cst, cbs, dtt, cums, cumt, ecum, eend, etot, yst, hst, *, n_lat, n_ctx):
    ch = SSD_CHUNK
    p = SSD_HEAD_DIM
    gw = cwx_ref.shape[1]
    ns = cwb_ref.shape[1]
    heads = gw // p
    x_ref, b_ref = u_ref.at[:, 0:gw], u_ref.at[:, gw:gw + ns]
    c_ref, z_ref = u_ref.at[:, gw + ns:gw + 2 * ns], u_ref.at[:, gw + 2 * ns:]
    n_chunks = (n_lat + n_ctx) // ch
    n_lat_chunks = n_lat // ch

    def conv_piece(r0, rows, period):
        sl = pl.ds(r0, rows)
        c0 = r0 // ch
        x = _silu(_conv3(x_ref[sl, :].astype(F32), cwx_ref[...], cbx_ref[...], period))
        c = _silu(_conv3(c_ref[sl, :].astype(F32), cwc_ref[...], cbc_ref[...], period))
        bmat = _silu(_conv3(b_ref[sl, :].astype(F32), cwb_ref[...], cbb_ref[...], period)).astype(BF16)
        bs[sl, :] = bmat
        dt = _softplus(dt_ref[sl, :] + dtb_ref[...])
        a = -jnp.exp(alog_ref[...]) * dt
        nd = 2 * heads
        for k in range(rows // ch):
            rk = slice(k * ch, (k + 1) * ch)
            for j in range(gw // ch):
                xst[c0 + k, j * ch:(j + 1) * ch, :] = x[rk, j * ch:(j + 1) * ch].T
            cc_t = c[rk].T.astype(BF16)
            cst[c0 + k] = cc_t
            cbs[c0 + k] = _dot(bmat[rk], cc_t)
            dt_t = dt[rk].T
            dtt[c0 + k] = dt_t[0:nd]
            for d in range(2):
                cum = _dot01(tri_ref[d], a[rk])
                cum_t = cum.T
                tot_c = cum_t[:, ch - 1:ch] if d == 0 else cum_t[:, 0:1]
                cums[d, c0 + k] = cum
                cumt[d, c0 + k] = cum_t[0:nd]
                ecum[d, c0 + k] = jnp.exp(cum_t)[0:nd]
                eend[d, c0 + k] = (jnp.exp(tot_c - cum_t) * dt_t)[0:nd]
                etot[d, c0 + k] = jnp.broadcast_to(jnp.exp(cum[ch - 1:ch, :] if d == 0 else cum[0:1, :]), (8, ch))

    _conv_rows(n_lat, n_ctx, conv_piece)

    si = lax.broadcasted_iota(jnp.int32, (ch, ch), 0)
    ti = lax.broadcasted_iota(jnp.int32, (ch, ch), 1)

    for d in range(2):
        mask = (si <= ti) if d == 0 else (si >= ti)
        hst[...] = jnp.zeros_like(hst)

        def body(j, carry, d=d, mask=mask):
            c = _chunk_index(j, d, n_lat_chunks, n_chunks)
            sl = pl.ds(pl.multiple_of(c * ch, ch), ch)
            cum, cum_t = cums[d, c], cumt[d, c]
            e_cum_t, e_end_t, e_tot = ecum[d, c], eend[d, c], etot[d, c][0:1]
            dt_t = dtt[c]
            bc = bs[sl, :]
            cc_t = cst[c]
            cb_t = cbs[c]
            for h in range(heads):
                col = d * heads + h
                hs = slice(h * p, (h + 1) * p)
                diff = cum_t[col:col + 1, :] - cum[:, col:col + 1]
                decay = jnp.exp(jnp.where(mask, diff, -jnp.inf))
                xh = xst[c, hs, :]
                state = hst[h]
                y = _dot((xh * dt_t[col:col + 1, :]).astype(BF16), (cb_t * decay).astype(BF16))
                y = y + _dot(state.astype(BF16), cc_t) * e_cum_t[col:col + 1, :]
                xw = (xh * e_end_t[col:col + 1, :]).astype(BF16)
                hst[h] = state * e_tot[:, col:col + 1] + _dot(xw, bc)
                if d == 0:
                    yst[c, hs, :] = y
                else:
                    yst[c, hs, :] = yst[c, hs, :] + y
            if d == 1:
                g_t = yst[c] + xst[c] * dsk_ref[...]
                g = jnp.concatenate([g_t[j * ch:(j + 1) * ch].T for j in range(gw // ch)], axis=1)
                g = g * _silu(z_ref[sl, :].astype(F32))
                o_ref[sl, :] = (_rms(g) * nw_ref[...]).astype(BF16)
            return carry

        lax.fori_loop(0, n_chunks, body, 0, unroll=6)


def _ssd(u, us, conv_w, conv_b, dtb, alog, dsk, nw, n_lat, n_ctx):
    b, lt, _ = u.shape
    gw = SSD_GROUP_WIDTH
    ns = SSD_STATE
    nc = lt // SSD_CHUNK
    nd = 2 * gw // SSD_HEAD_DIM
    cx_blk, cb_blk, cc_blk = 0, SSD_GROUPS * gw // ns, SSD_GROUPS * (gw + ns) // ns
    par = lambda r, w, off: pl.BlockSpec((r, w), lambda b, g, off=off: (0, off + g))
    grp = lambda w: pl.BlockSpec((None, 1, w), lambda b, g: (g, 0, 0))
    return pl.pallas_call(
        functools.partial(_ssd_kernel, n_lat=n_lat, n_ctx=n_ctx),
        grid=(b, SSD_GROUPS),
        in_specs=[
            pl.BlockSpec((None, lt, SSD_GROUP_LANES), lambda b, g: (b, 0, g)),
            pl.BlockSpec((None, lt, LANES), lambda b, g: (b, 0, g)),
            par(3, gw, cx_blk), par(3, ns, cb_blk), par(3, ns, cc_blk),
            par(1, gw, cx_blk), par(1, ns, cb_blk), par(1, ns, cc_blk),
            grp(128), grp(128), pl.BlockSpec((None, gw, 128), lambda b, g: (g, 0, 0)), grp(gw),
            pl.BlockSpec((2, SSD_CHUNK, 3 * SSD_CHUNK), lambda b, g: (0, 0, 0)),
        ],
        out_specs=pl.BlockSpec((None, lt, gw), lambda b, g: (b, 0, g)),
        out_shape=jax.ShapeDtypeStruct((b, lt, SSD_GROUPS * gw), BF16),
        scratch_shapes=[
            pltpu.VMEM((nc, gw, SSD_CHUNK), F32), pltpu.VMEM((lt, ns), BF16), pltpu.VMEM((nc, ns, SSD_CHUNK), BF16),
            pltpu.VMEM((nc, SSD_CHUNK, SSD_CHUNK), F32), pltpu.VMEM((nc, nd, SSD_CHUNK), F32), pltpu.VMEM((2, nc, SSD_CHUNK, LANES), F32),
            pltpu.VMEM((2, nc, nd, SSD_CHUNK), F32), pltpu.VMEM((2, nc, nd, SSD_CHUNK), F32),
            pltpu.VMEM((2, nc, nd, SSD_CHUNK), F32), pltpu.VMEM((2, nc, 8, LANES), F32),
            pltpu.VMEM((nc, gw, SSD_CHUNK), F32), pltpu.VMEM((gw // SSD_HEAD_DIM, SSD_HEAD_DIM, ns), F32),
        ],
        compiler_params=_cparams("parallel", "parallel"),
        name="ssd",
    )(u, us, conv_w, conv_w, conv_w, conv_b, conv_b, conv_b, dtb, alog, dsk, nw, _tri_consts(SSD_CHUNK))


_HGRN_LEVELS = (32, 16, 8, 4, 2, 1)


def _hgrn_consts():
    c = HGRN_CHUNK
    sums = np.zeros((7, c, c), np.float32)
    pairs = np.zeros((7, c, c), np.float32)
    for li, m in enumerate(_HGRN_LEVELS):
        for t in range(c):
            beta = (t // (2 * m)) * 2 * m
            mid = beta + m
            if t >= mid:
                sums[li, t, mid:t + 1] = 1.0
                pairs[li, t, beta:mid] = 1.0
            else:
                sums[li, t, t + 1:mid] = 1.0
    sums[6] = np.tril(np.ones((c, c), np.float32))
    pairs[6] = np.eye(c, dtype=np.float32)
    sums = np.stack([sums, sums[:, ::-1, ::-1]]).reshape(2, 7 * c, c)
    sums = np.concatenate([sums, sums, sums, np.zeros_like(sums)], axis=2)
    pairs = np.stack([pairs, pairs[:, ::-1, ::-1]])
    zero = np.zeros_like(pairs)
    pairs = np.concatenate([np.concatenate([pairs, zero], axis=3), np.concatenate([zero, pairs], axis=3)], axis=2)
    return jnp.asarray(sums, BF16), jnp.asarray(pairs, F32)


def _hgrn_kernel(u_ref, lb_ref, nw_ref, sums_ref, pairs_ref, o_ref,
                 ys, qb_s, *dir_scratch, n_lat, n_ctx):
    dk = dv = u_ref.shape[1] // 5
    q_ref, ff_ref, fb_ref, i_ref, g_ref = (u_ref.at[:, k * dk:(k + 1) * dk] for k in range(5))
    w_s, kb_s, qd_s, kd_s, et_s, att_s, p_s = zip(dir_scratch[:7], dir_scratch[7:])
    ch = HGRN_CHUNK
    n_chunks = (n_lat + n_ctx) // ch
    n_lat_chunks = n_lat // ch
    n_pairs = n_chunks // 2
    n_lv = len(_HGRN_LEVELS)
    lb = lb_ref[...]
    one_m_lb = 1.0 - lb
    t_idx = lax.broadcasted_iota(jnp.int32, (ch, dk), 0)

    laters = [[((t_idx & m) != 0) == (d == 0) for m in _HGRN_LEVELS] for d in range(2)]

    def operands(d, pi):
        f_ref = ff_ref if d == 0 else fb_ref
        r0 = pl.multiple_of(pi * 2 * ch, 2 * ch)
        sl = pl.ds(r0, 2 * ch)
        gate = jax.nn.sigmoid(f_ref[sl, :].astype(F32))
        kin = one_m_lb * (1.0 - gate)
        logf2 = jnp.maximum(jnp.log2(lb + one_m_lb * gate), HGRN_LOG2_FLOOR)
        q = _silu(q_ref[sl, :].astype(F32))
        if d == 0:
            qb_s[sl, :] = q.astype(BF16)
        kb_s[d][sl, :] = kin.astype(BF16)
        lf = jnp.concatenate([logf2[:ch], logf2[ch:]], axis=1)
        hi = lf.astype(BF16)
        r1 = lf - hi.astype(F32)
        mid = r1.astype(BF16)
        lo = (r1 - mid.astype(F32)).astype(BF16)
        rel2 = _dot(sums_ref[d], jnp.concatenate([hi, mid, lo, jnp.zeros_like(hi)], axis=0))
        for half in range(2):
            rows = slice(half * ch, (half + 1) * ch)
            rel = rel2[:, half * dk:(half + 1) * dk]
            qh, kh = q[rows], kin[rows]
            c = pi * 2 + half
            for li in range(n_lv):
                e = jnp.exp2(rel[li * ch:(li + 1) * ch])
                w_s[d][pi, li, rows, :] = (jnp.where(laters[d][li], qh, kh) * e).astype(BF16)
            bcum = rel[n_lv * ch:(n_lv + 1) * ch]
            tot = bcum[ch - 1:ch] if d == 0 else bcum[0:1]
            hs = pl.ds(r0 + half * ch, ch)
            qd_s[d][hs, :] = (qh * jnp.exp2(bcum)).astype(BF16)
            kd_s[d][hs, :] = (kh * jnp.exp2(tot - bcum)).astype(BF16)
            et_s[d][c] = jnp.broadcast_to(jnp.exp2(tot), (8, dk))

    def intra(d, pi):
        sl = pl.ds(pl.multiple_of(pi * 2 * ch, 2 * ch), 2 * ch)
        att = pairs_ref[d, n_lv] * _dot_nt(qb_s[sl, :], kb_s[d][sl, :])
        for li in range(n_lv):
            w = w_s[d][pi, li]
            att = att + pairs_ref[d, li] * _dot_nt(w, w)
        att_s[d][sl, :] = att.astype(BF16)
        for half in range(2):
            hs = pl.ds(pl.multiple_of(pi * 2 * ch, 2 * ch) + half * ch, ch)
            p_s[d][pi * 2 + half] = _dot_tn(i_ref[hs, :], kd_s[d][hs, :])

    def scan(d, j, state_t):
        pi = _chunk_index(j, d, n_lat_chunks // 2, n_pairs)
        r0 = pl.multiple_of(pi * 2 * ch, 2 * ch)
        sl = pl.ds(r0, 2 * ch)
        o_intra = _dot(att_s[d][sl, :], i_ref[sl, :])
        o_halves = [None, None]
        for half in ((0, 1) if d == 0 else (1, 0)):
            hs = pl.ds(r0 + half * ch, ch)
            o_halves[half] = (o_intra[half * ch:(half + 1) * ch]
                              + _dot_nt(qd_s[d][hs, :], state_t.astype(BF16)))
            c = pi * 2 + half
            state_t = state_t * et_s[d][c][0:1] + p_s[d][c]
        o = jnp.concatenate(o_halves, axis=0)
        if d == 0:
            ys[sl, :] = o
        else:
            o = ys[sl, :] + o
            o_ref[sl, :] = (_rms(o) * nw_ref[...] * _silu(g_ref[sl, :].astype(F32))).astype(BF16)
        return state_t

    zero_state = jnp.zeros((dv, dk), F32)

    def stage_a(pi, carry):
        operands(0, pi)
        return carry

    def stage_b(pi, carry):
        intra(0, pi)
        operands(1, pi)
        return carry

    def stage_c(j, state_t):
        intra(1, j)
        return scan(0, j, state_t)

    lax.fori_loop(0, n_pairs, stage_a, 0, unroll=9)
    lax.fori_loop(0, n_pairs, stage_b, 0, unroll=9)
    lax.fori_loop(0, n_pairs, stage_c, zero_state, unroll=18)
    lax.fori_loop(0, n_pairs, functools.partial(scan, 1), zero_state, unroll=18)


def _hgrn(u, lb, nw, n_lat, n_ctx):
    b, lt, _ = u.shape
    w = LANES
    ch = HGRN_CHUNK
    n_chunks = lt // ch
    sums, pairs = _hgrn_consts()
    head = pl.BlockSpec((None, 1, w), lambda b, h: (h, 0, 0))
    return pl.pallas_call(
        functools.partial(_hgrn_kernel, n_lat=n_lat, n_ctx=n_ctx),
        grid=(b, HGRN_HEADS),
        in_specs=[pl.BlockSpec((None, lt, HGRN_HEAD_LANES),
                               lambda b, h: (b, 0, SSD_GROUPS * SSD_GROUP_LANES // HGRN_HEAD_LANES + h)), head, head,
                  pl.BlockSpec(sums.shape, lambda b, h: (0, 0, 0)),
                  pl.BlockSpec(pairs.shape, lambda b, h: (0, 0, 0, 0))],
        out_specs=pl.BlockSpec((None, lt, w), lambda b, h: (b, 0, h)),
        out_shape=jax.ShapeDtypeStruct((b, lt, HGRN_HEADS * w), BF16),
        scratch_shapes=[
            pltpu.VMEM((lt, w), F32), pltpu.VMEM((lt, w), BF16),
            *([pltpu.VMEM((n_chunks // 2, len(_HGRN_LEVELS), 2 * ch, w), BF16),
               pltpu.VMEM((lt, w), BF16), pltpu.VMEM((lt, w), BF16), pltpu.VMEM((lt, w), BF16),
               pltpu.VMEM((n_chunks, 8, w), F32), pltpu.VMEM((lt, 2 * ch), BF16),
               pltpu.VMEM((n_chunks, w, w), F32)] * 2),
        ],
        compiler_params=_cparams("parallel", "parallel"),
        name="hgrn2",
    )(u, lb, nw, sums, pairs)


def _mlstm_kernel(u_ref, gt_ref, cwq_ref, cwk_ref, cbq_ref, cbk_ref, gb_ref, nw_ref, tri_ref,
                  o_ref, qst, ks, kqs, vst, gst, bst, yst, cst, *, n_lat, n_ctx):
    ch = MLSTM_CHUNK
    n_chunks = (n_lat + n_ctx) // ch
    n_lat_chunks = n_lat // ch
    dqk = cwq_ref.shape[1]
    dv = nw_ref.shape[1]
    q_ref, k_ref = u_ref.at[:, 0:dqk], u_ref.at[:, dqk:2 * dqk]
    v_ref, og_ref = u_ref.at[:, 2 * dqk:2 * dqk + dv], u_ref.at[:, 2 * dqk + dv:]
    k_scale = dqk ** -0.5

    def conv_piece(r0, rows, period):
        sl = pl.ds(r0, rows)
        c0 = r0 // ch
        q = _silu(_conv3(q_ref[sl, :].astype(F32), cwq_ref[...], cbq_ref[...], period))
        kmat = (_silu(_conv3(k_ref[sl, :].astype(F32), cwk_ref[...], cbk_ref[...], period)) * k_scale).astype(BF16)
        ks[sl, :] = kmat
        raw = gt_ref[sl, :] + gb_ref[...]
        lane = lax.broadcasted_iota(jnp.int32, raw.shape, 1)
        g = jnp.where(lane < 2, raw, _log_sigmoid(raw))
        v = v_ref[sl, :].astype(F32)
        for k in range(rows // ch):
            rk = slice(k * ch, (k + 1) * ch)
            for j in range(dqk // ch):
                qst[c0 + k, j * ch:(j + 1) * ch, :] = q[rk, j * ch:(j + 1) * ch].T.astype(BF16)
            kqs[c0 + k] = _dot(kmat[rk], qst[c0 + k])
            for j in range(dv // ch):
                vst[c0 + k, j * ch:(j + 1) * ch, :] = v[rk, j * ch:(j + 1) * ch].T.astype(BF16)
            vst[c0 + k, dv:, :] = jnp.ones((MLSTM_ONES_ROWS, ch), BF16)
            g_t = g[rk].T
            gst[c0 + k] = g_t[0:8]
            for d in range(2):
                bst[d, c0 + k] = _dot01_rows(g_t, tri_ref[d])[0:8]

    _conv_rows(n_lat, n_ctx, conv_piece)

    si = lax.broadcasted_iota(jnp.int32, (ch, ch), 0)
    ti = lax.broadcasted_iota(jnp.int32, (ch, ch), 1)

    for d in range(2):
        mask = (si <= ti) if d == 0 else (si >= ti)
        cst[...] = jnp.zeros_like(cst)

        def body(j, m_prev, d=d, mask=mask):
            c = _chunk_index(j, d, n_lat_chunks, n_chunks)
            sl = pl.ds(pl.multiple_of(c * ch, ch), ch)
            brow = bst[d, c][2 + d:3 + d, :]
            irow = gst[c][d:d + 1, :]
            tot = brow[:, ch - 1:ch] if d == 0 else brow[:, 0:1]
            logd = jnp.where(mask, brow + jnp.broadcast_to(irow - brow, (ch, ch)).T, -jnp.inf)
            gstate = brow + m_prev
            mt = jnp.maximum(jnp.max(logd, axis=0, keepdims=True), gstate)
            q_t = qst[c]
            kc = ks[sl, :]
            w = kqs[c] * jnp.exp(logd - mt)
            sw = jnp.exp(gstate - mt)
            v_t = vst[c]
            state = cst[...]
            qstate = _dot(state.astype(BF16), q_t)
            num = _dot(v_t[:dv], w.astype(BF16)) + sw * qstate[:dv]
            den = jnp.sum(w, axis=0, keepdims=True) + sw * qstate[dv:dv + 1]
            hout = num * (1.0 / jnp.maximum(jnp.abs(den), jnp.exp(-mt)))
            logw = tot - brow + irow
            m_new = jnp.maximum(tot + m_prev, jnp.max(logw, axis=1, keepdims=True))
            ws = jnp.exp(logw - m_new).astype(BF16)
            cst[...] = jnp.exp(tot + m_prev - m_new) * state + _dot(v_t * ws, kc)
            if d == 0:
                yst[c] = hout
            else:
                hh_t = yst[c] + hout
                hh = jnp.concatenate([hh_t[i * ch:(i + 1) * ch].T for i in range(dv // ch)], axis=1)
                o_ref[sl, :] = (_rms(hh) * nw_ref[...] * jax.nn.sigmoid(og_ref[sl, :].astype(F32))).astype(BF16)
            return m_new

        lax.fori_loop(0, n_chunks, body, jnp.zeros((1, 1), F32), unroll=6)


def _mlstm(u, us, conv_w, conv_b, gate_b, nw, n_lat, n_ctx):
    b, lt, _ = u.shape
    dqk, dv = MLSTM_DQK, MLSTM_DV
    nh = MLSTM_HEADS
    ch = MLSTM_CHUNK
    nc = lt // ch
    upper = np.triu(np.ones((ch, ch), np.float32))
    tri = np.stack([upper, upper.T])
    tri = jnp.asarray(np.concatenate([tri, tri, tri], axis=1), BF16)
    par = lambda r, off: pl.BlockSpec((r, dqk), lambda b, h, off=off: (0, off + h))
    return pl.pallas_call(
        functools.partial(_mlstm_kernel, n_lat=n_lat, n_ctx=n_ctx),
        grid=(b, nh),
        in_specs=[
            pl.BlockSpec((None, lt, 2 * dqk + 2 * dv), lambda b, h: (b, 0, h)),
            pl.BlockSpec((None, lt, 128), lambda b, h: (b, 0, h)),
            par(3, 0), par(3, nh), par(1, 0), par(1, nh),
            pl.BlockSpec((None, 1, 128), lambda b, h: (h, 0, 0)),
            pl.BlockSpec((None, 1, dv), lambda b, h: (h, 0, 0)),
            pl.BlockSpec(tri.shape, lambda b, h: (0, 0, 0)),
        ],
        out_specs=pl.BlockSpec((None, lt, dv), lambda b, h: (b, 0, h)),
        out_shape=jax.ShapeDtypeStruct((b, lt, nh * dv), BF16),
        scratch_shapes=[
            pltpu.VMEM((nc, dqk, ch), BF16), pltpu.VMEM((lt, dqk), BF16), pltpu.VMEM((nc, ch, ch), F32),
            pltpu.VMEM((nc, dv + MLSTM_ONES_ROWS, ch), BF16), pltpu.VMEM((nc, 8, ch), F32),
            pltpu.VMEM((2, nc, 8, ch), F32),
            pltpu.VMEM((nc, dv, ch), F32), pltpu.VMEM((dv + MLSTM_ONES_ROWS, dqk), F32),
        ],
        compiler_params=_cparams("parallel", "parallel"),
        name="mlstm",
    )(u, us, conv_w, conv_w, conv_b, conv_b, gate_b, nw, tri)


def _pad_lanes(a, width=128):
    return jnp.pad(a, [(0, 0)] * (a.ndim - 1) + [(0, width - a.shape[-1])])


def _even_params(w_in, dt_bias, a_log, d_skip):
    heads = a_log.shape[1]
    hg = heads // SSD_GROUPS
    width = heads * SSD_HEAD_DIM
    bc = SSD_GROUPS * SSD_STATE
    o_x, o_b, o_c, o_dt = width, 2 * width, 2 * width + bc, 2 * width + 2 * bc
    o_hgrn = o_dt + 2 * heads
    w_hgrn = w_in[:, o_hgrn:].reshape(w_in.shape[0], 5, HGRN_HEADS, -1).swapaxes(1, 2)
    grp = lambda a, b: w_in[:, a:b].reshape(w_in.shape[0], SSD_GROUPS, -1)
    w_ssd = jnp.concatenate([grp(o_x, o_b), grp(o_b, o_c), grp(o_c, o_dt), grp(0, o_x)], axis=2)
    w_main = jnp.concatenate([w_ssd.reshape(w_in.shape[0], -1), w_hgrn.reshape(w_in.shape[0], -1)], axis=1).astype(BF16)
    dt_w = w_in[:, o_dt:o_hgrn]
    per_group = lambda a: [_pad_lanes(jnp.concatenate([a[..., g * hg:(g + 1) * hg], a[..., heads + g * hg:heads + (g + 1) * hg]], axis=-1))
                           for g in range(SSD_GROUPS)]
    w_small = jnp.concatenate(per_group(dt_w), axis=1).astype(BF16)
    flat = lambda a: a.reshape(1, 2 * heads)
    dtb = jnp.stack(per_group(flat(dt_bias)))
    alog = jnp.stack(per_group(flat(a_log)))
    dsk = jnp.broadcast_to(jnp.repeat(d_skip, SSD_HEAD_DIM).reshape(SSD_GROUPS, hg * SSD_HEAD_DIM, 1),
                           (SSD_GROUPS, hg * SSD_HEAD_DIM, LANES))
    return w_main, w_small, dtb, alog, dsk


def _odd_params(w_in, gate_b):
    nh = MLSTM_HEADS
    edges = np.cumsum([0, nh * MLSTM_DQK, nh * MLSTM_DQK, nh * MLSTM_DV, nh * MLSTM_DV])
    parts = [w_in[:, a:b].reshape(w_in.shape[0], nh, -1) for a, b in zip(edges[:-1], edges[1:])]
    w_main = jnp.concatenate(parts, axis=2).reshape(w_in.shape[0], -1).astype(BF16)
    gw = w_in[:, edges[-1]:]
    w_small = jnp.concatenate([_pad_lanes(gw[:, h::nh]) for h in range(nh)], axis=1).astype(BF16)
    gb = jnp.stack([_pad_lanes(gate_b[:, h].reshape(1, 4)) for h in range(nh)])
    return w_main, w_small, gb


def kernel(x, c, ctx, c_ctx, mod_w, mod_b, norm_w, final_norm_w, mlp_w1, mlp_w2, even_w_in, even_w_out, ssd_conv_w, ssd_conv_b, ssd_a_log, ssd_dt_bias, ssd_d, ssd_norm_w, hgrn_lb, hgrn_norm_w, odd_w_in, odd_w_out, mlstm_conv_w, mlstm_conv_b, mlstm_gate_b, mlstm_norm_w):
    bsz, n_lat, d = x.shape
    n_ctx = ctx.shape[1]
    depth = mod_w.shape[0]
    lt = n_lat + n_ctx
    assert bsz < C_ROWS and n_lat % CONV_PIECE == 0 and n_ctx % SSD_CHUNK == 0 and n_ctx & (n_ctx - 1) == 0
    ctx_row = bsz

    c_all = jnp.zeros((C_ROWS, d), F32).at[:bsz].set(c).at[ctx_row].set(c_ctx)
    mods = _modulation(c_all, mod_w, mod_b).reshape(depth, C_ROWS, N_MOD, 1, d)

    lb_all = jnp.cumsum(jax.nn.softmax(hgrn_lb.astype(F32), axis=0), axis=0)
    lb_all = lb_all - lb_all[0]

    tm_in, tn_even, tn_odd = lt // 2, 1536, 1024
    tm_full = lt // 4
    tm_lat = min(512, n_lat)
    mlp_tf = 1024

    mlp_w1_b, mlp_w2_b = mlp_w1.astype(BF16), mlp_w2.astype(BF16)
    even_w_out_b, odd_w_out_b = even_w_out.astype(BF16), odd_w_out.astype(BF16)

    xx = jnp.concatenate([x, ctx], axis=1)
    for layer in range(depth):
        last = layer == depth - 1
        nw1 = norm_w[layer, 0].reshape(1, d)
        nw2 = norm_w[layer, 1].reshape(1, d)
        if layer % 2 == 0:
            e = layer // 2
            w_main, w_small, dtb, alog, dsk = _even_params(even_w_in[e], ssd_dt_bias[e], ssd_a_log[e], ssd_d[e])
            u, us = _inproj(xx, nw1, mods, layer, ctx_row, w_main, w_small, n_lat, tm_in, tn_even)
            ya = _ssd(u, us, ssd_conv_w[e], ssd_conv_b[e].reshape(1, -1), dtb, alog, dsk,
                      ssd_norm_w[e].reshape(SSD_GROUPS, 1, -1), n_lat, n_ctx)
            yb = _hgrn(u, lb_all[e].reshape(HGRN_HEADS, 1, -1), hgrn_norm_w[e].reshape(HGRN_HEADS, 1, -1), n_lat, n_ctx)
            ys, w_out, w_idx = [ya, yb], even_w_out_b, e
        else:
            o = layer // 2
            w_main, w_small, gb = _odd_params(odd_w_in[o], mlstm_gate_b[o])
            u, us = _inproj(xx, nw1, mods, layer, ctx_row, w_main, w_small, n_lat, tm_in, tn_odd)
            yc = _mlstm(u, us, mlstm_conv_w[o], mlstm_conv_b[o].reshape(1, -1), gb,
                        mlstm_norm_w[o].reshape(MLSTM_HEADS, 1, -1), n_lat, n_ctx)
            ys, w_out, w_idx = [yc], odd_w_out_b, o
        rows, tm = (n_lat, tm_lat) if last else (lt, tm_full)
        xx = _outproj(ys, w_out, w_idx, xx, mods, layer, ctx_row, n_lat, rows, tm // 2 if not last else tm)
        xx = _mlp(xx, nw2, mods, layer, ctx_row, mlp_w1_b, mlp_w2_b,
                  final_norm_w.reshape(1, d), n_lat, rows, tm, mlp_tf, last)
    return xx
```

```python
import functools

import numpy as np
import jax
import jax.numpy as jnp
from jax import lax
from jax.experimental import pallas as pl
from jax.experimental.pallas import tpu as pltpu

F32 = jnp.float32
BF16 = jnp.bfloat16

EPS = 1e-6
GRID_W = 64
N_MOD = 6
HGRN_LOG2_FLOOR = -1e5
C_ROWS = 32

V7X_VMEM_BYTES = 64 * 1024 * 1024
VMEM_LIMIT = V7X_VMEM_BYTES - 8 * 1024 * 1024

SSD_CHUNK = 128
SSD_HEAD_DIM = 64
SSD_STATE = 128
SSD_GROUPS = 2
HGRN_CHUNK = 64
HGRN_HEADS = 8
MLSTM_CHUNK = 128
MLSTM_HEADS = 4
MLSTM_DQK = 256
MLSTM_DV = 512
MLSTM_ONES_ROWS = 16
LANES = 128
SSD_GROUP_WIDTH = 8 * SSD_HEAD_DIM
SSD_GROUP_LANES = 2 * SSD_GROUP_WIDTH + 2 * SSD_STATE
HGRN_HEAD_LANES = 5 * LANES
CONV_PIECE = 256
NORM_ROWS = 16


def _cparams(*sem):
    return pltpu.CompilerParams(dimension_semantics=sem, vmem_limit_bytes=VMEM_LIMIT)


def _silu(x):
    return x * jax.nn.sigmoid(x)


def _softplus(x):
    return jnp.maximum(x, 0.0) + jnp.log(1.0 + jnp.exp(-jnp.abs(x)))


def _log_sigmoid(x):
    return -_softplus(-x)


def _dot(a, b):
    return jnp.dot(a, b, preferred_element_type=F32)


def _dot_nt(a, b):
    return lax.dot_general(a, b, (((1,), (1,)), ((), ())), preferred_element_type=F32)


def _dot_tn(a, b):
    return lax.dot_general(a, b, (((0,), (0,)), ((), ())), preferred_element_type=F32)


def _dot01(m01x3, x):
    hi = x.astype(BF16)
    r = x - hi.astype(F32)
    mid = r.astype(BF16)
    lo = (r - mid.astype(F32)).astype(BF16)
    return _dot(m01x3, jnp.concatenate([hi, mid, lo], axis=0))


def _dot01_rows(x, m01x3):
    hi = x.astype(BF16)
    r = x - hi.astype(F32)
    mid = r.astype(BF16)
    lo = (r - mid.astype(F32)).astype(BF16)
    return _dot(jnp.concatenate([hi, mid, lo], axis=1), m01x3)


def _rms(x):
    return x * lax.rsqrt(jnp.mean(x * x, axis=-1, keepdims=True) + EPS)


def _conv3(u, w, b, period):
    rows = u.shape[0]
    t = lax.broadcasted_iota(jnp.int32, u.shape, 0) & (period - 1)
    left = jnp.where(t == 0, 0.0, pltpu.roll(u, 1, axis=0))
    right = jnp.where(t == period - 1, 0.0, pltpu.roll(u, rows - 1, axis=0))
    return left * w[0:1] + u * w[1:2] + right * w[2:3] + b


def _modulated_norm(x_ref, h_ref, nw, sh_l, sc_l, sh_c, sc_c, row0, n_lat):
    a_l, a_c = nw * (1.0 + sc_l), nw * (1.0 + sc_c)

    def block(r, carry):
        start = pl.multiple_of(r * NORM_ROWS, NORM_ROWS)
        rs = pl.ds(start, NORM_ROWS)
        is_ctx = row0 + start >= n_lat
        h_ref[rs, :] = (_rms(x_ref[rs, :]) * jnp.where(is_ctx, a_c, a_l) + jnp.where(is_ctx, sh_c, sh_l)).astype(BF16)
        return carry

    lax.fori_loop(0, x_ref.shape[0] // NORM_ROWS, block, 0, unroll=4)


def _mod_kernel(c_ref, w_ref, b_ref, o_ref):
    a = _silu(c_ref[...]).astype(BF16)
    o_ref[...] = _dot(a, w_ref[...].astype(BF16)) + b_ref[...]


def _modulation(c_all, mod_w, mod_b):
    depth, d, n = mod_w.shape
    tn = 1024
    return pl.pallas_call(
        _mod_kernel,
        grid=(depth, n // tn),
        in_specs=[
            pl.BlockSpec((C_ROWS, d), lambda l, j: (0, 0)),
            pl.BlockSpec((None, d, tn), lambda l, j: (l, 0, j)),
            pl.BlockSpec((None, 1, tn), lambda l, j: (l, 0, j)),
        ],
        out_specs=pl.BlockSpec((None, C_ROWS, tn), lambda l, j: (l, 0, j)),
        out_shape=jax.ShapeDtypeStruct((depth, C_ROWS, n), F32),
        compiler_params=_cparams("parallel", "parallel"),
        name="modulation",
    )(c_all, mod_w, mod_b.reshape(depth, 1, n))


def _mod_specs(layer, ctx_row, ks, d, nargs):
    specs = []
    for k in ks:
        if nargs == 3:
            specs.append(pl.BlockSpec((None, None, None, 1, d), lambda b, i, j, k=k: (layer, b, k, 0, 0)))
            specs.append(pl.BlockSpec((None, None, None, 1, d), lambda b, i, j, k=k: (layer, ctx_row, k, 0, 0)))
        else:
            specs.append(pl.BlockSpec((None, None, None, 1, d), lambda b, i, k=k: (layer, b, k, 0, 0)))
            specs.append(pl.BlockSpec((None, None, None, 1, d), lambda b, i, k=k: (layer, ctx_row, k, 0, 0)))
    return specs


def _inproj_kernel(x_ref, nw_ref, shl_ref, shc_ref, scl_ref, scc_ref, w_ref, ws_ref, o_ref, os_ref, h_ref, *, n_lat):
    @pl.when(pl.program_id(2) == 0)
    def _():
        _modulated_norm(x_ref, h_ref, nw_ref[...], shl_ref[...], scl_ref[...], shc_ref[...], scc_ref[...],
                        pl.program_id(1) * x_ref.shape[0], n_lat)
        os_ref[...] = _dot(h_ref[...], ws_ref[...])

    o_ref[...] = _dot(h_ref[...], w_ref[...]).astype(o_ref.dtype)


def _inproj(x, nw, mods, layer, ctx_row, w_main, w_small, n_lat, tm, tn):
    b, lt, d = x.shape
    n = w_main.shape[1]
    ns = w_small.shape[1]
    return pl.pallas_call(
        functools.partial(_inproj_kernel, n_lat=n_lat),
        grid=(b, lt // tm, n // tn),
        in_specs=[
            pl.BlockSpec((None, tm, d), lambda b, i, j: (b, i, 0)),
            pl.BlockSpec((1, d), lambda b, i, j: (0, 0)),
            *_mod_specs(layer, ctx_row, (0, 1), d, 3),
            pl.BlockSpec((d, tn), lambda b, i, j: (0, j)),
            pl.BlockSpec((d, ns), lambda b, i, j: (0, 0)),
        ],
        out_specs=[
            pl.BlockSpec((None, tm, tn), lambda b, i, j: (b, i, j)),
            pl.BlockSpec((None, tm, ns), lambda b, i, j: (b, i, 0)),
        ],
        out_shape=[jax.ShapeDtypeStruct((b, lt, n), BF16), jax.ShapeDtypeStruct((b, lt, ns), F32)],
        scratch_shapes=[pltpu.VMEM((tm, d), BF16)],
        compiler_params=_cparams("parallel", "parallel", "arbitrary"),
        name="inproj",
    )(x, nw, mods, mods, mods, mods, w_main, w_small)


def _outproj_kernel(*refs, n_y, n_lat):
    y_refs, w_refs = refs[:n_y], refs[n_y:2 * n_y]
    x_ref, gl_ref, gc_ref, o_ref = refs[2 * n_y:]
    acc = _dot(y_refs[0][...], w_refs[0][...])
    for y_ref, w_ref in zip(y_refs[1:], w_refs[1:]):
        acc = acc + _dot(y_ref[...], w_ref[...])
    tm = x_ref.shape[0]
    rows = pl.program_id(1) * tm + lax.broadcasted_iota(jnp.int32, (tm, 1), 0)
    g = jnp.where(rows >= n_lat, gc_ref[...], gl_ref[...])
    o_ref[...] = x_ref[...] + g * acc


def _outproj(ys, w, w_idx, x, mods, layer, ctx_row, n_lat, rows, tm):
    b, lt, d = x.shape
    n_y = len(ys)
    y_specs = [pl.BlockSpec((None, tm, y.shape[2]), lambda b, i: (b, i, 0)) for y in ys]
    w_specs = [pl.BlockSpec((None, y.shape[2], d), lambda b, i, k=k: (w_idx, k, 0), pipeline_mode=pl.Buffered(1))
               for k, y in enumerate(ys)]
    return pl.pallas_call(
        functools.partial(_outproj_kernel, n_y=n_y, n_lat=n_lat),
        grid=(b, rows // tm),
        in_specs=[*y_specs, *w_specs,
                  pl.BlockSpec((None, tm, d), lambda b, i: (b, i, 0)),
                  *_mod_specs(layer, ctx_row, (2,), d, 2)],
        out_specs=pl.BlockSpec((None, tm, d), lambda b, i: (b, i, 0)),
        out_shape=jax.ShapeDtypeStruct((b, rows, d), F32),
        compiler_params=_cparams("parallel", "parallel"),
        name="outproj",
    )(*ys, *([w] * n_y), x, mods, mods)


def _mlp_kernel(x_ref, nw_ref, shl_ref, shc_ref, scl_ref, scc_ref, gl_ref, gc_ref, w1_ref, w2_ref, fw_ref,
                o_ref, h_ref, acc_ref, *, n_lat, final_norm):
    f = pl.program_id(2)
    tm = x_ref.shape[0]

    @pl.when(f == 0)
    def _():
        _modulated_norm(x_ref, h_ref, nw_ref[...], shl_ref[...], scl_ref[...], shc_ref[...], scc_ref[...],
                        pl.program_id(1) * tm, n_lat)
        acc_ref[...] = jnp.zeros_like(acc_ref)

    a = jnp.maximum(_dot(h_ref[...], w1_ref[...]), 0.0)
    acc_ref[...] += _dot((a * a).astype(BF16), w2_ref[...])

    @pl.when(f == pl.num_programs(2) - 1)
    def _():
        rows = pl.program_id(1) * tm + lax.broadcasted_iota(jnp.int32, (tm, 1), 0)
        g = jnp.where(rows >= n_lat, gc_ref[...], gl_ref[...])
        y = x_ref[...] + g * acc_ref[...]
        if final_norm:
            y = _rms(y) * fw_ref[...]
        o_ref[...] = y


def _mlp(x, nw, mods, layer, ctx_row, w1, w2, fw, n_lat, rows, tm, tf, final_norm):
    b, _, d = x.shape
    dff = w1.shape[2]
    return pl.pallas_call(
        functools.partial(_mlp_kernel, n_lat=n_lat, final_norm=final_norm),
        grid=(b, rows // tm, dff // tf),
        in_specs=[
            pl.BlockSpec((None, tm, d), lambda b, i, j: (b, i, 0)),
            pl.BlockSpec((1, d), lambda b, i, j: (0, 0)),
            *_mod_specs(layer, ctx_row, (3, 4, 5), d, 3),
            pl.BlockSpec((None, d, tf), lambda b, i, j: (layer, 0, j)),
            pl.BlockSpec((None, tf, d), lambda b, i, j: (layer, j, 0)),
            pl.BlockSpec((1, d), lambda b, i, j: (0, 0)),
        ],
        out_specs=pl.BlockSpec((None, tm, d), lambda b, i, j: (b, i, 0)),
        out_shape=jax.ShapeDtypeStruct((b, rows, d), F32),
        scratch_shapes=[pltpu.VMEM((tm, d), BF16), pltpu.VMEM((tm, d), F32)],
        compiler_params=_cparams("parallel", "parallel", "arbitrary"),
        name="mlp",
    )(x, nw, mods, mods, mods, mods, mods, mods, w1, w2, fw)


def _tri_consts(c):
    lower = np.tril(np.ones((c, c), np.float32))
    tri = np.stack([lower, lower.T])
    return jnp.asarray(np.concatenate([tri, tri, tri], axis=2), BF16)


def _chunk_index(j, d, n_lat_chunks, n_chunks):
    if d == 0:
        c = j + n_lat_chunks
        return jnp.where(c >= n_chunks, c - n_chunks, c)
    return n_chunks - 1 - j


def _conv_rows(n_lat, n_ctx, fn):
    def body(p, carry):
        fn(pl.multiple_of(p * CONV_PIECE, CONV_PIECE), CONV_PIECE, GRID_W)
        return carry
    lax.fori_loop(0, n_lat // CONV_PIECE, body, 0)
    fn(n_lat, n_ctx, n_ctx)


def _ssd_kernel(u_ref, dt_ref, cwx_ref, cwb_ref, cwc_ref, cbx_ref, cbb_ref, cbc_ref,
                dtb_ref, alog_ref, dsk_ref, nw_ref, tri_ref, o_ref,
                xst, bs, cst, cbs, dtt, cums, cumt, ecum, eend, etot, yst, hst, *, n_lat, n_ctx):
    ch = SSD_CHUNK
    p = SSD_HEAD_DIM
    gw = cwx_ref.shape[1]
    ns = cwb_ref.shape[1]
    heads = gw // p
    x_ref, b_ref = u_ref.at[:, 0:gw], u_ref.at[:, gw:gw + ns]
    c_ref, z_ref = u_ref.at[:, gw + ns:gw + 2 * ns], u_ref.at[:, gw + 2 * ns:]
    n_chunks = (n_lat + n_ctx) // ch
    n_lat_chunks = n_lat // ch

    def conv_piece(r0, rows, period):
        sl = pl.ds(r0, rows)
        c0 = r0 // ch
        x = _silu(_conv3(x_ref[sl, :].astype(F32), cwx_ref[...], cbx_ref[...], period))
        c = _silu(_conv3(c_ref[sl, :].astype(F32), cwc_ref[...], cbc_ref[...], period))
        bmat = _silu(_conv3(b_ref[sl, :].astype(F32), cwb_ref[...], cbb_ref[...], period)).astype(BF16)
        bs[sl, :] = bmat
        dt = _softplus(dt_ref[sl, :] + dtb_ref[...])
        a = -jnp.exp(alog_ref[...]) * dt
        nd = 2 * heads
        for k in range(rows // ch):
            rk = slice(k * ch, (k + 1) * ch)
            for j in range(gw // ch):
                xst[c0 + k, j * ch:(j + 1) * ch, :] = x[rk, j * ch:(j + 1) * ch].T
            cc_t = c[rk].T.astype(BF16)
            cst[c0 + k] = cc_t
            cbs[c0 + k] = _dot(bmat[rk], cc_t)
            dt_t = dt[rk].T
            dtt[c0 + k] = dt_t[0:nd]
            for d in range(2):
                cum = _dot01(tri_ref[d], a[rk])
                cum_t = cum.T
                tot_c = cum_t[:, ch - 1:ch] if d == 0 else cum_t[:, 0:1]
                cums[d, c0 + k] = cum
                cumt[d, c0 + k] = cum_t[0:nd]
                ecum[d, c0 + k] = jnp.exp(cum_t)[0:nd]
                eend[d, c0 + k] = (jnp.exp(tot_c - cum_t) * dt_t)[0:nd]
                etot[d, c0 + k] = jnp.broadcast_to(jnp.exp(cum[ch - 1:ch, :] if d == 0 else cum[0:1, :]), (8, ch))

    _conv_rows(n_lat, n_ctx, conv_piece)

    si = lax.broadcasted_iota(jnp.int32, (ch, ch), 0)
    ti = lax.broadcasted_iota(jnp.int32, (ch, ch), 1)

    for d in range(2):
        mask = (si <= ti) if d == 0 else (si >= ti)
        hst[...] = jnp.zeros_like(hst)

        def body(j, carry, d=d, mask=mask):
            c = _chunk_index(j, d, n_lat_chunks, n_chunks)
            sl = pl.ds(pl.multiple_of(c * ch, ch), ch)
            cum, cum_t = cums[d, c], cumt[d, c]
            e_cum_t, e_end_t, e_tot = ecum[d, c], eend[d, c], etot[d, c][0:1]
            dt_t = dtt[c]
            bc = bs[sl, :]
            cc_t = cst[c]
            cb_t = cbs[c]
            for h in range(heads):
                col = d * heads + h
                hs = slice(h * p, (h + 1) * p)
                diff = cum_t[col:col + 1, :] - cum[:, col:col + 1]
                decay = jnp.exp(jnp.where(mask, diff, -jnp.inf))
                xh = xst[c, hs, :]
                state = hst[h]
                y = _dot((xh * dt_t[col:col + 1, :]).astype(BF16), (cb_t * decay).astype(BF16))
                y = y + _dot(state.astype(BF16), cc_t) * e_cum_t[col:col + 1, :]
                xw = (xh * e_end_t[col:col + 1, :]).astype(BF16)
                hst[h] = state * e_tot[:, col:col + 1] + _dot(xw, bc)
                if d == 0:
                    yst[c, hs, :] = y
                else:
                    yst[c, hs, :] = yst[c, hs, :] + y
            if d == 1:
                g_t = yst[c] + xst[c] * dsk_ref[...]
                g = jnp.concatenate([g_t[j * ch:(j + 1) * ch].T for j in range(gw // ch)], axis=1)
                g = g * _silu(z_ref[sl, :].astype(F32))
                o_ref[sl, :] = (_rms(g) * nw_ref[...]).astype(BF16)
            return carry

        lax.fori_loop(0, n_chunks, body, 0, unroll=6)


def _ssd(u, us, conv_w, conv_b, dtb, alog, dsk, nw, n_lat, n_ctx):
    b, lt, _ = u.shape
    gw = SSD_GROUP_WIDTH
    ns = SSD_STATE
    nc = lt // SSD_CHUNK
    nd = 2 * gw // SSD_HEAD_DIM
    cx_blk, cb_blk, cc_blk = 0, SSD_GROUPS * gw // ns, SSD_GROUPS * (gw + ns) // ns
    par = lambda r, w, off: pl.BlockSpec((r, w), lambda b, g, off=off: (0, off + g))
    grp = lambda w: pl.BlockSpec((None, 1, w), lambda b, g: (g, 0, 0))
    return pl.pallas_call(
        functools.partial(_ssd_kernel, n_lat=n_lat, n_ctx=n_ctx),
        grid=(b, SSD_GROUPS),
        in_specs=[
            pl.BlockSpec((None, lt, SSD_GROUP_LANES), lambda b, g: (b, 0, g)),
            pl.BlockSpec((None, lt, LANES), lambda b, g: (b, 0, g)),
            par(3, gw, cx_blk), par(3, ns, cb_blk), par(3, ns, cc_blk),
            par(1, gw, cx_blk), par(1, ns, cb_blk), par(1, ns, cc_blk),
            grp(128), grp(128), pl.BlockSpec((None, gw, 128), lambda b, g: (g, 0, 0)), grp(gw),
            pl.BlockSpec((2, SSD_CHUNK, 3 * SSD_CHUNK), lambda b, g: (0, 0, 0)),
        ],
        out_specs=pl.BlockSpec((None, lt, gw), lambda b, g: (b, 0, g)),
        out_shape=jax.ShapeDtypeStruct((b, lt, SSD_GROUPS * gw), BF16),
        scratch_shapes=[
            pltpu.VMEM((nc, gw, SSD_CHUNK), F32), pltpu.VMEM((lt, ns), BF16), pltpu.VMEM((nc, ns, SSD_CHUNK), BF16),
            pltpu.VMEM((nc, SSD_CHUNK, SSD_CHUNK), F32), pltpu.VMEM((nc, nd, SSD_CHUNK), F32), pltpu.VMEM((2, nc, SSD_CHUNK, LANES), F32),
            pltpu.VMEM((2, nc, nd, SSD_CHUNK), F32), pltpu.VMEM((2, nc, nd, SSD_CHUNK), F32),
            pltpu.VMEM((2, nc, nd, SSD_CHUNK), F32), pltpu.VMEM((2, nc, 8, LANES), F32),
            pltpu.VMEM((nc, gw, SSD_CHUNK), F32), pltpu.VMEM((gw // SSD_HEAD_DIM, SSD_HEAD_DIM, ns), F32),
        ],
        compiler_params=_cparams("parallel", "parallel"),
        name="ssd",
    )(u, us, conv_w, conv_w, conv_w, conv_b, conv_b, conv_b, dtb, alog, dsk, nw, _tri_consts(SSD_CHUNK))


_HGRN_LEVELS = (32, 16, 8, 4, 2, 1)


def _hgrn_consts():
    c = HGRN_CHUNK
    sums = np.zeros((7, c, c), np.float32)
    pairs = np.zeros((7, c, c), np.float32)
    for li, m in enumerate(_HGRN_LEVELS):
        for t in range(c):
            beta = (t // (2 * m)) * 2 * m
            mid = beta + m
            if t >= mid:
                sums[li, t, mid:t + 1] = 1.0
                pairs[li, t, beta:mid] = 1.0
            else:
                sums[li, t, t + 1:mid] = 1.0
    sums[6] = np.tril(np.ones((c, c), np.float32))
    pairs[6] = np.eye(c, dtype=np.float32)
    sums = np.stack([sums, sums[:, ::-1, ::-1]]).reshape(2, 7 * c, c)
    sums = np.concatenate([sums, sums, sums, np.zeros_like(sums)], axis=2)
    pairs = np.stack([pairs, pairs[:, ::-1, ::-1]])
    zero = np.zeros_like(pairs)
    pairs = np.concatenate([np.concatenate([pairs, zero], axis=3), np.concatenate([zero, pairs], axis=3)], axis=2)
    return jnp.asarray(sums, BF16), jnp.asarray(pairs, F32)


def _hgrn_kernel(u_ref, lb_ref, nw_ref, sums_ref, pairs_ref, o_ref,
                 ys, qb_s, *dir_scratch, n_lat, n_ctx):
    dk = dv = u_ref.shape[1] // 5
    q_ref, ff_ref, fb_ref, i_ref, g_ref = (u_ref.at[:, k * dk:(k + 1) * dk] for k in range(5))
    w_s, kb_s, qd_s, kd_s, et_s, att_s, p_s = zip(dir_scratch[:7], dir_scratch[7:])
    ch = HGRN_CHUNK
    n_chunks = (n_lat + n_ctx) // ch
    n_lat_chunks = n_lat // ch
    n_pairs = n_chunks // 2
    n_lv = len(_HGRN_LEVELS)
    lb = lb_ref[...]
    one_m_lb = 1.0 - lb
    t_idx = lax.broadcasted_iota(jnp.int32, (ch, dk), 0)

    laters = [[((t_idx & m) != 0) == (d == 0) for m in _HGRN_LEVELS] for d in range(2)]

    def operands(d, pi):
        f_ref = ff_ref if d == 0 else fb_ref
        r0 = pl.multiple_of(pi * 2 * ch, 2 * ch)
        sl = pl.ds(r0, 2 * ch)
        gate = jax.nn.sigmoid(f_ref[sl, :].astype(F32))
        kin = one_m_lb * (1.0 - gate)
        logf2 = jnp.maximum(jnp.log2(lb + one_m_lb * gate), HGRN_LOG2_FLOOR)
        q = _silu(q_ref[sl, :].astype(F32))
        if d == 0:
            qb_s[sl, :] = q.astype(BF16)
        kb_s[d][sl, :] = kin.astype(BF16)
        lf = jnp.concatenate([logf2[:ch], logf2[ch:]], axis=1)
        hi = lf.astype(BF16)
        r1 = lf - hi.astype(F32)
        mid = r1.astype(BF16)
        lo = (r1 - mid.astype(F32)).astype(BF16)
        rel2 = _dot(sums_ref[d], jnp.concatenate([hi, mid, lo, jnp.zeros_like(hi)], axis=0))
        for half in range(2):
            rows = slice(half * ch, (half + 1) * ch)
            rel = rel2[:, half * dk:(half + 1) * dk]
            qh, kh = q[rows], kin[rows]
            c = pi * 2 + half
            for li in range(n_lv):
                e = jnp.exp2(rel[li * ch:(li + 1) * ch])
                w_s[d][pi, li, rows, :] = (jnp.where(laters[d][li], qh, kh) * e).astype(BF16)
            bcum = rel[n_lv * ch:(n_lv + 1) * ch]
            tot = bcum[ch - 1:ch] if d == 0 else bcum[0:1]
            hs = pl.ds(r0 + half * ch, ch)
            qd_s[d][hs, :] = (qh * jnp.exp2(bcum)).astype(BF16)
            kd_s[d][hs, :] = (kh * jnp.exp2(tot - bcum)).astype(BF16)
            et_s[d][c] = jnp.broadcast_to(jnp.exp2(tot), (8, dk))

    def intra(d, pi):
        sl = pl.ds(pl.multiple_of(pi * 2 * ch, 2 * ch), 2 * ch)
        att = pairs_ref[d, n_lv] * _dot_nt(qb_s[sl, :], kb_s[d][sl, :])
        for li in range(n_lv):
            w = w_s[d][pi, li]
            att = att + pairs_ref[d, li] * _dot_nt(w, w)
        att_s[d][sl, :] = att.astype(BF16)
        for half in range(2):
            hs = pl.ds(pl.multiple_of(pi * 2 * ch, 2 * ch) + half * ch, ch)
            p_s[d][pi * 2 + half] = _dot_tn(i_ref[hs, :], kd_s[d][hs, :])

    def scan(d, j, state_t):
        pi = _chunk_index(j, d, n_lat_chunks // 2, n_pairs)
        r0 = pl.multiple_of(pi * 2 * ch, 2 * ch)
        sl = pl.ds(r0, 2 * ch)
        o_intra = _dot(att_s[d][sl, :], i_ref[sl, :])
        o_halves = [None, None]
        for half in ((0, 1) if d == 0 else (1, 0)):
            hs = pl.ds(r0 + half * ch, ch)
            o_halves[half] = (o_intra[half * ch:(half + 1) * ch]
                              + _dot_nt(qd_s[d][hs, :], state_t.astype(BF16)))
            c = pi * 2 + half
            state_t = state_t * et_s[d][c][0:1] + p_s[d][c]
        o = jnp.concatenate(o_halves, axis=0)
        if d == 0:
            ys[sl, :] = o
        else:
            o = ys[sl, :] + o
            o_ref[sl, :] = (_rms(o) * nw_ref[...] * _silu(g_ref[sl, :].astype(F32))).astype(BF16)
        return state_t

    zero_state = jnp.zeros((dv, dk), F32)

    def stage_a(pi, carry):
        operands(0, pi)
        return carry

    def stage_b(pi, carry):
        intra(0, pi)
        operands(1, pi)
        return carry

    def stage_c(j, state_t):
        intra(1, j)
        return scan(0, j, state_t)

    lax.fori_loop(0, n_pairs, stage_a, 0, unroll=9)
    lax.fori_loop(0, n_pairs, stage_b, 0, unroll=9)
    lax.fori_loop(0, n_pairs, stage_c, zero_state, unroll=18)
    lax.fori_loop(0, n_pairs, functools.partial(scan, 1), zero_state, unroll=18)


def _hgrn(u, lb, nw, n_lat, n_ctx):
    b, lt, _ = u.shape
    w = LANES
    ch = HGRN_CHUNK
    n_chunks = lt // ch
    sums, pairs = _hgrn_consts()
    head = pl.BlockSpec((None, 1, w), lambda b, h: (h, 0, 0))
    return pl.pallas_call(
        functools.partial(_hgrn_kernel, n_lat=n_lat, n_ctx=n_ctx),
        grid=(b, HGRN_HEADS),
        in_specs=[pl.BlockSpec((None, lt, HGRN_HEAD_LANES),
                               lambda b, h: (b, 0, SSD_GROUPS * SSD_GROUP_LANES // HGRN_HEAD_LANES + h)), head, head,
                  pl.BlockSpec(sums.shape, lambda b, h: (0, 0, 0)),
                  pl.BlockSpec(pairs.shape, lambda b, h: (0, 0, 0, 0))],
        out_specs=pl.BlockSpec((None, lt, w), lambda b, h: (b, 0, h)),
        out_shape=jax.ShapeDtypeStruct((b, lt, HGRN_HEADS * w), BF16),
        scratch_shapes=[
            pltpu.VMEM((lt, w), F32), pltpu.VMEM((lt, w), BF16),
            *([pltpu.VMEM((n_chunks // 2, len(_HGRN_LEVELS), 2 * ch, w), BF16),
               pltpu.VMEM((lt, w), BF16), pltpu.VMEM((lt, w), BF16), pltpu.VMEM((lt, w), BF16),
               pltpu.VMEM((n_chunks, 8, w), F32), pltpu.VMEM((lt, 2 * ch), BF16),
               pltpu.VMEM((n_chunks, w, w), F32)] * 2),
        ],
        compiler_params=_cparams("parallel", "parallel"),
        name="hgrn2",
    )(u, lb, nw, sums, pairs)


def _mlstm_kernel(u_ref, gt_ref, cwq_ref, cwk_ref, cbq_ref, cbk_ref, gb_ref, nw_ref, tri_ref,
                  o_ref, qst, ks, kqs, vst, gst, bst, yst, cst, *, n_lat, n_ctx):
    ch = MLSTM_CHUNK
    n_chunks = (n_lat + n_ctx) // ch
    n_lat_chunks = n_lat // ch
    dqk = cwq_ref.shape[1]
    dv = nw_ref.shape[1]
    q_ref, k_ref = u_ref.at[:, 0:dqk], u_ref.at[:, dqk:2 * dqk]
    v_ref, og_ref = u_ref.at[:, 2 * dqk:2 * dqk + dv], u_ref.at[:, 2 * dqk + dv:]
    k_scale = dqk ** -0.5

    def conv_piece(r0, rows, period):
        sl = pl.ds(r0, rows)
        c0 = r0 // ch
        q = _silu(_conv3(q_ref[sl, :].astype(F32), cwq_ref[...], cbq_ref[...], period))
        kmat = (_silu(_conv3(k_ref[sl, :].astype(F32), cwk_ref[...], cbk_ref[...], period)) * k_scale).astype(BF16)
        ks[sl, :] = kmat
        raw = gt_ref[sl, :] + gb_ref[...]
        lane = lax.broadcasted_iota(jnp.int32, raw.shape, 1)
        g = jnp.where(lane < 2, raw, _log_sigmoid(raw))
        v = v_ref[sl, :].astype(F32)
        for k in range(rows // ch):
            rk = slice(k * ch, (k + 1) * ch)
            for j in range(dqk // ch):
                qst[c0 + k, j * ch:(j + 1) * ch, :] = q[rk, j * ch:(j + 1) * ch].T.astype(BF16)
            kqs[c0 + k] = _dot(kmat[rk], qst[c0 + k])
            for j in range(dv // ch):
                vst[c0 + k, j * ch:(j + 1) * ch, :] = v[rk, j * ch:(j + 1) * ch].T.astype(BF16)
            vst[c0 + k, dv:, :] = jnp.ones((MLSTM_ONES_ROWS, ch), BF16)
            g_t = g[rk].T
            gst[c0 + k] = g_t[0:8]
            for d in range(2):
                bst[d, c0 + k] = _dot01_rows(g_t, tri_ref[d])[0:8]

    _conv_rows(n_lat, n_ctx, conv_piece)

    si = lax.broadcasted_iota(jnp.int32, (ch, ch), 0)
    ti = lax.broadcasted_iota(jnp.int32, (ch, ch), 1)

    for d in range(2):
        mask = (si <= ti) if d == 0 else (si >= ti)
        cst[...] = jnp.zeros_like(cst)

        def body(j, m_prev, d=d, mask=mask):
            c = _chunk_index(j, d, n_lat_chunks, n_chunks)
            sl = pl.ds(pl.multiple_of(c * ch, ch), ch)
            brow = bst[d, c][2 + d:3 + d, :]
            irow = gst[c][d:d + 1, :]
            tot = brow[:, ch - 1:ch] if d == 0 else brow[:, 0:1]
            logd = jnp.where(mask, brow + jnp.broadcast_to(irow - brow, (ch, ch)).T, -jnp.inf)
            gstate = brow + m_prev
            mt = jnp.maximum(jnp.max(logd, axis=0, keepdims=True), gstate)
            q_t = qst[c]
            kc = ks[sl, :]
            w = kqs[c] * jnp.exp(logd - mt)
            sw = jnp.exp(gstate - mt)
            v_t = vst[c]
            state = cst[...]
            qstate = _dot(state.astype(BF16), q_t)
            num = _dot(v_t[:dv], w.astype(BF16)) + sw * qstate[:dv]
            den = jnp.sum(w, axis=0, keepdims=True) + sw * qstate[dv:dv + 1]
            hout = num * (1.0 / jnp.maximum(jnp.abs(den), jnp.exp(-mt)))
            logw = tot - brow + irow
            m_new = jnp.maximum(tot + m_prev, jnp.max(logw, axis=1, keepdims=True))
            ws = jnp.exp(logw - m_new).astype(BF16)
            cst[...] = jnp.exp(tot + m_prev - m_new) * state + _dot(v_t * ws, kc)
            if d == 0:
                yst[c] = hout
            else:
                hh_t = yst[c] + hout
                hh = jnp.concatenate([hh_t[i * ch:(i + 1) * ch].T for i in range(dv // ch)], axis=1)
                o_ref[sl, :] = (_rms(hh) * nw_ref[...] * jax.nn.sigmoid(og_ref[sl, :].astype(F32))).astype(BF16)
            return m_new

        lax.fori_loop(0, n_chunks, body, jnp.zeros((1, 1), F32), unroll=6)


def _mlstm(u, us, conv_w, conv_b, gate_b, nw, n_lat, n_ctx):
    b, lt, _ = u.shape
    dqk, dv = MLSTM_DQK, MLSTM_DV
    nh = MLSTM_HEADS
    ch = MLSTM_CHUNK
    nc = lt // ch
    upper = np.triu(np.ones((ch, ch), np.float32))
    tri = np.stack([upper, upper.T])
    tri = jnp.asarray(np.concatenate([tri, tri, tri], axis=1), BF16)
    par = lambda r, off: pl.BlockSpec((r, dqk), lambda b, h, off=off: (0, off + h))
    return pl.pallas_call(
        functools.partial(_mlstm_kernel, n_lat=n_lat, n_ctx=n_ctx),
        grid=(b, nh),
        in_specs=[
            pl.BlockSpec((None, lt, 2 * dqk + 2 * dv), lambda b, h: (b, 0, h)),
            pl.BlockSpec((None, lt, 128), lambda b, h: (b, 0, h)),
            par(3, 0), par(3, nh), par(1, 0), par(1, nh),
            pl.BlockSpec((None, 1, 128), lambda b, h: (h, 0, 0)),
            pl.BlockSpec((None, 1, dv), lambda b, h: (h, 0, 0)),
            pl.BlockSpec(tri.shape, lambda b, h: (0, 0, 0)),
        ],
        out_specs=pl.BlockSpec((None, lt, dv), lambda b, h: (b, 0, h)),
        out_shape=jax.ShapeDtypeStruct((b, lt, nh * dv), BF16),
        scratch_shapes=[
            pltpu.VMEM((nc, dqk, ch), BF16), pltpu.VMEM((lt, dqk), BF16), pltpu.VMEM((nc, ch, ch), F32),
            pltpu.VMEM((nc, dv + MLSTM_ONES_ROWS, ch), BF16), pltpu.VMEM((nc, 8, ch), F32),
            pltpu.VMEM((2, nc, 8, ch), F32),
            pltpu.VMEM((nc, dv, ch), F32), pltpu.VMEM((dv + MLSTM_ONES_ROWS, dqk), F32),
        ],
        compiler_params=_cparams("parallel", "parallel"),
        name="mlstm",
    )(u, us, conv_w, conv_w, conv_b, conv_b, gate_b, nw, tri)


def _pad_lanes(a, width=128):
    return jnp.pad(a, [(0, 0)] * (a.ndim - 1) + [(0, width - a.shape[-1])])


def _even_params(w_in, dt_bias, a_log, d_skip):
    heads = a_log.shape[1]
    hg = heads // SSD_GROUPS
    width = heads * SSD_HEAD_DIM
    bc = SSD_GROUPS * SSD_STATE
    o_x, o_b, o_c, o_dt = width, 2 * width, 2 * width + bc, 2 * width + 2 * bc
    o_hgrn = o_dt + 2 * heads
    w_hgrn = w_in[:, o_hgrn:].reshape(w_in.shape[0], 5, HGRN_HEADS, -1).swapaxes(1, 2)
    grp = lambda a, b: w_in[:, a:b].reshape(w_in.shape[0], SSD_GROUPS, -1)
    w_ssd = jnp.concatenate([grp(o_x, o_b), grp(o_b, o_c), grp(o_c, o_dt), grp(0, o_x)], axis=2)
    w_main = jnp.concatenate([w_ssd.reshape(w_in.shape[0], -1), w_hgrn.reshape(w_in.shape[0], -1)], axis=1).astype(BF16)
    dt_w = w_in[:, o_dt:o_hgrn]
    per_group = lambda a: [_pad_lanes(jnp.concatenate([a[..., g * hg:(g + 1) * hg], a[..., heads + g * hg:heads + (g + 1) * hg]], axis=-1))
                           for g in range(SSD_GROUPS)]
    w_small = jnp.concatenate(per_group(dt_w), axis=1).astype(BF16)
    flat = lambda a: a.reshape(1, 2 * heads)
    dtb = jnp.stack(per_group(flat(dt_bias)))
    alog = jnp.stack(per_group(flat(a_log)))
    dsk = jnp.broadcast_to(jnp.repeat(d_skip, SSD_HEAD_DIM).reshape(SSD_GROUPS, hg * SSD_HEAD_DIM, 1),
                           (SSD_GROUPS, hg * SSD_HEAD_DIM, LANES))
    return w_main, w_small, dtb, alog, dsk


def _odd_params(w_in, gate_b):
    nh = MLSTM_HEADS
    edges = np.cumsum([0, nh * MLSTM_DQK, nh * MLSTM_DQK, nh * MLSTM_DV, nh * MLSTM_DV])
    parts = [w_in[:, a:b].reshape(w_in.shape[0], nh, -1) for a, b in zip(edges[:-1], edges[1:])]
    w_main = jnp.concatenate(parts, axis=2).reshape(w_in.shape[0], -1).astype(BF16)
    gw = w_in[:, edges[-1]:]
    w_small = jnp.concatenate([_pad_lanes(gw[:, h::nh]) for h in range(nh)], axis=1).astype(BF16)
    gb = jnp.stack([_pad_lanes(gate_b[:, h].reshape(1, 4)) for h in range(nh)])
    return w_main, w_small, gb


def kernel(x, c, ctx, c_ctx, mod_w, mod_b, norm_w, final_norm_w, mlp_w1, mlp_w2, even_w_in, even_w_out, ssd_conv_w, ssd_conv_b, ssd_a_log, ssd_dt_bias, ssd_d, ssd_norm_w, hgrn_lb, hgrn_norm_w, odd_w_in, odd_w_out, mlstm_conv_w, mlstm_conv_b, mlstm_gate_b, mlstm_norm_w):
    bsz, n_lat, d = x.shape
    n_ctx = ctx.shape[1]
    depth = mod_w.shape[0]
    lt = n_lat + n_ctx
    assert bsz < C_ROWS and n_lat % CONV_PIECE == 0 and n_ctx % SSD_CHUNK == 0 and n_ctx & (n_ctx - 1) == 0
    ctx_row = bsz

    c_all = jnp.zeros((C_ROWS, d), F32).at[:bsz].set(c).at[ctx_row].set(c_ctx)
    mods = _modulation(c_all, mod_w, mod_b).reshape(depth, C_ROWS, N_MOD, 1, d)

    lb_all = jnp.cumsum(jax.nn.softmax(hgrn_lb.astype(F32), axis=0), axis=0)
    lb_all = lb_all - lb_all[0]

    tm_in, tn_even, tn_odd = lt // 2, 1536, 1024
    tm_full = lt // 4
    tm_lat = min(512, n_lat)
    mlp_tf = 1024

    mlp_w1_b, mlp_w2_b = mlp_w1.astype(BF16), mlp_w2.astype(BF16)
    even_w_out_b, odd_w_out_b = even_w_out.astype(BF16), odd_w_out.astype(BF16)

    xx = jnp.concatenate([x, ctx], axis=1)
    for layer in range(depth):
        last = layer == depth - 1
        nw1 = norm_w[layer, 0].reshape(1, d)
        nw2 = norm_w[layer, 1].reshape(1, d)
        if layer % 2 == 0:
            e = layer // 2
            w_main, w_small, dtb, alog, dsk = _even_params(even_w_in[e], ssd_dt_bias[e], ssd_a_log[e], ssd_d[e])
            u, us = _inproj(xx, nw1, mods, layer, ctx_row, w_main, w_small, n_lat, tm_in, tn_even)
            ya = _ssd(u, us, ssd_conv_w[e], ssd_conv_b[e].reshape(1, -1), dtb, alog, dsk,
                      ssd_norm_w[e].reshape(SSD_GROUPS, 1, -1), n_lat, n_ctx)
            yb = _hgrn(u, lb_all[e].reshape(HGRN_HEADS, 1, -1), hgrn_norm_w[e].reshape(HGRN_HEADS, 1, -1), n_lat, n_ctx)
            ys, w_out, w_idx = [ya, yb], even_w_out_b, e
        else:
            o = layer // 2
            w_main, w_small, gb = _odd_params(odd_w_in[o], mlstm_gate_b[o])
            u, us = _inproj(xx, nw1, mods, layer, ctx_row, w_main, w_small, n_lat, tm_in, tn_odd)
            yc = _mlstm(u, us, mlstm_conv_w[o], mlstm_conv_b[o].reshape(1, -1), gb,
                        mlstm_norm_w[o].reshape(MLSTM_HEADS, 1, -1), n_lat, n_ctx)
            ys, w_out, w_idx = [yc], odd_w_out_b, o
        rows, tm = (n_lat, tm_lat) if last else (lt, tm_full)
        xx = _outproj(ys, w_out, w_idx, xx, mods, layer, ctx_row, n_lat, rows, tm)
        xx = _mlp(xx, nw2, mods, layer, ctx_row, mlp_w1_b, mlp_w2_b,
                  final_norm_w.reshape(1, d), n_lat, rows, tm, mlp_tf, last)
    return xx
```
